```python
import math
import jax
import jax.numpy as jnp
from jax import lax
import numpy as np

D_MODEL = 1024
BATCH = 4
SEQ = 8192
DEPTH = 4

CTX_LEN = 256
GRID_W = 64
EPS = 1e-6
ROPE_BASE = 10000.0
BLOCK = 128

A_HEADS = 4
A_QK = 32
A_V = 2 * A_QK
A_W = A_HEADS * A_V
B_HEADS = 6
B_KV = 2
B_HD = 64
WINDOW = 128
B_W = B_HEADS * B_HD
C_HEADS = 4
C_DK = 48
C_DV = 96
C_GATE_RANK = 16
C_GATE_NORM = 16.0
C_CHUNK = 64
C_W = C_HEADS * C_DV
MIX_W = A_W + B_W + C_W

A_QK_W = A_HEADS * 2 * A_QK
B_KV_W = B_KV * B_HD
C_QK_W = C_HEADS * C_DK
IN_SIZES = (A_QK_W, A_QK_W, A_W, B_W, B_KV_W, B_KV_W, C_QK_W, C_QK_W, C_W, C_W, 2 * C_GATE_RANK)
IN_W = 2 * A_QK_W + A_W + B_W + 2 * B_KV_W + 2 * C_QK_W + 2 * C_W + 2 * C_GATE_RANK

D_FF = 2816
N_EXPERTS = 8
TOP_K = 2
D_FF_EXPERT = 3584

kernel_name = 'hybrid_diffattn_swa_gla_moe_prefix_trunk'


def rms_norm(x, g):
    xf = x.astype(jnp.float32)
    y = xf * lax.rsqrt(jnp.mean(xf * xf, axis=-1, keepdims=True) + EPS)
    return (y * g.astype(jnp.float32)).astype(x.dtype)


def axial_rope_tables(rows, dim):
    row = jnp.repeat(jnp.arange(rows, dtype=jnp.float32), GRID_W)
    col = jnp.tile(jnp.arange(GRID_W, dtype=jnp.float32), rows)
    quarter = dim // 4
    inv = ROPE_BASE ** (-jnp.arange(quarter, dtype=jnp.float32) / quarter)
    ang_r = row[:, None] * inv
    ang_c = col[:, None] * inv
    return (jnp.cos(ang_r), jnp.sin(ang_r), jnp.cos(ang_c), jnp.sin(ang_c))


def _rotate(x, cos, sin):
    x1, x2 = jnp.split(x, 2, axis=-1)
    return jnp.concatenate([x1 * cos - x2 * sin, x2 * cos + x1 * sin], axis=-1)


def apply_axial_rope(x, tabs):
    cr, sr, cc, sc = (t.astype(x.dtype) for t in tabs)
    xr, xc = jnp.split(x, 2, axis=-1)
    return jnp.concatenate([_rotate(xr, cr, sr), _rotate(xc, cc, sc)], axis=-1)


def diff_attention(q_l, k_l, v_l, q_c, k_c, v_c, lam_p, norm_g, lam_init, tabs, want_ctx):
    bsz, n_lat, _ = q_l.shape
    n_ctx = q_c.shape[1]

    def heads_qk(t, n):
        return t.reshape(bsz, n, A_HEADS, 2, A_QK).transpose(0, 2, 3, 1, 4)

    def heads_v(t, n):
        return t.reshape(bsz, n, A_HEADS, A_V).transpose(0, 2, 1, 3)

    ql = apply_axial_rope(heads_qk(q_l, n_lat), tabs)
    kl = apply_axial_rope(heads_qk(k_l, n_lat), tabs)
    kc, vc = heads_qk(k_c, n_ctx), heads_v(v_c, n_ctx)
    k_all = jnp.concatenate([kc, kl], axis=3)
    v_all = jnp.concatenate([vc, heads_v(v_l, n_lat)], axis=2)
    lp = lam_p.astype(jnp.float32)
    lam = jnp.exp(jnp.sum(lp[0] * lp[1])) - jnp.exp(jnp.sum(lp[2] * lp[3])) + lam_init
    scale = A_QK ** -0.5

    def attend(qb, keys, vals):
        s = jnp.einsum('bhmqd,bhmkd->bhmqk', qb, keys).astype(jnp.float32) * scale
        p = jax.nn.softmax(s, axis=-1)
        p = p[:, :, 0] - lam * p[:, :, 1]
        return jnp.einsum('bhqk,bhkd->bhqd', p.astype(vals.dtype), vals)

    nb = n_lat // BLOCK
    q_blocks = jnp.moveaxis(ql.reshape(bsz, A_HEADS, 2, nb, BLOCK, A_QK), 3, 0)
    o_l = lax.map(lambda qb: attend(qb, k_all, v_all), q_blocks)
    o_l = jnp.moveaxis(o_l, 0, 2).reshape(bsz, A_HEADS, n_lat, A_V)

    def finish(o, n):
        o = rms_norm(o, norm_g) * (1.0 - lam_init)
        return o.transpose(0, 2, 1, 3).reshape(bsz, n, A_W)

    out_l = finish(o_l, n_lat)
    out_c = finish(attend(heads_qk(q_c, n_ctx), kc, vc), n_ctx) if want_ctx else None
    return out_l, out_c


def window_attention(q_l, k_l, v_l, q_c, k_c, v_c, sink, tabs, want_ctx):
    bsz, n_lat, _ = q_l.shape
    n_ctx = q_c.shape[1]
    rep = B_HEADS // B_KV

    def heads_q(t, n):
        return t.reshape(bsz, n, B_KV, rep, B_HD).transpose(0, 2, 3, 1, 4)

    def heads_kv(t, n):
        return t.reshape(bsz, n, B_KV, B_HD).transpose(0, 2, 1, 3)

    ql = apply_axial_rope(heads_q(q_l, n_lat), tabs)
    kl = apply_axial_rope(heads_kv(k_l, n_lat), tabs)
    vl = heads_kv(v_l, n_lat)
    kc, vc = heads_kv(k_c, n_ctx), heads_kv(v_c, n_ctx)
    pad = ((0, 0), (0, 0), (WINDOW, WINDOW), (0, 0))
    kp, vp = jnp.pad(kl, pad), jnp.pad(vl, pad)
    sink_l = sink.astype(jnp.float32).reshape(B_KV, rep, 1, 1)
    scale = B_HD ** -0.5
    span = BLOCK + 2 * WINDOW

    def attend(qb, keys, vals, mask):
        s = jnp.einsum('bgrqd,bgkd->bgrqk', qb, keys).astype(jnp.float32) * scale
        s = jnp.where(mask, s, -jnp.inf)
        s = jnp.concatenate([s, jnp.broadcast_to(sink_l, s.shape[:-1] + (1,))], axis=-1)
        p = jax.nn.softmax(s, axis=-1)[..., :-1]
        return jnp.einsum('bgrqk,bgkd->bgrqd', p.astype(vals.dtype), vals)

    def latent_block(args):
        qb, n = args
        start = n * BLOCK
        kw = lax.dynamic_slice_in_dim(kp, start, span, axis=2)
        vw = lax.dynamic_slice_in_dim(vp, start, span, axis=2)
        qpos = start + jnp.arange(BLOCK)
        kpos = start - WINDOW + jnp.arange(span)
        win = (jnp.abs(qpos[:, None] - kpos[None, :]) <= WINDOW) & (kpos >= 0) & (kpos < n_lat)
        mask = jnp.concatenate([jnp.ones((BLOCK, n_ctx), dtype=bool), win], axis=1)
        return attend(qb, jnp.concatenate([kc, kw], axis=2), jnp.concatenate([vc, vw], axis=2), mask)

    nb = n_lat // BLOCK
    q_blocks = jnp.moveaxis(ql.reshape(bsz, B_KV, rep, nb, BLOCK, B_HD), 3, 0)
    o = lax.map(latent_block, (q_blocks, jnp.arange(nb)))
    o = jnp.moveaxis(o, 0, 3).reshape(bsz, B_KV, rep, n_lat, B_HD)
    out_l = o.transpose(0, 3, 1, 2, 4).reshape(bsz, n_lat, B_W)
    out_c = None
    if want_ctx:
        oc = attend(heads_q(q_c, n_ctx), kc, vc, True)
        out_c = oc.transpose(0, 3, 1, 2, 4).reshape(bsz, n_ctx, B_W)
    return out_l, out_c


def gla_chunked(q, k, v, log_a, s0):
    bsz, nh, length, _ = q.shape
    dv = v.shape[-1]
    n = length // C_CHUNK

    def chunks(t):
        return jnp.moveaxis(t.reshape(bsz, nh, n, C_CHUNK, t.shape[-1]).astype(jnp.float32), 2, 0)

    qc, kc, vc, ac = chunks(q), chunks(k), chunks(v), chunks(log_a)
    b = jnp.cumsum(ac, axis=3)
    b_last = b[:, :, :, -1:, :]
    q_dec = qc * jnp.exp(b)
    k_inv = kc * jnp.exp(-b)
    k_end = kc * jnp.exp(b_last - b)
    causal = jnp.tril(jnp.ones((C_CHUNK, C_CHUNK), dtype=bool))
    att = jnp.where(causal, jnp.einsum('nbhcd,nbhsd->nbhcs', q_dec, k_inv), 0.0)
    o_intra = jnp.einsum('nbhcs,nbhse->nbhce', att, vc)
    upd = jnp.einsum('nbhcd,nbhce->nbhde', k_end, vc)
    decay = jnp.exp(b_last[:, :, :, 0, :])

    def step(state, xs):
        dec, u = xs
        return dec[..., None] * state + u, state

    s_final, s_in = lax.scan(step, s0, (decay, upd))
    o_inter = jnp.einsum('nbhcd,nbhde->nbhce', q_dec, s_in)
    o = jnp.moveaxis(o_intra + o_inter, 0, 2).reshape(bsz, nh, length, dv)
    return o.astype(v.dtype), s_final


def gla_mixer(parts_l, parts_c, w2, bg, norm_g, want_ctx):
    def prep(parts):
        q, k, v, r, g_low = parts
        bsz, n, _ = q.shape

        def heads(t):
            return t.reshape(bsz, n, C_HEADS, -1).transpose(0, 2, 1, 3)

        gates = [heads(jax.nn.log_sigmoid(
            (g_low[..., d * C_GATE_RANK:(d + 1) * C_GATE_RANK] @ w2[d] + bg[d]).astype(jnp.float32)) / C_GATE_NORM)
            for d in range(2)]
        return heads(q * C_DK ** -0.5), heads(k), heads(v), r, gates

    def flip(t):
        return jnp.flip(t, axis=2)

    def finish(o, r):
        bsz, _, n, _ = o.shape
        o = rms_norm(o, norm_g).transpose(0, 2, 1, 3).reshape(bsz, n, C_W)
        return o * jax.nn.silu(r)

    qc, kc, vc, rc, gc = prep(parts_c)
    s0 = jnp.zeros(qc.shape[:2] + (C_DK, C_DV), jnp.float32)
    o_cf, s_f = gla_chunked(qc, kc, vc, gc[0], s0)
    o_cb, s_b = gla_chunked(flip(qc), flip(kc), flip(vc), flip(gc[1]), s0)
    ql, kl, vl, rl, gl = prep(parts_l)
    o_lf, _ = gla_chunked(ql, kl, vl, gl[0], s_f)
    o_lb, _ = gla_chunked(flip(ql), flip(kl), flip(vl), flip(gl[1]), s_b)
    out_l = finish(o_lf + flip(o_lb), rl)
    out_c = finish(o_cf + flip(o_cb), rc) if want_ctx else None
    return out_l, out_c


def swiglu(h, wg, wu, wd):
    return (jax.nn.silu(h @ wg) * (h @ wu)) @ wd


def moe_ffn(h, router, wg, wu, wd):
    logits = (h @ router).astype(jnp.float32)
    top_v, top_i = lax.top_k(logits, TOP_K)
    top_w = jax.nn.softmax(top_v, axis=-1)
    combine = jnp.sum(jax.nn.one_hot(top_i, N_EXPERTS, dtype=jnp.float32) * top_w[..., None], axis=-2)
    out = jnp.zeros_like(h)
    for e in range(N_EXPERTS):
        out = out + combine[..., e:e + 1].astype(h.dtype) * swiglu(h, wg[e], wu[e], wd[e])
    return out


def setup_inputs(seed: int = 0) -> dict:
    key = jax.random.key(seed)
    ks = jax.random.split(key, 26)
    f32 = jnp.float32
    n_dense = (DEPTH + 1) // 2
    n_moe = DEPTH // 2

    def nrm(k, shape, fan_in, gain=1.0):
        return jax.random.normal(k, shape, f32) * (gain * fan_in ** -0.5)

    def gain(k, shape):
        return 1.0 + 0.05 * jax.random.normal(k, shape, f32)

    return {
        'x': jax.random.normal(ks[0], (BATCH, SEQ, D_MODEL), f32),
        'c': jax.random.normal(ks[1], (BATCH, D_MODEL), f32),
        'ctx': jax.random.normal(ks[2], (BATCH, CTX_LEN, D_MODEL), f32),
        'c_ctx': jax.random.normal(ks[3], (D_MODEL,), f32),
        'norm1_g': gain(ks[4], (DEPTH, D_MODEL)),
        'norm2_g': gain(ks[5], (DEPTH, D_MODEL)),
        'ada_w': nrm(ks[6], (DEPTH, D_MODEL, 6 * D_MODEL), D_MODEL, 0.5),
        'ada_b': 0.02 * jax.random.normal(ks[7], (DEPTH, 6 * D_MODEL), f32),
        'w_in': nrm(ks[8], (DEPTH, D_MODEL, IN_W), D_MODEL),
        'w_out': nrm(ks[9], (DEPTH, MIX_W, D_MODEL), MIX_W),
        'a_lambda': 0.1 * jax.random.normal(ks[10], (DEPTH, 4, A_QK), f32),
        'a_norm_g': gain(ks[11], (DEPTH, A_V)),
        'b_sink': jax.random.normal(ks[12], (DEPTH, B_HEADS), f32),
        'c_gate_w2': nrm(ks[13], (DEPTH, 2, C_GATE_RANK, C_HEADS * C_DK), C_GATE_RANK),
        'c_gate_b': 1.0 + 0.1 * jax.random.normal(ks[14], (DEPTH, 2, C_HEADS * C_DK), f32),
        'c_norm_g': gain(ks[15], (DEPTH, C_DV)),
        'ffn_w_gate': nrm(ks[16], (n_dense, D_MODEL, D_FF), D_MODEL),
        'ffn_w_up': nrm(ks[17], (n_dense, D_MODEL, D_FF), D_MODEL),
        'ffn_w_down': nrm(ks[18], (n_dense, D_FF, D_MODEL), D_FF),
        'moe_router': nrm(ks[19], (n_moe, D_MODEL, N_EXPERTS), D_MODEL),
        'moe_w_gate': nrm(ks[20], (n_moe, N_EXPERTS, D_MODEL, D_FF_EXPERT), D_MODEL),
        'moe_w_up': nrm(ks[21], (n_moe, N_EXPERTS, D_MODEL, D_FF_EXPERT), D_MODEL),
        'moe_w_down': nrm(ks[22], (n_moe, N_EXPERTS, D_FF_EXPERT, D_MODEL), D_FF_EXPERT),
        'final_g': gain(ks[23], (D_MODEL,)),
    }


def reference(x, c, ctx, c_ctx, norm1_g, norm2_g, ada_w, ada_b, w_in, w_out, a_lambda, a_norm_g,
              b_sink, c_gate_w2, c_gate_b, c_norm_g, ffn_w_gate, ffn_w_up, ffn_w_down,
              moe_router, moe_w_gate, moe_w_up, moe_w_down, final_g):
    n_lat = x.shape[1]
    n_ctx = ctx.shape[1]
    rows = n_lat // GRID_W
    tabs_a = axial_rope_tables(rows, A_QK)
    tabs_b = axial_rope_tables(rows, B_HD)
    split_at = [int(v) for v in np.cumsum(IN_SIZES)[:-1]]
    silu_c = jax.nn.silu(c)
    silu_cc = jax.nn.silu(c_ctx)
    xl, xc = x, ctx
    for layer in range(DEPTH):
        last = layer == DEPTH - 1
        lam_init = 0.8 - 0.6 * math.exp(-0.3 * layer)
        mod_l = (silu_c @ ada_w[layer] + ada_b[layer])[:, None, :]
        mod_c = (silu_cc @ ada_w[layer] + ada_b[layer])[None, None, :]
        sh1_l, sc1_l, g1_l, sh2_l, sc2_l, g2_l = jnp.split(mod_l, 6, axis=-1)
        sh1_c, sc1_c, g1_c, sh2_c, sc2_c, g2_c = jnp.split(mod_c, 6, axis=-1)

        h_l = rms_norm(xl, norm1_g[layer]) * (1.0 + sc1_l) + sh1_l
        h_c = rms_norm(xc, norm1_g[layer]) * (1.0 + sc1_c) + sh1_c
        proj = jnp.concatenate([h_c, h_l], axis=1) @ w_in[layer]
        pc = jnp.split(proj[:, :n_ctx], split_at, axis=-1)
        pl = jnp.split(proj[:, n_ctx:], split_at, axis=-1)
        a_l, a_c = diff_attention(pl[0], pl[1], pl[2], pc[0], pc[1], pc[2], a_lambda[layer],
                                  a_norm_g[layer], lam_init, tabs_a, not last)
        b_l, b_c = window_attention(pl[3], pl[4], pl[5], pc[3], pc[4], pc[5], b_sink[layer],
                                    tabs_b, not last)
        g_l, g_c = gla_mixer(pl[6:11], pc[6:11], c_gate_w2[layer], c_gate_b[layer],
                             c_norm_g[layer], not last)
        xl = xl + g1_l * (jnp.concatenate([a_l, b_l, g_l], axis=-1) @ w_out[layer])
        if not last:
            xc = xc + g1_c * (jnp.concatenate([a_c, b_c, g_c], axis=-1) @ w_out[layer])

        h2_l = rms_norm(xl, norm2_g[layer]) * (1.0 + sc2_l) + sh2_l
        if last:
            tok = h2_l
        else:
            h2_c = rms_norm(xc, norm2_g[layer]) * (1.0 + sc2_c) + sh2_c
            tok = jnp.concatenate([h2_c, h2_l], axis=1)
        j = layer // 2
        if layer % 2 == 0:
            y = swiglu(tok, ffn_w_gate[j], ffn_w_up[j], ffn_w_down[j])
        else:
            y = moe_ffn(tok, moe_router[j], moe_w_gate[j], moe_w_up[j], moe_w_down[j])
        xl = xl + g2_l * y[:, y.shape[1] - n_lat:]
        if not last:
            xc = xc + g2_c * y[:, :n_ctx]
    return rms_norm(xl, final_g)
```

```python
import functools
import math

import numpy as np
import jax
import jax.numpy as jnp
from jax import lax
from jax.experimental import pallas as pl
from jax.experimental.pallas import tpu as pltpu

F32 = jnp.float32
BF16 = jnp.bfloat16

EPS = 1e-6
ROPE_BASE = 10000.0
GRID_W = 64
LANES = 128
LOG2E = math.log2(math.e)
NEG = -1e30

A_HEADS, A_QK, A_V = 4, 32, 64
B_HEADS, B_KV, B_HD, WINDOW, BLOCK = 6, 2, 64, 128, 128
C_HEADS, C_DK, C_DV, C_RANK, C_GATE_NORM, C_CHUNK = 4, 48, 96, 16, 16.0, 64
N_EXPERTS = 8
IN_SIZES = (256, 256, 256, 384, 128, 128, 192, 192, 384, 384, 32)
IN_W = sum(IN_SIZES)

ROPE_WIDTHS = (256, 256, 768, 128)
PLAIN_WIDTHS = (256, 128, 512, 512, 512, 512, 128)
ROPE_W = sum(ROPE_WIDTHS)
GLA_GROUP = 256
VMEM_LIMIT = 48 * 1024 * 1024


def _cparams(sem):
    return pltpu.CompilerParams(dimension_semantics=sem, vmem_limit_bytes=VMEM_LIMIT)


def _column_maps():
    off = np.concatenate([[0], np.cumsum(IN_SIZES)])
    aq0, ak0, av0, bq0, bk0, bv0, cq0, ck0, cv0, cr0, cg0 = [int(v) for v in off[:11]]
    zero = IN_W

    def a_partner(d):
        return d + 8 if d % 16 < 8 else d - 8

    def b_partner(d):
        return d + 16 if d % 32 < 16 else d - 16

    main, part, dim, dd, scale = [], [], [], [], []
    a_scale = A_QK ** -0.5 * LOG2E
    b_scale = B_HD ** -0.5 * LOG2E
    for base, sc in ((aq0, a_scale), (ak0, 1.0)):
        for j in range(256):
            d = j % 32
            main.append(base + j); part.append(base + j - d + a_partner(d))
            dim.append(32); dd.append(d); scale.append(sc)
    for t in range(B_HEADS):
        g = t // (B_HEADS // B_KV)
        for lane in range(LANES):
            d = lane % 64
            if lane // 64 == g:
                main.append(bq0 + t * 64 + d); part.append(bq0 + t * 64 + b_partner(d))
            else:
                main.append(zero); part.append(zero)
            dim.append(64); dd.append(d); scale.append(b_scale)
    for j in range(128):
        d = j % 64
        main.append(bk0 + j); part.append(bk0 + j - d + b_partner(d))
        dim.append(64); dd.append(d); scale.append(1.0)

    plain = list(range(av0, av0 + 256)) + list(range(bv0, bv0 + 128))
    for base in (cq0, ck0):
        for h in range(C_HEADS):
            plain += [base + h * C_DK + d if d < C_DK else zero for d in range(LANES)]
    for base in (cv0, cr0):
        for h in range(C_HEADS):
            plain += [base + h * C_DV + d if d < C_DV else zero for d in range(LANES)]
    plain += [cg0 + d if d < 2 * C_RANK else zero for d in range(LANES)]
    return (np.array(main + part, np.int32), np.array(plain, np.int32),
            np.array(dim), np.array(dd), np.array(scale, np.float32))


_ROPE_COLS, _PLAIN_COLS, _R_DIM, _R_D, _R_SCALE = _column_maps()


def _rope_tables(seq, pad_rows):
    pos = jnp.arange(seq, dtype=jnp.int32)
    row = (pos // GRID_W).astype(F32)[:, None]
    col = (pos % GRID_W).astype(F32)[:, None]
    quarter = _R_DIM // 4
    half = _R_DIM // 2
    is_col = (_R_D % _R_DIM) >= half
    ddh = _R_D % half
    first = ddh < quarter
    f = (ddh % quarter).astype(np.float32)
    inv = jnp.asarray(ROPE_BASE, F32) ** (-jnp.asarray(f) / jnp.asarray(quarter.astype(np.float32)))
    ang = jnp.where(jnp.asarray(is_col)[None, :], col, row) * inv[None, :]
    scale = jnp.asarray(_R_SCALE)[None, :]
    cos = jnp.cos(ang) * scale
    sin = jnp.where(jnp.asarray(first)[None, :], -jnp.sin(ang), jnp.sin(ang)) * scale
    cos = jnp.concatenate([cos, jnp.broadcast_to(scale, (pad_rows, ROPE_W))], axis=0)
    sin = jnp.concatenate([sin, jnp.zeros((pad_rows, ROPE_W), F32)], axis=0)
    return cos, sin


def _mod_kernel(c_ref, w_ref, b_ref, o_ref):
    c = c_ref[...]
    s = c * (1.0 / (1.0 + jnp.exp(-c)))
    o_ref[0] = jnp.dot(s, w_ref[0], precision=lax.Precision.HIGHEST,
                       preferred_element_type=F32) + b_ref[0]


def _modulation(cc, ada_w, ada_b):
    depth, d, n = ada_w.shape
    tn = n // 4
    return pl.pallas_call(
        _mod_kernel,
        grid=(depth, n // tn),
        in_specs=[pl.BlockSpec((8, d), lambda l, j: (0, 0)),
                  pl.BlockSpec((1, d, tn), lambda l, j: (l, 0, j)),
                  pl.BlockSpec((1, 1, tn), lambda l, j: (l, 0, j))],
        out_specs=pl.BlockSpec((1, 8, tn), lambda l, j: (l, 0, j)),
        out_shape=jax.ShapeDtypeStruct((depth, 8, n), F32),
        compiler_params=_cparams(("arbitrary", "arbitrary")),
        name="adaln_mod",
    )(cc, ada_w, ada_b.reshape(depth, 1, n))


def _norm_mod(x, g, sc, sh):
    ms = jnp.mean(x * x, axis=-1, keepdims=True)
    return (x * lax.rsqrt(ms + EPS) * g) * (1.0 + sc) + sh


def _normproj_kernel(x_ref, g_ref, sc_ref, sh_ref, w_ref, *rest, rope, widths):
    if rope:
        c_ref, s_ref = rest[:2]
        outs = rest[2:]
    else:
        outs = rest
    h = _norm_mod(x_ref[...], g_ref[...], sc_ref[0], sh_ref[0])
    acc = jnp.dot(h.astype(BF16), w_ref[...], preferred_element_type=F32)
    if rope:
        rw = sum(widths)
        acc = acc[:, :rw] * c_ref[...] + acc[:, rw:] * s_ref[...]
    off = 0
    for o_ref, w in zip(outs, widths):
        o_ref[...] = acc[:, off:off + w].astype(o_ref.dtype)
        off += w


def _normproj(xs, g, mod, sc_chunk, sh_chunk, w, widths, tm, n_lat_tiles, tiles_per_batch, nb, tables=None):
    t, d = xs.shape
    rope = tables is not None

    def modrow(i):
        return jnp.where(i < n_lat_tiles, i // tiles_per_batch, nb)

    in_specs = [pl.BlockSpec((tm, d), lambda i: (i, 0)),
                pl.BlockSpec((1, d), lambda i: (0, 0)),
                pl.BlockSpec((1, 1, d), lambda i: (modrow(i), 0, sc_chunk)),
                pl.BlockSpec((1, 1, d), lambda i: (modrow(i), 0, sh_chunk)),
                pl.BlockSpec(w.shape, lambda i: (0, 0))]
    args = [xs, g.reshape(1, d), mod, mod, w]
    if rope:
        def tabrow(i):
            return jnp.where(i < n_lat_tiles, i % tiles_per_batch, tiles_per_batch)
        in_specs += [pl.BlockSpec((tm, ROPE_W), lambda i: (tabrow(i), 0))] * 2
        args += list(tables)
    return pl.pallas_call(
        functools.partial(_normproj_kernel, rope=rope, widths=widths),
        grid=(t // tm,),
        in_specs=in_specs,
        out_specs=[pl.BlockSpec((tm, wd), lambda i: (i, 0)) for wd in widths],
        out_shape=[jax.ShapeDtypeStruct((t, wd), BF16) for wd in widths],
        compiler_params=_cparams(("parallel",)),
        name="normproj_rope" if rope else "normproj_plain",
    )(*args)


def _diff_attn_kernel(*refs, lam_init, has_lat, tk):
    if has_lat:
        q_ref, kc_ref, vtc_ref, kl_ref, vtl_ref, lam_ref, g_ref, o_ref = refs
    else:
        q_ref, kc_ref, vtc_ref, lam_ref, g_ref, o_ref = refs
    q = q_ref[...]
    tq = q.shape[0]
    lane = lax.broadcasted_iota(jnp.int32, (1, LANES), 1)
    lp = lam_ref[...]
    lam = (jnp.exp(jnp.sum(lp[0:1] * lp[1:2], axis=1, keepdims=True))
           - jnp.exp(jnp.sum(lp[2:3] * lp[3:4], axis=1, keepdims=True)) + lam_init)
    nt = (((1,), (1,)), ((), ()))
    heads = []
    for hh in range(2):
        maps = []
        for m in range(2):
            idx = hh * 2 + m
            qm = jnp.where((lane >= idx * A_QK) & (lane < (idx + 1) * A_QK), q, jnp.zeros_like(q))

            def chunk(k, vt, carry, qm=qm):
                m_run, acc = carry
                s = lax.dot_general(k, qm, nt, preferred_element_type=F32)
                m_new = jnp.maximum(m_run, jnp.max(s, axis=0, keepdims=True))
                alpha = jnp.exp2(m_run - m_new)
                p = jnp.exp2(s - m_new).astype(BF16)
                return m_new, alpha * acc + jnp.dot(vt, p, preferred_element_type=F32)

            carry = (jnp.full((1, tq), NEG, F32), jnp.zeros((A_V + 8, tq), F32))
            carry = chunk(kc_ref[...], vtc_ref[0, hh], carry)
            if has_lat:
                def body(c, carry, hh=hh, chunk=chunk):
                    st = pl.multiple_of(c * tk, tk)
                    return chunk(kl_ref[pl.ds(st, tk), :], vtl_ref[0, hh, :, pl.ds(st, tk)], carry)
                carry = lax.fori_loop(0, kl_ref.shape[0] // tk, body, carry)
            acc = carry[1]
            maps.append(acc[0:A_V] / acc[A_V:A_V + 1])
        oh = maps[0] - lam * maps[1]
        ms = jnp.mean(oh * oh, axis=0, keepdims=True)
        heads.append(oh * lax.rsqrt(ms + EPS) * g_ref[...] * (1.0 - lam_init))
    o_ref[...] = jnp.concatenate(heads, axis=0).T.astype(o_ref.dtype)


def _diff_attn(aq, ak, vt_ctx, vt_lat, lam_p, norm_g, lam_init, nb, seq, ctx, latent):
    tq = 256
    tk = 512
    kern = functools.partial(_diff_attn_kernel, lam_init=lam_init, has_lat=latent, tk=tk)
    ctx_blk0 = nb * seq // ctx
    vtc_spec = pl.BlockSpec((1, 2, A_V + 8, ctx), lambda b, p, i: (b, p, 0, 0))
    kc_spec = pl.BlockSpec((ctx, LANES), lambda b, p, i: (ctx_blk0 + b, p))
    par_specs = [pl.BlockSpec((4, A_QK), lambda b, p, i: (0, 0)),
                 pl.BlockSpec((A_V, 1), lambda b, p, i: (0, 0))]
    if latent:
        nq = seq // tq
        in_specs = [pl.BlockSpec((tq, LANES), lambda b, p, i: (b * nq + i, p)), kc_spec, vtc_spec,
                    pl.BlockSpec((seq, LANES), lambda b, p, i: (b, p)),
                    pl.BlockSpec((1, 2, A_V + 8, seq), lambda b, p, i: (b, p, 0, 0))] + par_specs
        args = (aq, ak, vt_ctx, ak, vt_lat, lam_p, norm_g.reshape(A_V, 1))
        rows = nb * seq
    else:
        nq = ctx // tq
        q_blk0 = nb * seq // tq
        in_specs = [pl.BlockSpec((tq, LANES), lambda b, p, i: (q_blk0 + b * nq + i, p)),
                    kc_spec, vtc_spec] + par_specs
        args = (aq, ak, vt_ctx, lam_p, norm_g.reshape(A_V, 1))
        rows = nb * ctx
    return pl.pallas_call(
        kern,
        grid=(nb, 2, nq),
        in_specs=in_specs,
        out_specs=pl.BlockSpec((tq, LANES), lambda b, p, i: (b * nq + i, p)),
        out_shape=jax.ShapeDtypeStruct((rows, 2 * LANES), BF16),
        compiler_params=_cparams(("parallel", "parallel", "arbitrary")),
        name="diff_attn_lat" if latent else "diff_attn_ctx",
    )(*args)


def _win_attn_kernel(*refs, with_window, seq, ctx):
    if with_window:
        q_ref, kc_ref, vtc_ref, kp_ref, kn0_ref, kn_ref, vtp_ref, vt0_ref, vtn_ref, sink_ref, o_ref = refs
        n = pl.program_id(1)
        k_all = jnp.concatenate([kc_ref[...], kp_ref[...], kn0_ref[...], kn_ref[...]], axis=0)
        vt_all = jnp.concatenate([vtc_ref[0], vtp_ref[0], vt0_ref[0], vtn_ref[0]], axis=1)
        nk = ctx + 3 * BLOCK
        r = lax.broadcasted_iota(jnp.int32, (nk, BLOCK), 0)
        c = lax.broadcasted_iota(jnp.int32, (nk, BLOCK), 1)
        krel = r - (ctx + BLOCK)
        in_win = ((jnp.abs(c - krel) <= WINDOW) & (krel >= -n * BLOCK) & (krel < seq - n * BLOCK))
        is_ctx = r < ctx
    else:
        q_ref, kc_ref, vtc_ref, sink_ref, o_ref = refs
        k_all = kc_ref[...]
        vt_all = vtc_ref[0]
    lane = lax.broadcasted_iota(jnp.int32, (1, LANES), 1)
    sink_row = sink_ref[...]
    nt = (((1,), (1,)), ((), ()))
    rep = B_HEADS // B_KV
    outs = []
    for j in range(B_HEADS):
        g = j // rep
        s = lax.dot_general(k_all, q_ref[:, j * LANES:(j + 1) * LANES], nt, preferred_element_type=F32)
        if with_window:
            s = jnp.where(is_ctx, s, jnp.where(in_win, s, NEG))
        sk = jnp.max(jnp.where(lane == j, sink_row, NEG), axis=1, keepdims=True)
        m = jnp.maximum(jnp.max(s, axis=0, keepdims=True), sk)
        p = jnp.exp2(s - m)
        l = jnp.sum(p, axis=0, keepdims=True) + jnp.exp2(sk - m)
        ot = jnp.dot(vt_all, p.astype(BF16), preferred_element_type=F32)
        outs.append(ot[g * B_HD:(g + 1) * B_HD] / l)
    for t in range(B_HEADS // 2):
        pair = jnp.concatenate([outs[2 * t], outs[2 * t + 1]], axis=0)
        o_ref[:, t * LANES:(t + 1) * LANES] = pair.T.astype(o_ref.dtype)


def _win_attn(bq, bk, vt_ctx, vt_lat, sink_row, nb, seq, ctx, latent):
    nblk = seq // BLOCK
    ctx_blk0 = nb * seq // ctx
    kern = functools.partial(_win_attn_kernel, with_window=latent, seq=seq, ctx=ctx)
    kc_spec = pl.BlockSpec((ctx, LANES), lambda b, n: (ctx_blk0 + b, 0))
    vtc_spec = pl.BlockSpec((1, LANES, ctx), lambda b, n: (b, 0, 0))
    sink_spec = pl.BlockSpec((1, LANES), lambda b, n: (0, 0))
    qw = B_HEADS * LANES
    if latent:
        def prev(n):
            return jnp.maximum(n - 1, 0)

        def nxt(n):
            return jnp.minimum(n + 1, nblk - 1)

        in_specs = [pl.BlockSpec((BLOCK, qw), lambda b, n: (b * nblk + n, 0)), kc_spec, vtc_spec,
                    pl.BlockSpec((BLOCK, LANES), lambda b, n: (b * nblk + prev(n), 0)),
                    pl.BlockSpec((BLOCK, LANES), lambda b, n: (b * nblk + n, 0)),
                    pl.BlockSpec((BLOCK, LANES), lambda b, n: (b * nblk + nxt(n), 0)),
                    pl.BlockSpec((1, LANES, BLOCK), lambda b, n: (b, 0, prev(n))),
                    pl.BlockSpec((1, LANES, BLOCK), lambda b, n: (b, 0, n)),
                    pl.BlockSpec((1, LANES, BLOCK), lambda b, n: (b, 0, nxt(n))),
                    sink_spec]
        args = (bq, bk, vt_ctx, bk, bk, bk, vt_lat, vt_lat, vt_lat, sink_row)
        grid = (nb, nblk)
        rows = nb * seq
        nq = nblk
    else:
        nq = ctx // BLOCK
        q_blk0 = nb * seq // BLOCK
        in_specs = [pl.BlockSpec((BLOCK, qw), lambda b, n: (q_blk0 + b * nq + n, 0)),
                    kc_spec, vtc_spec, sink_spec]
        args = (bq, bk, vt_ctx, sink_row)
        grid = (nb, nq)
        rows = nb * ctx
    return pl.pallas_call(
        kern,
        grid=grid,
        in_specs=in_specs,
        out_specs=pl.BlockSpec((BLOCK, B_HEADS * B_HD), lambda b, n: (b * nq + n, 0)),
        out_shape=jax.ShapeDtypeStruct((rows, B_HEADS * B_HD), BF16),
        compiler_params=_cparams(("parallel", "arbitrary")),
        name="win_attn_lat" if latent else "win_attn_ctx",
    )(*args)


def _split3(x):
    hi = x.astype(BF16)
    r1 = x - hi.astype(F32)
    mid = r1.astype(BF16)
    lo = (r1 - mid.astype(F32)).astype(BF16)
    return hi, mid, lo


def _gla_kernel(*refs, reverse, final):
    if final:
        q_ref, k_ref, v_ref, vt_ref, gl_ref, w2_ref, bg_ref, of_ref, r_ref, ng_ref, o_ref, st_ref = refs
    else:
        q_ref, k_ref, v_ref, vt_ref, gl_ref, w2_ref, bg_ref, o_ref, st_ref = refs
    gsz = GLA_GROUP
    nch = gsz // C_CHUNK

    @pl.when(pl.program_id(1) == 0)
    def _():
        st_ref[...] = jnp.zeros(st_ref.shape, F32)

    q = q_ref[...].astype(F32)
    k = k_ref[...].astype(F32)
    pre = jnp.dot(gl_ref[...], w2_ref[...], preferred_element_type=F32) + bg_ref[...]
    la = (jnp.minimum(pre, 0.0) - jnp.log(1.0 + jnp.exp(-jnp.abs(pre)))) * (1.0 / C_GATE_NORM)

    r = lax.broadcasted_iota(jnp.int32, (gsz, gsz), 0)
    c = lax.broadcasted_iota(jnp.int32, (gsz, gsz), 1)
    same = (r // C_CHUNK) == (c // C_CHUNK)
    tri = same & ((c >= r) if reverse else (c <= r))
    tri_b = jnp.where(tri, 1.0, 0.0).astype(BF16)
    same_b = jnp.where(same, 1.0, 0.0).astype(BF16)
    parts = _split3(la)
    bcum = sum(jnp.dot(tri_b, p, preferred_element_type=F32) for p in parts)
    btot = sum(jnp.dot(same_b, p, preferred_element_type=F32) for p in parts)
    q_dec = q * (jnp.exp(bcum) * (C_DK ** -0.5))
    k_inv = k * jnp.exp(-bcum)
    k_end = k * jnp.exp(btot - bcum)
    dec = jnp.exp(btot)

    rowid = lax.broadcasted_iota(jnp.int32, (gsz, LANES), 0) // C_CHUNK
    nt = (((1,), (1,)), ((), ()))
    order = list(range(nch))[::-1] if reverse else list(range(nch))
    outs = []
    for h in range(C_HEADS):
        sl = slice(h * LANES, (h + 1) * LANES)
        qd = q_dec[:, sl].astype(BF16)
        ki = k_inv[:, sl].astype(BF16)
        ke = k_end[:, sl].astype(BF16)
        att = lax.dot_general(qd, ki, nt, preferred_element_type=F32)
        att = jnp.where(tri, att, 0.0).astype(BF16)
        o_h = jnp.dot(att, v_ref[:, sl], preferred_element_type=F32)
        vth = vt_ref[sl, :]
        st = st_ref[h]
        inter = [None] * nch
        for ci in order:
            rows = slice(ci * C_CHUNK, (ci + 1) * C_CHUNK)
            inter[ci] = lax.dot_general(qd[rows], st.astype(BF16), nt, preferred_element_type=F32)
            ke_c = jnp.where(rowid == ci, ke, jnp.zeros_like(ke))
            upd = jnp.dot(vth, ke_c, preferred_element_type=F32)
            st = dec[ci * C_CHUNK:ci * C_CHUNK + 1, sl] * st + upd
        st_ref[h] = st
        outs.append(o_h + jnp.concatenate(inter, axis=0))
    if not final:
        for h in range(C_HEADS):
            o_ref[:, h * LANES:(h + 1) * LANES] = outs[h]
    else:
        for h in range(C_HEADS):
            sl = slice(h * LANES, (h + 1) * LANES)
            o = outs[h] + of_ref[:, sl]
            ms = jnp.sum(o * o, axis=1, keepdims=True) * (1.0 / C_DV)
            y = o * lax.rsqrt(ms + EPS) * ng_ref[:, sl]
            rr = r_ref[:, sl].astype(F32)
            o_ref[:, sl] = (y * (rr * (1.0 / (1.0 + jnp.exp(-rr))))).astype(o_ref.dtype)


def _gla_dir(cq, ck, cv, cvt, cg, w2p, bgp, nb, seq, ctx, reverse, fwd_out=None, cr=None, ng=None):
    t = cq.shape[0]
    gsz = GLA_GROUP
    assert ctx == gsz
    n_lat_g = seq // gsz
    ctx_g0 = nb * n_lat_g
    final = fwd_out is not None

    def grp(b, i):
        lat = (n_lat_g - i) if reverse else (i - 1)
        return jnp.where(i == 0, ctx_g0 + b, b * n_lat_g + lat)

    w = C_HEADS * LANES
    row_spec = pl.BlockSpec((gsz, w), lambda b, i: (grp(b, i), 0))
    in_specs = [row_spec, row_spec, row_spec,
                pl.BlockSpec((w, gsz), lambda b, i: (0, grp(b, i))),
                pl.BlockSpec((gsz, LANES), lambda b, i: (grp(b, i), 0)),
                pl.BlockSpec((LANES, w), lambda b, i: (0, 0)),
                pl.BlockSpec((1, w), lambda b, i: (0, 0))]
    args = [cq, ck, cv, cvt, cg, w2p, bgp]
    if final:
        in_specs += [row_spec, row_spec, pl.BlockSpec((1, w), lambda b, i: (0, 0))]
        args += [fwd_out, cr, ng]
    return pl.pallas_call(
        functools.partial(_gla_kernel, reverse=reverse, final=final),
        grid=(nb, n_lat_g + 1),
        in_specs=in_specs,
        out_specs=row_spec,
        out_shape=jax.ShapeDtypeStruct((t, w), BF16 if final else F32),
        scratch_shapes=[pltpu.VMEM((C_HEADS, LANES, LANES), F32)],
        compiler_params=_cparams(("parallel", "arbitrary")),
        name="gla_bwd" if reverse else "gla_fwd",
    )(*args)


def _outproj_kernel(x_ref, a_ref, b_ref, c_ref, wa_ref, wb_ref, wc_ref, g1_ref, o_ref):
    y = jnp.dot(a_ref[...], wa_ref[...], preferred_element_type=F32)
    y += jnp.dot(b_ref[...], wb_ref[...], preferred_element_type=F32)
    y += jnp.dot(c_ref[...], wc_ref[...], preferred_element_type=F32)
    o_ref[...] = x_ref[...] + g1_ref[0] * y


def _outproj(xs, a, b, c, wa, wb, wc, mod, tm, modrow):
    t, d = xs.shape
    return pl.pallas_call(
        _outproj_kernel,
        grid=(t // tm,),
        in_specs=[pl.BlockSpec((tm, d), lambda i: (i, 0)),
                  pl.BlockSpec((tm, a.shape[1]), lambda i: (i, 0)),
                  pl.BlockSpec((tm, b.shape[1]), lambda i: (i, 0)),
                  pl.BlockSpec((tm, c.shape[1]), lambda i: (i, 0)),
                  pl.BlockSpec(wa.shape, lambda i: (0, 0)),
                  pl.BlockSpec(wb.shape, lambda i: (0, 0)),
                  pl.BlockSpec(wc.shape, lambda i: (0, 0)),
                  pl.BlockSpec((1, 1, d), lambda i: (modrow(i), 0, 2))],
        out_specs=pl.BlockSpec((tm, d), lambda i: (i, 0)),
        out_shape=jax.ShapeDtypeStruct((t, d), F32),
        compiler_params=_cparams(("parallel",)),
        name="outproj",
    )(xs, a, b, c, wa, wb, wc, mod)


def _ffn_kernel(x_ref, g_ref, sc_ref, sh_ref, gate_ref, wg_ref, wu_ref, wd_ref, o_ref, h_ref, acc_ref):
    f = pl.program_id(1)

    @pl.when(f == 0)
    def _():
        h_ref[...] = _norm_mod(x_ref[...], g_ref[...], sc_ref[0], sh_ref[0]).astype(BF16)
        acc_ref[...] = jnp.zeros(acc_ref.shape, F32)

    h = h_ref[...]
    a = jnp.dot(h, wg_ref[...], preferred_element_type=F32)
    u = jnp.dot(h, wu_ref[...], preferred_element_type=F32)
    act = a * (1.0 / (1.0 + jnp.exp(-a))) * u
    acc_ref[...] += jnp.dot(act.astype(BF16), wd_ref[...], preferred_element_type=F32)

    @pl.when(f == pl.num_programs(1) - 1)
    def _():
        o_ref[...] = x_ref[...] + gate_ref[0] * acc_ref[...]


def _ffn(xs, g, mod, wg, wu, wd, tm, tf, modrow):
    t, d = xs.shape
    ff = wg.shape[1]
    return pl.pallas_call(
        _ffn_kernel,
        grid=(t // tm, ff // tf),
        in_specs=[pl.BlockSpec((tm, d), lambda i, f: (i, 0)),
                  pl.BlockSpec((1, d), lambda i, f: (0, 0)),
                  pl.BlockSpec((1, 1, d), lambda i, f: (modrow(i), 0, 4)),
                  pl.BlockSpec((1, 1, d), lambda i, f: (modrow(i), 0, 3)),
                  pl.BlockSpec((1, 1, d), lambda i, f: (modrow(i), 0, 5)),
                  pl.BlockSpec((d, tf), lambda i, f: (0, f)),
                  pl.BlockSpec((d, tf), lambda i, f: (0, f)),
                  pl.BlockSpec((tf, d), lambda i, f: (f, 0))],
        out_specs=pl.BlockSpec((tm, d), lambda i, f: (i, 0)),
        out_shape=jax.ShapeDtypeStruct((t, d), F32),
        scratch_shapes=[pltpu.VMEM((tm, d), BF16), pltpu.VMEM((tm, d), F32)],
        compiler_params=_cparams(("parallel", "arbitrary")),
        name="ffn_swiglu",
    )(xs, g.reshape(1, d), mod, mod, mod, wg, wu, wd)


def _moe_kernel(x_ref, g_ref, sc_ref, sh_ref, gate_ref, rt_ref, wg_ref, wu_ref, wd_ref, o_ref,
                h_ref, acc_ref, comb_ref):
    e = pl.program_id(1)
    f = pl.program_id(2)
    lane = lax.broadcasted_iota(jnp.int32, (1, LANES), 1)

    @pl.when((e == 0) & (f == 0))
    def _():
        hf = _norm_mod(x_ref[...], g_ref[...], sc_ref[0], sh_ref[0])
        h_ref[...] = hf.astype(BF16)
        acc_ref[...] = jnp.zeros(acc_ref.shape, F32)
        logits = jnp.dot(hf, rt_ref[...], precision=lax.Precision.HIGHEST, preferred_element_type=F32)
        lanef = lane.astype(F32)
        lg = jnp.where(lane < N_EXPERTS, logits, NEG)
        m1 = jnp.max(lg, axis=1, keepdims=True)
        i1 = jnp.min(jnp.where(lg == m1, lanef, float(LANES)), axis=1, keepdims=True)
        lg2 = jnp.where(lanef == i1, NEG, lg)
        m2 = jnp.max(lg2, axis=1, keepdims=True)
        i2 = jnp.min(jnp.where(lg2 == m2, lanef, float(LANES)), axis=1, keepdims=True)
        e2 = jnp.exp(m2 - m1)
        w1 = 1.0 / (1.0 + e2)
        comb_ref[...] = jnp.where(lanef == i1, w1, 0.0) + jnp.where(lanef == i2, e2 * w1, 0.0)

    ce = jnp.sum(jnp.where(lane == e, comb_ref[...], 0.0), axis=1, keepdims=True)
    h = h_ref[...]
    a = jnp.dot(h, wg_ref[0], preferred_element_type=F32)
    u = jnp.dot(h, wu_ref[0], preferred_element_type=F32)
    act = a * (1.0 / (1.0 + jnp.exp(-a))) * u * ce
    acc_ref[...] += jnp.dot(act.astype(BF16), wd_ref[0], preferred_element_type=F32)

    @pl.when((e == pl.num_programs(1) - 1) & (f == pl.num_programs(2) - 1))
    def _():
        o_ref[...] = x_ref[...] + gate_ref[0] * acc_ref[...]


def _moe(xs, g, mod, router_p, wg, wu, wd, tm, tf, modrow):
    t, d = xs.shape
    ne, _, ff = wg.shape
    return pl.pallas_call(
        _moe_kernel,
        grid=(t // tm, ne, ff // tf),
        in_specs=[pl.BlockSpec((tm, d), lambda i, e, f: (i, 0)),
                  pl.BlockSpec((1, d), lambda i, e, f: (0, 0)),
                  pl.BlockSpec((1, 1, d), lambda i, e, f: (modrow(i), 0, 4)),
                  pl.BlockSpec((1, 1, d), lambda i, e, f: (modrow(i), 0, 3)),
                  pl.BlockSpec((1, 1, d), lambda i, e, f: (modrow(i), 0, 5)),
                  pl.BlockSpec((d, LANES), lambda i, e, f: (0, 0)),
                  pl.BlockSpec((1, d, tf), lambda i, e, f: (e, 0, f)),
                  pl.BlockSpec((1, d, tf), lambda i, e, f: (e, 0, f)),
                  pl.BlockSpec((1, tf, d), lambda i, e, f: (e, f, 0))],
        out_specs=pl.BlockSpec((tm, d), lambda i, e, f: (i, 0)),
        out_shape=jax.ShapeDtypeStruct((t, d), F32),
        scratch_shapes=[pltpu.VMEM((tm, d), BF16), pltpu.VMEM((tm, d), F32), pltpu.VMEM((tm, LANES), F32)],
        compiler_params=_cparams(("parallel", "arbitrary", "arbitrary")),
        name="moe_swiglu",
    )(xs, g.reshape(1, d), mod, mod, mod, router_p, wg, wu, wd)


def _final_norm_kernel(x_ref, g_ref, o_ref):
    x = x_ref[...]
    ms = jnp.mean(x * x, axis=-1, keepdims=True)
    o_ref[...] = x * lax.rsqrt(ms + EPS) * g_ref[...]


def _final_norm(xs, g, rows, tm):
    d = xs.shape[1]
    return pl.pallas_call(
        _final_norm_kernel,
        grid=(rows // tm,),
        in_specs=[pl.BlockSpec((tm, d), lambda i: (i, 0)), pl.BlockSpec((1, d), lambda i: (0, 0))],
        out_specs=pl.BlockSpec((tm, d), lambda i: (i, 0)),
        out_shape=jax.ShapeDtypeStruct((rows, d), F32),
        compiler_params=_cparams(("parallel",)),
        name="final_norm",
    )(xs, g.reshape(1, d))


def _pad_heads(v, width, used):
    lead = v.shape[:-1]
    v = v.reshape(lead + (C_HEADS, used))
    v = jnp.pad(v, [(0, 0)] * len(lead) + [(0, 0), (0, width - used)])
    return v.reshape(lead + (C_HEADS * width,))


def kernel(x, c, ctx, c_ctx, norm1_g, norm2_g, ada_w, ada_b, w_in, w_out, a_lambda, a_norm_g, b_sink,
           c_gate_w2, c_gate_b, c_norm_g, ffn_w_gate, ffn_w_up, ffn_w_down, moe_router, moe_w_gate,
           moe_w_up, moe_w_down, final_g):
    nb, seq, d = x.shape
    nctx = ctx.shape[1]
    depth = w_in.shape[0]
    n_lat = nb * seq
    tm = nb * nctx
    assert seq % tm == 0 and nb < 8
    tm_r = tm // 2
    n_lat_tiles = n_lat // tm

    def modrow(i):
        return jnp.where(i < n_lat_tiles, i // (seq // tm), nb)

    xs = jnp.concatenate([x.reshape(n_lat, d), ctx.reshape(nb * nctx, d)], axis=0)
    cc = jnp.zeros((8, d), F32).at[:nb].set(c).at[nb].set(c_ctx)
    mod_all = _modulation(cc, ada_w, ada_b).reshape(depth, 8, 1, 6 * d)
    tables = _rope_tables(seq, tm_r)

    for layer in range(depth):
        lam_init = 0.8 - 0.6 * math.exp(-0.3 * layer)
        mod = mod_all[layer]
        w_pad = jnp.concatenate([w_in[layer], jnp.zeros((d, 1), F32)], axis=1)
        w_rope = jnp.take(w_pad, _ROPE_COLS, axis=1).astype(BF16)
        w_plain = jnp.take(w_pad, _PLAIN_COLS, axis=1).astype(BF16)

        aq, ak, bq, bk = _normproj(xs, norm1_g[layer], mod, 1, 0, w_rope, ROPE_WIDTHS, tm_r,
                                   n_lat // tm_r, seq // tm_r, nb, tables)
        av, bv, cq, ck, cv, cr, cg = _normproj(xs, norm1_g[layer], mod, 1, 0, w_plain, PLAIN_WIDTHS, tm,
                                               n_lat_tiles, seq // tm, nb)

        ones_l = jnp.ones((nb, A_HEADS, 8, seq), BF16)
        ones_c = jnp.ones((nb, A_HEADS, 8, nctx), BF16)
        avt_lat = jnp.concatenate(
            [av[:n_lat].reshape(nb, seq, A_HEADS, A_V).transpose(0, 2, 3, 1), ones_l], axis=2)
        avt_ctx = jnp.concatenate(
            [av[n_lat:].reshape(nb, nctx, A_HEADS, A_V).transpose(0, 2, 3, 1), ones_c], axis=2)
        bvt_lat = bv[:n_lat].reshape(nb, seq, LANES).transpose(0, 2, 1)
        bvt_ctx = bv[n_lat:].reshape(nb, nctx, LANES).transpose(0, 2, 1)
        cvt = cv.T

        a_lat = _diff_attn(aq, ak, avt_ctx, avt_lat, a_lambda[layer], a_norm_g[layer], lam_init,
                           nb, seq, nctx, True)
        a_ctx = _diff_attn(aq, ak, avt_ctx, None, a_lambda[layer], a_norm_g[layer], lam_init,
                           nb, seq, nctx, False)
        sink_row = jnp.zeros((1, LANES), F32).at[0, :B_HEADS].set(b_sink[layer] * LOG2E)
        b_lat = _win_attn(bq, bk, bvt_ctx, bvt_lat, sink_row, nb, seq, nctx, True)
        b_ctx = _win_attn(bq, bk, bvt_ctx, None, sink_row, nb, seq, nctx, False)

        w2 = c_gate_w2[layer]
        w2p = [jnp.zeros((LANES, C_HEADS * LANES), F32).at[dd * C_RANK:(dd + 1) * C_RANK].set(
            _pad_heads(w2[dd], LANES, C_DK)).astype(BF16) for dd in range(2)]
        bgp = [_pad_heads(c_gate_b[layer, dd], LANES, C_DK).reshape(1, -1) for dd in range(2)]
        ng = _pad_heads(jnp.tile(c_norm_g[layer], C_HEADS), LANES, C_DV).reshape(1, -1)
        o_f = _gla_dir(cq, ck, cv, cvt, cg, w2p[0], bgp[0], nb, seq, nctx, False)
        g_out = _gla_dir(cq, ck, cv, cvt, cg, w2p[1], bgp[1], nb, seq, nctx, True, o_f, cr, ng)

        a_all = jnp.concatenate([a_lat, a_ctx], axis=0)
        b_all = jnp.concatenate([b_lat, b_ctx], axis=0)
        wo = w_out[layer]
        wa = wo[:256].astype(BF16)
        wb = wo[256:640].astype(BF16)
        wc = jnp.pad(wo[640:].reshape(C_HEADS, C_DV, d), ((0, 0), (0, LANES - C_DV), (0, 0))).reshape(
            C_HEADS * LANES, d).astype(BF16)
        xs = _outproj(xs, a_all, b_all, g_out, wa, wb, wc, mod, tm, modrow)

        j = layer // 2
        if layer % 2 == 0:
            xs = _ffn(xs, norm2_g[layer], mod, ffn_w_gate[j].astype(BF16), ffn_w_up[j].astype(BF16),
                      ffn_w_down[j].astype(BF16), tm, 256, modrow)
        else:
            router_p = jnp.pad(moe_router[j], ((0, 0), (0, LANES - N_EXPERTS)))
            xs = _moe(xs, norm2_g[layer], mod, router_p, moe_w_gate[j].astype(BF16),
                      moe_w_up[j].astype(BF16), moe_w_down[j].astype(BF16), tm, 256, modrow)

    return _final_norm(xs, final_g, n_lat, tm).reshape(nb, seq, d)
```

```python
import functools
import math

import numpy as np
import jax
import jax.numpy as jnp
from jax import lax
from jax.experimental import pallas as pl
from jax.experimental.pallas import tpu as pltpu

F32 = jnp.float32
BF16 = jnp.bfloat16

EPS = 1e-6
ROPE_BASE = 10000.0
GRID_W = 64
LANES = 128
LOG2E = math.log2(math.e)
NEG = -1e30

A_HEADS, A_QK, A_V = 4, 32, 64
B_HEADS, B_KV, B_HD, WINDOW, BLOCK = 6, 2, 64, 128, 128
C_HEADS, C_DK, C_DV, C_RANK, C_GATE_NORM, C_CHUNK = 4, 48, 96, 16, 16.0, 64
N_EXPERTS = 8
IN_SIZES = (256, 256, 256, 384, 128, 128, 192, 192, 384, 384, 32)
IN_W = sum(IN_SIZES)

ROPE_WIDTHS = (256, 256, 768, 128)
PLAIN_WIDTHS = (256, 128, 512, 512, 512, 512, 128)
ROPE_W = sum(ROPE_WIDTHS)
GLA_GROUP = 256
VMEM_LIMIT = 48 * 1024 * 1024


def _cparams(sem):
    return pltpu.CompilerParams(dimension_semantics=sem, vmem_limit_bytes=VMEM_LIMIT)


def _column_maps():
    off = np.concatenate([[0], np.cumsum(IN_SIZES)])
    aq0, ak0, av0, bq0, bk0, bv0, cq0, ck0, cv0, cr0, cg0 = [int(v) for v in off[:11]]
    zero = IN_W

    def a_partner(d):
        return d + 8 if d % 16 < 8 else d - 8

    def b_partner(d):
        return d + 16 if d % 32 < 16 else d - 16

    main, part, dim, dd, scale = [], [], [], [], []
    a_scale = A_QK ** -0.5 * LOG2E
    b_scale = B_HD ** -0.5 * LOG2E
    for base, sc in ((aq0, a_scale), (ak0, 1.0)):
        for j in range(256):
            d = j % 32
            main.append(base + j); part.append(base + j - d + a_partner(d))
            dim.append(32); dd.append(d); scale.append(sc)
    for t in range(B_HEADS):
        g = t // (B_HEADS // B_KV)
        for lane in range(LANES):
            d = lane % 64
            if lane // 64 == g:
                main.append(bq0 + t * 64 + d); part.append(bq0 + t * 64 + b_partner(d))
            else:
                main.append(zero); part.append(zero)
            dim.append(64); dd.append(d); scale.append(b_scale)
    for j in range(128):
        d = j % 64
        main.append(bk0 + j); part.append(bk0 + j - d + b_partner(d))
        dim.append(64); dd.append(d); scale.append(1.0)

    plain = list(range(av0, av0 + 256)) + list(range(bv0, bv0 + 128))
    for base in (cq0, ck0):
        for h in range(C_HEADS):
            plain += [base + h * C_DK + d if d < C_DK else zero for d in range(LANES)]
    for base in (cv0, cr0):
        for h in range(C_HEADS):
            plain += [base + h * C_DV + d if d < C_DV else zero for d in range(LANES)]
    plain += [cg0 + d if d < 2 * C_RANK else zero for d in range(LANES)]
    return (np.array(main + part, np.int32), np.array(plain, np.int32),
            np.array(dim), np.array(dd), np.array(scale, np.float32))


_ROPE_COLS, _PLAIN_COLS, _R_DIM, _R_D, _R_SCALE = _column_maps()


def _rope_tables(seq, pad_rows):
    pos = jnp.arange(seq, dtype=jnp.int32)
    row = (pos // GRID_W).astype(F32)[:, None]
    col = (pos % GRID_W).astype(F32)[:, None]
    quarter = _R_DIM // 4
    half = _R_DIM // 2
    is_col = (_R_D % _R_DIM) >= half
    ddh = _R_D % half
    first = ddh < quarter
    f = (ddh % quarter).astype(np.float32)
    inv = jnp.asarray(ROPE_BASE, F32) ** (-jnp.asarray(f) / jnp.asarray(quarter.astype(np.float32)))
    ang = jnp.where(jnp.asarray(is_col)[None, :], col, row) * inv[None, :]
    scale = jnp.asarray(_R_SCALE)[None, :]
    cos = jnp.cos(ang) * scale
    sin = jnp.where(jnp.asarray(first)[None, :], -jnp.sin(ang), jnp.sin(ang)) * scale
    cos = jnp.concatenate([cos, jnp.broadcast_to(scale, (pad_rows, ROPE_W))], axis=0)
    sin = jnp.concatenate([sin, jnp.zeros((pad_rows, ROPE_W), F32)], axis=0)
    return cos, sin


def _mod_kernel(c_ref, w_ref, b_ref, o_ref):
    c = c_ref[...]
    s = c * (1.0 / (1.0 + jnp.exp(-c)))
    o_ref[0] = jnp.dot(s, w_ref[0], precision=lax.Precision.HIGHEST,
                       preferred_element_type=F32) + b_ref[0]


def _modulation(cc, ada_w, ada_b):
    depth, d, n = ada_w.shape
    tn = n // 4
    return pl.pallas_call(
        _mod_kernel,
        grid=(depth, n // tn),
        in_specs=[pl.BlockSpec((8, d), lambda l, j: (0, 0)),
                  pl.BlockSpec((1, d, tn), lambda l, j: (l, 0, j)),
                  pl.BlockSpec((1, 1, tn), lambda l, j: (l, 0, j))],
        out_specs=pl.BlockSpec((1, 8, tn), lambda l, j: (l, 0, j)),
        out_shape=jax.ShapeDtypeStruct((depth, 8, n), F32),
        compiler_params=_cparams(("arbitrary", "arbitrary")),
        name="adaln_mod",
    )(cc, ada_w, ada_b.reshape(depth, 1, n))


def _norm_mod(x, g, sc, sh):
    ms = jnp.mean(x * x, axis=-1, keepdims=True)
    return (x * lax.rsqrt(ms + EPS) * g) * (1.0 + sc) + sh


def _normproj_kernel(x_ref, g_ref, sc_ref, sh_ref, w_ref, *rest, rope, widths):
    if rope:
        c_ref, s_ref = rest[:2]
        outs = rest[2:]
    else:
        outs = rest
    h = _norm_mod(x_ref[...], g_ref[...], sc_ref[0], sh_ref[0])
    acc = jnp.dot(h.astype(BF16), w_ref[...], preferred_element_type=F32)
    if rope:
        rw = sum(widths)
        acc = acc[:, :rw] * c_ref[...] + acc[:, rw:] * s_ref[...]
    off = 0
    for o_ref, w in zip(outs, widths):
        o_ref[...] = acc[:, off:off + w].astype(o_ref.dtype)
        off += w


def _normproj(xs, g, mod, sc_chunk, sh_chunk, w, widths, tm, n_lat_tiles, tiles_per_batch, nb, tables=None):
    t, d = xs.shape
    rope = tables is not None

    def modrow(i):
        return jnp.where(i < n_lat_tiles, i // tiles_per_batch, nb)

    in_specs = [pl.BlockSpec((tm, d), lambda i: (i, 0)),
                pl.BlockSpec((1, d), lambda i: (0, 0)),
                pl.BlockSpec((1, 1, d), lambda i: (modrow(i), 0, sc_chunk)),
                pl.BlockSpec((1, 1, d), lambda i: (modrow(i), 0, sh_chunk)),
                pl.BlockSpec(w.shape, lambda i: (0, 0))]
    args = [xs, g.reshape(1, d), mod, mod, w]
    if rope:
        def tabrow(i):
            return jnp.where(i < n_lat_tiles, i % tiles_per_batch, tiles_per_batch)
        in_specs += [pl.BlockSpec((tm, ROPE_W), lambda i: (tabrow(i), 0))] * 2
        args += list(tables)
    return pl.pallas_call(
        functools.partial(_normproj_kernel, rope=rope, widths=widths),
        grid=(t // tm,),
        in_specs=in_specs,
        out_specs=[pl.BlockSpec((tm, wd), lambda i: (i, 0)) for wd in widths],
        out_shape=[jax.ShapeDtypeStruct((t, wd), BF16) for wd in widths],
        compiler_params=_cparams(("parallel",)),
        name="normproj_rope" if rope else "normproj_plain",
    )(*args)


def _diff_attn_kernel(*refs, lam_init, has_lat, tk):
    if has_lat:
        q_ref, kc_ref, vtc_ref, kl_ref, vtl_ref, lam_ref, g_ref, o_ref, m_scr, acc_scr, mc_scr, s_scr = refs
    else:
        q_ref, kc_ref, vtc_ref, lam_ref, g_ref, o_ref, m_scr, acc_scr, mc_scr, s_scr = refs
    q = q_ref[...]
    lane = lax.broadcasted_iota(jnp.int32, (1, LANES), 1)
    nt = (((1,), (1,)), ((), ()))
    qms = [jnp.where((lane >= i * A_QK) & (lane < (i + 1) * A_QK), q, jnp.zeros_like(q)) for i in range(4)]

    sub = 512

    def scores(k, slot, nk):
        for i in range(4):
            s = lax.dot_general(k, qms[i], nt, preferred_element_type=F32)
            s_scr[slot, i, 0:nk, :] = s
            mc_scr[slot, i] = jnp.max(s, axis=0, keepdims=True)

    def consume(vt_of, slot, nk, first):
        for i in range(4):
            m_cur = mc_scr[slot, i]
            if first:
                m_new = m_cur
            else:
                m_run = m_scr[i]
                m_new = jnp.maximum(m_run, m_cur)
            pv = None
            for t in range(nk // min(sub, nk)):
                w = min(sub, nk)
                p = jnp.exp2(s_scr[slot, i, t * w:(t + 1) * w, :] - m_new).astype(BF16)
                d = jnp.dot(vt_of(i // 2, t, w), p, preferred_element_type=F32)
                pv = d if pv is None else pv + d
            if first:
                acc_scr[i] = pv
            else:
                acc_scr[i] = jnp.exp2(m_run - m_new) * acc_scr[i] + pv
            m_scr[i] = m_new

    nctx = kc_ref.shape[0]
    scores(kc_ref[...], 1, nctx)

    def vt_ctx(hh, t, w):
        return vtc_ref[0, hh, :, t * w:(t + 1) * w]

    if not has_lat:
        consume(vt_ctx, 1, nctx, True)
    else:
        n_chunks = kl_ref.shape[0] // tk

        def k_lat(c):
            return kl_ref[pl.ds(pl.multiple_of(c * tk, tk), tk), :]

        def vt_lat(c):
            def get(hh, t, w):
                return vtl_ref[0, hh, :, pl.ds(pl.multiple_of(c * tk + t * w, w), w)]
            return get

        scores(k_lat(0), 0, tk)
        consume(vt_ctx, 1, nctx, True)

        def body(j, carry):
            c = 2 * j
            scores(k_lat(c + 1), 1, tk)
            consume(vt_lat(c), 0, tk, False)
            scores(k_lat(c + 2), 0, tk)
            consume(vt_lat(c + 1), 1, tk, False)
            return carry
        pairs = (n_chunks - 1) // 2
        lax.fori_loop(0, pairs, body, 0)
        c_last = 2 * pairs
        if c_last + 1 < n_chunks:
            scores(k_lat(c_last + 1), 1, tk)
            consume(vt_lat(c_last), 0, tk, False)
            consume(vt_lat(c_last + 1), 1, tk, False)
        else:
            consume(vt_lat(c_last), 0, tk, False)

    lp = lam_ref[...]
    lam = (jnp.exp(jnp.sum(lp[0:1] * lp[1:2], axis=1, keepdims=True))
           - jnp.exp(jnp.sum(lp[2:3] * lp[3:4], axis=1, keepdims=True)) + lam_init)
    heads = []
    for hh in range(2):
        maps = []
        for m in range(2):
            acc = acc_scr[hh * 2 + m]
            maps.append(acc[0:A_V] / acc[A_V:A_V + 1])
        oh = maps[0] - lam * maps[1]
        ms = jnp.mean(oh * oh, axis=0, keepdims=True)
        heads.append(oh * lax.rsqrt(ms + EPS) * g_ref[...] * (1.0 - lam_init))
    o_ref[...] = jnp.concatenate(heads, axis=0).T.astype(o_ref.dtype)


def _diff_attn(aq, ak, vt_ctx, vt_lat, lam_p, norm_g, lam_init, nb, seq, ctx, latent):
    tq = 256
    tk = min(1024, seq)
    kern = functools.partial(_diff_attn_kernel, lam_init=lam_init, has_lat=latent, tk=tk)
    ctx_blk0 = nb * seq // ctx
    vtc_spec = pl.BlockSpec((1, 2, A_V + 8, ctx), lambda b, p, i: (b, p, 0, 0))
    kc_spec = pl.BlockSpec((ctx, LANES), lambda b, p, i: (ctx_blk0 + b, p))
    par_specs = [pl.BlockSpec((4, A_QK), lambda b, p, i: (0, 0)),
                 pl.BlockSpec((A_V, 1), lambda b, p, i: (0, 0))]
    if latent:
        nq = seq // tq
        in_specs = [pl.BlockSpec((tq, LANES), lambda b, p, i: (b * nq + i, p)), kc_spec, vtc_spec,
                    pl.BlockSpec((seq, LANES), lambda b, p, i: (b, p)),
                    pl.BlockSpec((1, 2, A_V + 8, seq), lambda b, p, i: (b, p, 0, 0))] + par_specs
        args = (aq, ak, vt_ctx, ak, vt_lat, lam_p, norm_g.reshape(A_V, 1))
        rows = nb * seq
    else:
        nq = ctx // tq
        q_blk0 = nb * seq // tq
        in_specs = [pl.BlockSpec((tq, LANES), lambda b, p, i: (q_blk0 + b * nq + i, p)),
                    kc_spec, vtc_spec] + par_specs
        args = (aq, ak, vt_ctx, lam_p, norm_g.reshape(A_V, 1))
        rows = nb * ctx
    return pl.pallas_call(
        kern,
        grid=(nb, 2, nq),
        in_specs=in_specs,
        out_specs=pl.BlockSpec((tq, LANES), lambda b, p, i: (b * nq + i, p)),
        out_shape=jax.ShapeDtypeStruct((rows, 2 * LANES), BF16),
        scratch_shapes=[pltpu.VMEM((4, 1, tq), F32), pltpu.VMEM((4, A_V + 8, tq), F32),
                        pltpu.VMEM((2, 4, 1, tq), F32),
                        pltpu.VMEM((2, 4, tk if latent else ctx, tq), F32)],
        compiler_params=_cparams(("parallel", "parallel", "arbitrary")),
        name="diff_attn_lat" if latent else "diff_attn_ctx",
    )(*args)


def _win_attn_kernel(*refs, with_window, seq, ctx):
    if with_window:
        q_ref, kc_ref, vtc_ref, kp_ref, kn0_ref, kn_ref, vtp_ref, vt0_ref, vtn_ref, sink_ref, o_ref = refs
        n = pl.program_id(1)
        k_all = jnp.concatenate([kc_ref[...], kp_ref[...], kn0_ref[...], kn_ref[...]], axis=0)
        vt_all = jnp.concatenate([vtc_ref[0], vtp_ref[0], vt0_ref[0], vtn_ref[0]], axis=1)
        nk = ctx + 3 * BLOCK
        r = lax.broadcasted_iota(jnp.int32, (nk, BLOCK), 0)
        c = lax.broadcasted_iota(jnp.int32, (nk, BLOCK), 1)
        krel = r - (ctx + BLOCK)
        in_win = ((jnp.abs(c - krel) <= WINDOW) & (krel >= -n * BLOCK) & (krel < seq - n * BLOCK))
        is_ctx = r < ctx
    else:
        q_ref, kc_ref, vtc_ref, sink_ref, o_ref = refs
        k_all = kc_ref[...]
        vt_all = vtc_ref[0]
    lane = lax.broadcasted_iota(jnp.int32, (1, LANES), 1)
    sink_row = sink_ref[...]
    nt = (((1,), (1,)), ((), ()))
    rep = B_HEADS // B_KV
    outs = []
    for j in range(B_HEADS):
        g = j // rep
        s = lax.dot_general(k_all, q_ref[:, j * LANES:(j + 1) * LANES], nt, preferred_element_type=F32)
        if with_window:
            s = jnp.where(is_ctx, s, jnp.where(in_win, s, NEG))
        sk = jnp.max(jnp.where(lane == j, sink_row, NEG), axis=1, keepdims=True)
        m = jnp.maximum(jnp.max(s, axis=0, keepdims=True), sk)
        p = jnp.exp2(s - m)
        l = jnp.sum(p, axis=0, keepdims=True) + jnp.exp2(sk - m)
        ot = jnp.dot(vt_all, p.astype(BF16), preferred_element_type=F32)
        outs.append(ot[g * B_HD:(g + 1) * B_HD] / l)
    for t in range(B_HEADS // 2):
        pair = jnp.concatenate([outs[2 * t], outs[2 * t + 1]], axis=0)
        o_ref[:, t * LANES:(t + 1) * LANES] = pair.T.astype(o_ref.dtype)


def _win_attn(bq, bk, vt_ctx, vt_lat, sink_row, nb, seq, ctx, latent):
    nblk = seq // BLOCK
    ctx_blk0 = nb * seq // ctx
    kern = functools.partial(_win_attn_kernel, with_window=latent, seq=seq, ctx=ctx)
    kc_spec = pl.BlockSpec((ctx, LANES), lambda b, n: (ctx_blk0 + b, 0))
    vtc_spec = pl.BlockSpec((1, LANES, ctx), lambda b, n: (b, 0, 0))
    sink_spec = pl.BlockSpec((1, LANES), lambda b, n: (0, 0))
    qw = B_HEADS * LANES
    if latent:
        def prev(n):
            return jnp.maximum(n - 1, 0)

        def nxt(n):
            return jnp.minimum(n + 1, nblk - 1)

        in_specs = [pl.BlockSpec((BLOCK, qw), lambda b, n: (b * nblk + n, 0)), kc_spec, vtc_spec,
                    pl.BlockSpec((BLOCK, LANES), lambda b, n: (b * nblk + prev(n), 0)),
                    pl.BlockSpec((BLOCK, LANES), lambda b, n: (b * nblk + n, 0)),
                    pl.BlockSpec((BLOCK, LANES), lambda b, n: (b * nblk + nxt(n), 0)),
                    pl.BlockSpec((1, LANES, BLOCK), lambda b, n: (b, 0, prev(n))),
                    pl.BlockSpec((1, LANES, BLOCK), lambda b, n: (b, 0, n)),
                    pl.BlockSpec((1, LANES, BLOCK), lambda b, n: (b, 0, nxt(n))),
                    sink_spec]
        args = (bq, bk, vt_ctx, bk, bk, bk, vt_lat, vt_lat, vt_lat, sink_row)
        grid = (nb, nblk)
        rows = nb * seq
        nq = nblk
    else:
        nq = ctx // BLOCK
        q_blk0 = nb * seq // BLOCK
        in_specs = [pl.BlockSpec((BLOCK, qw), lambda b, n: (q_blk0 + b * nq + n, 0)),
                    kc_spec, vtc_spec, sink_spec]
        args = (bq, bk, vt_ctx, sink_row)
        grid = (nb, nq)
        rows = nb * ctx
    return pl.pallas_call(
        kern,
        grid=grid,
        in_specs=in_specs,
        out_specs=pl.BlockSpec((BLOCK, B_HEADS * B_HD), lambda b, n: (b * nq + n, 0)),
        out_shape=jax.ShapeDtypeStruct((rows, B_HEADS * B_HD), BF16),
        compiler_params=_cparams(("parallel", "arbitrary")),
        name="win_attn_lat" if latent else "win_attn_ctx",
    )(*args)


def _split3(x):
    hi = x.astype(BF16)
    r1 = x - hi.astype(F32)
    mid = r1.astype(BF16)
    lo = (r1 - mid.astype(F32)).astype(BF16)
    return hi, mid, lo


def _gla_kernel(*refs, reverse, final):
    if final:
        q_ref, k_ref, v_ref, vt_ref, gl_ref, w2_ref, bg_ref, of_ref, r_ref, ng_ref, o_ref, st_ref = refs
    else:
        q_ref, k_ref, v_ref, vt_ref, gl_ref, w2_ref, bg_ref, o_ref, st_ref = refs
    gsz = GLA_GROUP
    nch = gsz // C_CHUNK

    @pl.when(pl.program_id(1) == 0)
    def _():
        st_ref[...] = jnp.zeros(st_ref.shape, F32)

    q = q_ref[...].astype(F32)
    k = k_ref[...].astype(F32)
    pre = jnp.dot(gl_ref[...], w2_ref[...], preferred_element_type=F32) + bg_ref[...]
    la = (jnp.minimum(pre, 0.0) - jnp.log(1.0 + jnp.exp(-jnp.abs(pre)))) * (1.0 / C_GATE_NORM)

    r = lax.broadcasted_iota(jnp.int32, (gsz, gsz), 0)
    c = lax.broadcasted_iota(jnp.int32, (gsz, gsz), 1)
    same = (r // C_CHUNK) == (c // C_CHUNK)
    tri = same & ((c >= r) if reverse else (c <= r))
    tri_b = jnp.where(tri, 1.0, 0.0).astype(BF16)
    same_b = jnp.where(same, 1.0, 0.0).astype(BF16)
    parts = _split3(la)
    bcum = sum(jnp.dot(tri_b, p, preferred_element_type=F32) for p in parts)
    btot = sum(jnp.dot(same_b, p, preferred_element_type=F32) for p in parts)
    q_dec = q * (jnp.exp(bcum) * (C_DK ** -0.5))
    k_inv = k * jnp.exp(-bcum)
    k_end = k * jnp.exp(btot - bcum)
    dec = jnp.exp(btot)

    rowid = lax.broadcasted_iota(jnp.int32, (gsz, LANES), 0) // C_CHUNK
    nt = (((1,), (1,)), ((), ()))
    order = list(range(nch))[::-1] if reverse else list(range(nch))
    outs = []
    for h in range(C_HEADS):
        sl = slice(h * LANES, (h + 1) * LANES)
        qd = q_dec[:, sl].astype(BF16)
        ki = k_inv[:, sl].astype(BF16)
        ke = k_end[:, sl].astype(BF16)
        att = lax.dot_general(qd, ki, nt, preferred_element_type=F32)
        att = jnp.where(tri, att, 0.0).astype(BF16)
        o_h = jnp.dot(att, v_ref[:, sl], preferred_element_type=F32)
        vth = vt_ref[sl, :]
        st = st_ref[h]
        inter = [None] * nch
        for ci in order:
            rows = slice(ci * C_CHUNK, (ci + 1) * C_CHUNK)
            inter[ci] = lax.dot_general(qd[rows], st.astype(BF16), nt, preferred_element_type=F32)
            ke_c = jnp.where(rowid == ci, ke, jnp.zeros_like(ke))
            upd = jnp.dot(vth, ke_c, preferred_element_type=F32)
            st = dec[ci * C_CHUNK:ci * C_CHUNK + 1, sl] * st + upd
        st_ref[h] = st
        outs.append(o_h + jnp.concatenate(inter, axis=0))
    if not final:
        for h in range(C_HEADS):
            o_ref[:, h * LANES:(h + 1) * LANES] = outs[h]
    else:
        for h in range(C_HEADS):
            sl = slice(h * LANES, (h + 1) * LANES)
            o = outs[h] + of_ref[:, sl]
            ms = jnp.sum(o * o, axis=1, keepdims=True) * (1.0 / C_DV)
            y = o * lax.rsqrt(ms + EPS) * ng_ref[:, sl]
            rr = r_ref[:, sl].astype(F32)
            o_ref[:, sl] = (y * (rr * (1.0 / (1.0 + jnp.exp(-rr))))).astype(o_ref.dtype)


def _gla_dir(cq, ck, cv, cvt, cg, w2p, bgp, nb, seq, ctx, reverse, fwd_out=None, cr=None, ng=None):
    t = cq.shape[0]
    gsz = GLA_GROUP
    assert ctx == gsz
    n_lat_g = seq // gsz
    ctx_g0 = nb * n_lat_g
    final = fwd_out is not None

    def grp(b, i):
        lat = (n_lat_g - i) if reverse else (i - 1)
        return jnp.where(i == 0, ctx_g0 + b, b * n_lat_g + lat)

    w = C_HEADS * LANES
    row_spec = pl.BlockSpec((gsz, w), lambda b, i: (grp(b, i), 0))
    in_specs = [row_spec, row_spec, row_spec,
                pl.BlockSpec((w, gsz), lambda b, i: (0, grp(b, i))),
                pl.BlockSpec((gsz, LANES), lambda b, i: (grp(b, i), 0)),
                pl.BlockSpec((LANES, w), lambda b, i: (0, 0)),
                pl.BlockSpec((1, w), lambda b, i: (0, 0))]
    args = [cq, ck, cv, cvt, cg, w2p, bgp]
    if final:
        in_specs += [row_spec, row_spec, pl.BlockSpec((1, w), lambda b, i: (0, 0))]
        args += [fwd_out, cr, ng]
    return pl.pallas_call(
        functools.partial(_gla_kernel, reverse=reverse, final=final),
        grid=(nb, n_lat_g + 1),
        in_specs=in_specs,
        out_specs=row_spec,
        out_shape=jax.ShapeDtypeStruct((t, w), BF16 if final else F32),
        scratch_shapes=[pltpu.VMEM((C_HEADS, LANES, LANES), F32)],
        compiler_params=_cparams(("parallel", "arbitrary")),
        name="gla_bwd" if reverse else "gla_fwd",
    )(*args)


def _outproj_kernel(x_ref, a_ref, b_ref, c_ref, wa_ref, wb_ref, wc_ref, g1_ref, o_ref):
    y = jnp.dot(a_ref[...], wa_ref[...], preferred_element_type=F32)
    y += jnp.dot(b_ref[...], wb_ref[...], preferred_element_type=F32)
    y += jnp.dot(c_ref[...], wc_ref[...], preferred_element_type=F32)
    o_ref[...] = x_ref[...] + g1_ref[0] * y


def _outproj(xs, a, b, c, wa, wb, wc, mod, tm, modrow):
    t, d = xs.shape
    return pl.pallas_call(
        _outproj_kernel,
        grid=(t // tm,),
        in_specs=[pl.BlockSpec((tm, d), lambda i: (i, 0)),
                  pl.BlockSpec((tm, a.shape[1]), lambda i: (i, 0)),
                  pl.BlockSpec((tm, b.shape[1]), lambda i: (i, 0)),
                  pl.BlockSpec((tm, c.shape[1]), lambda i: (i, 0)),
                  pl.BlockSpec(wa.shape, lambda i: (0, 0)),
                  pl.BlockSpec(wb.shape, lambda i: (0, 0)),
                  pl.BlockSpec(wc.shape, lambda i: (0, 0)),
                  pl.BlockSpec((1, 1, d), lambda i: (modrow(i), 0, 2))],
        out_specs=pl.BlockSpec((tm, d), lambda i: (i, 0)),
        out_shape=jax.ShapeDtypeStruct((t, d), F32),
        compiler_params=_cparams(("parallel",)),
        name="outproj",
    )(xs, a, b, c, wa, wb, wc, mod)


def _ffn_kernel(x_ref, g_ref, sc_ref, sh_ref, gate_ref, wg_ref, wu_ref, wd_ref, o_ref, h_ref, acc_ref):
    f = pl.program_id(1)

    @pl.when(f == 0)
    def _():
        h_ref[...] = _norm_mod(x_ref[...], g_ref[...], sc_ref[0], sh_ref[0]).astype(BF16)
        acc_ref[...] = jnp.zeros(acc_ref.shape, F32)

    h = h_ref[...]
    a = jnp.dot(h, wg_ref[...], preferred_element_type=F32)
    u = jnp.dot(h, wu_ref[...], preferred_element_type=F32)
    act = a * (1.0 / (1.0 + jnp.exp(-a))) * u
    acc_ref[...] += jnp.dot(act.astype(BF16), wd_ref[...], preferred_element_type=F32)

    @pl.when(f == pl.num_programs(1) - 1)
    def _():
        o_ref[...] = x_ref[...] + gate_ref[0] * acc_ref[...]


def _ffn(xs, g, mod, wg, wu, wd, tm, tf, modrow):
    t, d = xs.shape
    ff = wg.shape[1]
    return pl.pallas_call(
        _ffn_kernel,
        grid=(t // tm, ff // tf),
        in_specs=[pl.BlockSpec((tm, d), lambda i, f: (i, 0)),
                  pl.BlockSpec((1, d), lambda i, f: (0, 0)),
                  pl.BlockSpec((1, 1, d), lambda i, f: (modrow(i), 0, 4)),
                  pl.BlockSpec((1, 1, d), lambda i, f: (modrow(i), 0, 3)),
                  pl.BlockSpec((1, 1, d), lambda i, f: (modrow(i), 0, 5)),
                  pl.BlockSpec((d, tf), lambda i, f: (0, f)),
                  pl.BlockSpec((d, tf), lambda i, f: (0, f)),
                  pl.BlockSpec((tf, d), lambda i, f: (f, 0))],
        out_specs=pl.BlockSpec((tm, d), lambda i, f: (i, 0)),
        out_shape=jax.ShapeDtypeStruct((t, d), F32),
        scratch_shapes=[pltpu.VMEM((tm, d), BF16), pltpu.VMEM((tm, d), F32)],
        compiler_params=_cparams(("parallel", "arbitrary")),
        name="ffn_swiglu",
    )(xs, g.reshape(1, d), mod, mod, mod, wg, wu, wd)


def _router_kernel(x_ref, g_ref, sc_ref, sh_ref, rt_ref, h_ref, r_ref):
    lane = lax.broadcasted_iota(jnp.int32, (1, LANES), 1)
    hf = _norm_mod(x_ref[...], g_ref[...], sc_ref[0], sh_ref[0])
    h_ref[...] = hf
    logits = jnp.dot(hf, rt_ref[...], precision=lax.Precision.HIGHEST, preferred_element_type=F32)
    lanef = lane.astype(F32)
    lg = jnp.where(lane < N_EXPERTS, logits, NEG)
    m1 = jnp.max(lg, axis=1, keepdims=True)
    i1 = jnp.min(jnp.where(lg == m1, lanef, float(LANES)), axis=1, keepdims=True)
    lg2 = jnp.where(lanef == i1, NEG, lg)
    m2 = jnp.max(lg2, axis=1, keepdims=True)
    i2 = jnp.min(jnp.where(lg2 == m2, lanef, float(LANES)), axis=1, keepdims=True)
    e2 = jnp.exp(m2 - m1)
    w1 = 1.0 / (1.0 + e2)
    r_ref[...] = jnp.where(lane == 0, i1, jnp.where(lane == 1, i2, jnp.where(lane == 2, w1,
                           jnp.where(lane == 3, e2 * w1, 0.0))))


def _router(xs, g, mod, router_p, tm, modrow):
    t, d = xs.shape
    return pl.pallas_call(
        _router_kernel,
        grid=(t // tm,),
        in_specs=[pl.BlockSpec((tm, d), lambda i: (i, 0)),
                  pl.BlockSpec((1, d), lambda i: (0, 0)),
                  pl.BlockSpec((1, 1, d), lambda i: (modrow(i), 0, 4)),
                  pl.BlockSpec((1, 1, d), lambda i: (modrow(i), 0, 3)),
                  pl.BlockSpec((d, LANES), lambda i: (0, 0))],
        out_specs=[pl.BlockSpec((tm, d), lambda i: (i, 0)), pl.BlockSpec((tm, LANES), lambda i: (i, 0))],
        out_shape=[jax.ShapeDtypeStruct((t, d), F32), jax.ShapeDtypeStruct((t, LANES), F32)],
        compiler_params=_cparams(("parallel",)),
        name="moe_router",
    )(xs, g.reshape(1, d), mod, mod, router_p)


def _route_plan(rinfo, tg):
    t = rinfo.shape[0]
    n_tiles = -(-2 * t // tg) + N_EXPERTS
    e_flat = jnp.concatenate([rinfo[:, 0], rinfo[:, 1]]).astype(jnp.int32)
    w_flat = jnp.concatenate([rinfo[:, 2], rinfo[:, 3]])
    onehot = (e_flat[:, None] == jnp.arange(N_EXPERTS, dtype=jnp.int32)[None, :]).astype(jnp.int32)
    csum = jnp.cumsum(onehot, axis=0)
    rank = jnp.sum(onehot * (csum - 1), axis=1)
    counts = csum[-1]
    padded = (counts + tg - 1) // tg * tg
    ends = jnp.cumsum(padded)
    pos = jnp.sum(onehot * (ends - padded)[None, :], axis=1) + rank
    tok = jnp.tile(jnp.arange(t, dtype=jnp.int32), 2)
    src_tok = jnp.zeros((n_tiles * tg,), jnp.int32).at[pos].set(tok)
    row_w = jnp.zeros((n_tiles * tg,), F32).at[pos].set(w_flat)
    tile_start = jnp.arange(n_tiles, dtype=jnp.int32) * tg
    tile_expert = jnp.minimum(jnp.sum((tile_start[:, None] >= ends[None, :]).astype(jnp.int32), axis=1),
                              N_EXPERTS - 1)
    n_used = (ends[-1] // tg).reshape(1)
    return src_tok, row_w, tile_expert, n_used, pos


def _row_copy(src_ref, row, dst_ref, r, sem):
    return pltpu.make_async_copy(src_ref.at[pl.ds(row, 1), :], dst_ref.at[pl.ds(r, 1), :], sem)


def _gather_rows_kernel(idx_ref, src_ref, o_ref, sem):
    rows = o_ref.shape[0]

    def issue(r, carry):
        _row_copy(src_ref, idx_ref[0, 0, r], o_ref, r, sem).start()
        return carry
    lax.fori_loop(0, rows, issue, 0, unroll=8)

    def drain(r, carry):
        _row_copy(src_ref, 0, o_ref, r, sem).wait()
        return carry
    lax.fori_loop(0, rows, drain, 0, unroll=8)


def _gather_rows(src, idx, rows):
    n = idx.shape[0]
    d = src.shape[1]
    return pl.pallas_call(
        _gather_rows_kernel,
        grid=(n // rows,),
        in_specs=[pl.BlockSpec((1, 1, rows), lambda i: (i, 0, 0), memory_space=pltpu.SMEM),
                  pl.BlockSpec(memory_space=pl.ANY)],
        out_specs=pl.BlockSpec((rows, d), lambda i: (i, 0)),
        out_shape=jax.ShapeDtypeStruct((n, d), src.dtype),
        scratch_shapes=[pltpu.SemaphoreType.DMA(())],
        compiler_params=_cparams(("arbitrary",)),
        name="moe_gather",
    )(idx.reshape(n // rows, 1, rows), src)


def _moe_experts_kernel(te_ref, nu_ref, x_ref, w_ref, wg_ref, wu_ref, wd_ref, o_ref, h_ref, acc_ref):
    i = pl.program_id(0)
    f = pl.program_id(1)
    last = pl.num_programs(1) - 1
    used = i < nu_ref[0]

    @pl.when(used & (f == 0))
    def _():
        h_ref[...] = x_ref[...].astype(BF16)
        acc_ref[...] = jnp.zeros(acc_ref.shape, F32)

    @pl.when(used)
    def _():
        h = h_ref[...]
        a = jnp.dot(h, wg_ref[0], preferred_element_type=F32)
        u = jnp.dot(h, wu_ref[0], preferred_element_type=F32)
        act = a * (1.0 / (1.0 + jnp.exp(-a))) * u
        acc_ref[...] += jnp.dot(act.astype(BF16), wd_ref[0], preferred_element_type=F32)

    @pl.when(used & (f == last))
    def _():
        o_ref[...] = acc_ref[...] * w_ref[...]

    @pl.when(jnp.logical_not(used) & (f == last))
    def _():
        o_ref[...] = jnp.zeros(o_ref.shape, F32)


def _moe_experts(xg, row_w, tile_expert, n_used, wg, wu, wd, tg, tf):
    p, d = xg.shape
    ff = wg.shape[2]
    grid_spec = pltpu.PrefetchScalarGridSpec(
        num_scalar_prefetch=2,
        grid=(p // tg, ff // tf),
        in_specs=[pl.BlockSpec((tg, d), lambda i, f, te, nu: (i, 0)),
                  pl.BlockSpec((tg, 1), lambda i, f, te, nu: (i, 0)),
                  pl.BlockSpec((1, d, tf), lambda i, f, te, nu: (te[i], 0, f)),
                  pl.BlockSpec((1, d, tf), lambda i, f, te, nu: (te[i], 0, f)),
                  pl.BlockSpec((1, tf, d), lambda i, f, te, nu: (te[i], f, 0))],
        out_specs=pl.BlockSpec((tg, d), lambda i, f, te, nu: (i, 0)),
        scratch_shapes=[pltpu.VMEM((tg, d), BF16), pltpu.VMEM((tg, d), F32)])
    return pl.pallas_call(
        _moe_experts_kernel,
        grid_spec=grid_spec,
        out_shape=jax.ShapeDtypeStruct((p, d), F32),
        compiler_params=_cparams(("arbitrary", "arbitrary")),
        name="moe_experts",
    )(tile_expert, n_used, xg, row_w.reshape(p, 1), wg, wu, wd)


def _moe_combine_kernel(i0_ref, i1_ref, y_ref, x_ref, gate_ref, o_ref, b0_ref, b1_ref, sem):
    rows = o_ref.shape[0]

    def issue(r, carry):
        _row_copy(y_ref, i0_ref[0, 0, r], b0_ref, r, sem).start()
        _row_copy(y_ref, i1_ref[0, 0, r], b1_ref, r, sem).start()
        return carry
    lax.fori_loop(0, rows, issue, 0, unroll=8)

    def drain(r, carry):
        _row_copy(y_ref, 0, b0_ref, r, sem).wait()
        _row_copy(y_ref, 0, b1_ref, r, sem).wait()
        return carry
    lax.fori_loop(0, rows, drain, 0, unroll=8)
    o_ref[...] = x_ref[...] + gate_ref[0] * (b0_ref[...] + b1_ref[...])


def _moe_combine(xs, yg, pos, mod, tm, modrow):
    t, d = xs.shape
    p3 = pos.reshape(2, t // tm, 1, tm)
    return pl.pallas_call(
        _moe_combine_kernel,
        grid=(t // tm,),
        in_specs=[pl.BlockSpec((1, 1, tm), lambda i: (i, 0, 0), memory_space=pltpu.SMEM),
                  pl.BlockSpec((1, 1, tm), lambda i: (i, 0, 0), memory_space=pltpu.SMEM),
                  pl.BlockSpec(memory_space=pl.ANY),
                  pl.BlockSpec((tm, d), lambda i: (i, 0)),
                  pl.BlockSpec((1, 1, d), lambda i: (modrow(i), 0, 5))],
        out_specs=pl.BlockSpec((tm, d), lambda i: (i, 0)),
        out_shape=jax.ShapeDtypeStruct((t, d), F32),
        scratch_shapes=[pltpu.VMEM((tm, d), F32), pltpu.VMEM((tm, d), F32), pltpu.SemaphoreType.DMA(())],
        compiler_params=_cparams(("arbitrary",)),
        name="moe_combine",
    )(p3[0], p3[1], yg, xs, mod)


def _moe(xs, g, mod, router_p, wg, wu, wd, tm, tf, modrow):
    tg = tm
    h2, rinfo = _router(xs, g, mod, router_p, tm, modrow)
    src_tok, row_w, tile_expert, n_used, pos = _route_plan(rinfo, tg)
    xg = _gather_rows(h2, src_tok, 512)
    yg = _moe_experts(xg, row_w, tile_expert, n_used, wg, wu, wd, tg, tf)
    return _moe_combine(xs, yg, pos, mod, tm, modrow)


def _final_norm_kernel(x_ref, g_ref, o_ref):
    x = x_ref[...]
    ms = jnp.mean(x * x, axis=-1, keepdims=True)
    o_ref[...] = x * lax.rsqrt(ms + EPS) * g_ref[...]


def _final_norm(xs, g, rows, tm):
    d = xs.shape[1]
    return pl.pallas_call(
        _final_norm_kernel,
        grid=(rows // tm,),
        in_specs=[pl.BlockSpec((tm, d), lambda i: (i, 0)), pl.BlockSpec((1, d), lambda i: (0, 0))],
        out_specs=pl.BlockSpec((tm, d), lambda i: (i, 0)),
        out_shape=jax.ShapeDtypeStruct((rows, d), F32),
        compiler_params=_cparams(("parallel",)),
        name="final_norm",
    )(xs, g.reshape(1, d))


def _pad_heads(v, width, used):
    lead = v.shape[:-1]
    v = v.reshape(lead + (C_HEADS, used))
    v = jnp.pad(v, [(0, 0)] * len(lead) + [(0, 0), (0, width - used)])
    return v.reshape(lead + (C_HEADS * width,))


def kernel(x, c, ctx, c_ctx, norm1_g, norm2_g, ada_w, ada_b, w_in, w_out, a_lambda, a_norm_g, b_sink,
           c_gate_w2, c_gate_b, c_norm_g, ffn_w_gate, ffn_w_up, ffn_w_down, moe_router, moe_w_gate,
           moe_w_up, moe_w_down, final_g):
    nb, seq, d = x.shape
    nctx = ctx.shape[1]
    depth = w_in.shape[0]
    n_lat = nb * seq
    tm = nb * nctx
    assert seq % tm == 0 and nb < 8
    tm_r = tm // 2
    n_lat_tiles = n_lat // tm

    def modrow(i):
        return jnp.where(i < n_lat_tiles, i // (seq // tm), nb)

    xs = jnp.concatenate([x.reshape(n_lat, d), ctx.reshape(nb * nctx, d)], axis=0)
    cc = jnp.zeros((8, d), F32).at[:nb].set(c).at[nb].set(c_ctx)
    mod_all = _modulation(cc, ada_w, ada_b).reshape(depth, 8, 1, 6 * d)
    tables = _rope_tables(seq, tm_r)

    for layer in range(depth):
        lam_init = 0.8 - 0.6 * math.exp(-0.3 * layer)
        mod = mod_all[layer]
        w_pad = jnp.concatenate([w_in[layer], jnp.zeros((d, 1), F32)], axis=1)
        w_rope = jnp.take(w_pad, _ROPE_COLS, axis=1).astype(BF16)
        w_plain = jnp.take(w_pad, _PLAIN_COLS, axis=1).astype(BF16)

        aq, ak, bq, bk = _normproj(xs, norm1_g[layer], mod, 1, 0, w_rope, ROPE_WIDTHS, tm_r,
                                   n_lat // tm_r, seq // tm_r, nb, tables)
        av, bv, cq, ck, cv, cr, cg = _normproj(xs, norm1_g[layer], mod, 1, 0, w_plain, PLAIN_WIDTHS, tm,
                                               n_lat_tiles, seq // tm, nb)

        ones_l = jnp.ones((nb, A_HEADS, 8, seq), BF16)
        ones_c = jnp.ones((nb, A_HEADS, 8, nctx), BF16)
        avt_lat = jnp.concatenate(
            [av[:n_lat].reshape(nb, seq, A_HEADS, A_V).transpose(0, 2, 3, 1), ones_l], axis=2)
        avt_ctx = jnp.concatenate(
            [av[n_lat:].reshape(nb, nctx, A_HEADS, A_V).transpose(0, 2, 3, 1), ones_c], axis=2)
        bvt_lat = bv[:n_lat].reshape(nb, seq, LANES).transpose(0, 2, 1)
        bvt_ctx = bv[n_lat:].reshape(nb, nctx, LANES).transpose(0, 2, 1)
        cvt = cv.T

        a_lat = _diff_attn(aq, ak, avt_ctx, avt_lat, a_lambda[layer], a_norm_g[layer], lam_init,
                           nb, seq, nctx, True)
        a_ctx = _diff_attn(aq, ak, avt_ctx, None, a_lambda[layer], a_norm_g[layer], lam_init,
                           nb, seq, nctx, False)
        sink_row = jnp.zeros((1, LANES), F32).at[0, :B_HEADS].set(b_sink[layer] * LOG2E)
        b_lat = _win_attn(bq, bk, bvt_ctx, bvt_lat, sink_row, nb, seq, nctx, True)
        b_ctx = _win_attn(bq, bk, bvt_ctx, None, sink_row, nb, seq, nctx, False)

        w2 = c_gate_w2[layer]
        w2p = [jnp.zeros((LANES, C_HEADS * LANES), F32).at[dd * C_RANK:(dd + 1) * C_RANK].set(
            _pad_heads(w2[dd], LANES, C_DK)).astype(BF16) for dd in range(2)]
        bgp = [_pad_heads(c_gate_b[layer, dd], LANES, C_DK).reshape(1, -1) for dd in range(2)]
        ng = _pad_heads(jnp.tile(c_norm_g[layer], C_HEADS), LANES, C_DV).reshape(1, -1)
        o_f = _gla_dir(cq, ck, cv, cvt, cg, w2p[0], bgp[0], nb, seq, nctx, False)
        g_out = _gla_dir(cq, ck, cv, cvt, cg, w2p[1], bgp[1], nb, seq, nctx, True, o_f, cr, ng)

        a_all = jnp.concatenate([a_lat, a_ctx], axis=0)
        b_all = jnp.concatenate([b_lat, b_ctx], axis=0)
        wo = w_out[layer]
        wa = wo[:256].astype(BF16)
        wb = wo[256:640].astype(BF16)
        wc = jnp.pad(wo[640:].reshape(C_HEADS, C_DV, d), ((0, 0), (0, LANES - C_DV), (0, 0))).reshape(
            C_HEADS * LANES, d).astype(BF16)
        xs = _outproj(xs, a_all, b_all, g_out, wa, wb, wc, mod, tm, modrow)

        j = layer // 2
        if layer % 2 == 0:
            xs = _ffn(xs, norm2_g[layer], mod, ffn_w_gate[j].astype(BF16), ffn_w_up[j].astype(BF16),
                      ffn_w_down[j].astype(BF16), tm, 256, modrow)
        else:
            router_p = jnp.pad(moe_router[j], ((0, 0), (0, LANES - N_EXPERTS)))
            xs = _moe(xs, norm2_g[layer], mod, router_p, moe_w_gate[j].astype(BF16),
                      moe_w_up[j].astype(BF16), moe_w_down[j].astype(BF16), tm, 512, modrow)

    return _final_norm(xs, final_g, n_lat, tm).reshape(nb, seq, d)
```

```python
import functools
import math

import numpy as np
import jax
import jax.numpy as jnp
from jax import lax
from jax.experimental import pallas as pl
from jax.experimental.pallas import tpu as pltpu

F32 = jnp.float32
BF16 = jnp.bfloat16

EPS = 1e-6
ROPE_BASE = 10000.0
GRID_W = 64
LANES = 128
LOG2E = math.log2(math.e)
NEG = -1e30

A_HEADS, A_QK, A_V = 4, 32, 64
B_HEADS, B_KV, B_HD, WINDOW, BLOCK = 6, 2, 64, 128, 128
C_HEADS, C_DK, C_DV, C_RANK, C_GATE_NORM, C_CHUNK = 4, 48, 96, 16, 16.0, 64
N_EXPERTS = 8
IN_SIZES = (256, 256, 256, 384, 128, 128, 192, 192, 384, 384, 32)
IN_W = sum(IN_SIZES)

ROPE_WIDTHS = (256, 256, 768, 128)
PLAIN_WIDTHS = (512, 512, 512, 512, 128)
VT_WIDTHS = (256, 128, 512)
ROPE_W = sum(ROPE_WIDTHS)
GLA_GROUP = 256
VMEM_LIMIT = 48 * 1024 * 1024


def _cparams(sem):
    return pltpu.CompilerParams(dimension_semantics=sem, vmem_limit_bytes=VMEM_LIMIT)


def _column_maps():
    off = np.concatenate([[0], np.cumsum(IN_SIZES)])
    aq0, ak0, av0, bq0, bk0, bv0, cq0, ck0, cv0, cr0, cg0 = [int(v) for v in off[:11]]
    zero = IN_W

    def a_partner(d):
        return d + 8 if d % 16 < 8 else d - 8

    def b_partner(d):
        return d + 16 if d % 32 < 16 else d - 16

    main, part, dim, dd, scale = [], [], [], [], []
    a_scale = A_QK ** -0.5 * LOG2E
    b_scale = B_HD ** -0.5 * LOG2E
    for base, sc in ((aq0, a_scale), (ak0, 1.0)):
        for j in range(256):
            d = j % 32
            main.append(base + j); part.append(base + j - d + a_partner(d))
            dim.append(32); dd.append(d); scale.append(sc)
    for t in range(B_HEADS):
        g = t // (B_HEADS // B_KV)
        for lane in range(LANES):
            d = lane % 64
            if lane // 64 == g:
                main.append(bq0 + t * 64 + d); part.append(bq0 + t * 64 + b_partner(d))
            else:
                main.append(zero); part.append(zero)
            dim.append(64); dd.append(d); scale.append(b_scale)
    for j in range(128):
        d = j % 64
        main.append(bk0 + j); part.append(bk0 + j - d + b_partner(d))
        dim.append(64); dd.append(d); scale.append(1.0)

    plain = []
    for base in (cq0, ck0):
        for h in range(C_HEADS):
            plain += [base + h * C_DK + d if d < C_DK else zero for d in range(LANES)]
    for base in (cv0, cr0):
        for h in range(C_HEADS):
            plain += [base + h * C_DV + d if d < C_DV else zero for d in range(LANES)]
    plain += [cg0 + d if d < 2 * C_RANK else zero for d in range(LANES)]
    vt = list(range(av0, av0 + 256)) + list(range(bv0, bv0 + 128))
    for h in range(C_HEADS):
        vt += [cv0 + h * C_DV + d if d < C_DV else zero for d in range(LANES)]
    return (np.array(main + part, np.int32), np.array(plain, np.int32), np.array(vt, np.int32),
            np.array(dim), np.array(dd), np.array(scale, np.float32))


_ROPE_COLS, _PLAIN_COLS, _VT_COLS, _R_DIM, _R_D, _R_SCALE = _column_maps()


def _rope_tables(seq, pad_rows):
    sec_start = np.cumsum((0,) + ROPE_WIDTHS[:-1])
    cols = np.concatenate([np.arange(s, s + LANES) for s in sec_start])
    r_dim, r_d, r_scale = _R_DIM[cols], _R_D[cols], _R_SCALE[cols]
    pos = jnp.arange(seq, dtype=jnp.int32)
    row = (pos // GRID_W).astype(F32)[:, None]
    col = (pos % GRID_W).astype(F32)[:, None]
    quarter = r_dim // 4
    half = r_dim // 2
    is_col = (r_d % r_dim) >= half
    ddh = r_d % half
    first = ddh < quarter
    f = (ddh % quarter).astype(np.float32)
    inv = jnp.asarray(ROPE_BASE, F32) ** (-jnp.asarray(f) / jnp.asarray(quarter.astype(np.float32)))
    ang = jnp.where(jnp.asarray(is_col)[None, :], col, row) * inv[None, :]
    scale = jnp.asarray(r_scale)[None, :]
    cos = jnp.cos(ang) * scale
    sin = jnp.where(jnp.asarray(first)[None, :], -jnp.sin(ang), jnp.sin(ang)) * scale
    cos = jnp.concatenate([cos, jnp.broadcast_to(scale, (pad_rows, cos.shape[1]))], axis=0)
    sin = jnp.concatenate([sin, jnp.zeros((pad_rows, sin.shape[1]), F32)], axis=0)
    nsec = len(ROPE_WIDTHS)
    return jnp.stack([cos.reshape(-1, nsec, LANES), sin.reshape(-1, nsec, LANES)], axis=2).reshape(
        -1, 2 * nsec * LANES)


def _mod_kernel(c_ref, w_ref, b_ref, o_ref):
    c = c_ref[...]
    s = c * (1.0 / (1.0 + jnp.exp(-c)))
    o_ref[0] = jnp.dot(s, w_ref[0], precision=lax.Precision.HIGHEST,
                       preferred_element_type=F32) + b_ref[0]


def _modulation(cc, ada_w, ada_b):
    depth, d, n = ada_w.shape
    tn = n // 4
    return pl.pallas_call(
        _mod_kernel,
        grid=(depth, n // tn),
        in_specs=[pl.BlockSpec((8, d), lambda l, j: (0, 0)),
                  pl.BlockSpec((1, d, tn), lambda l, j: (l, 0, j)),
                  pl.BlockSpec((1, 1, tn), lambda l, j: (l, 0, j))],
        out_specs=pl.BlockSpec((1, 8, tn), lambda l, j: (l, 0, j)),
        out_shape=jax.ShapeDtypeStruct((depth, 8, n), F32),
        compiler_params=_cparams(("arbitrary", "arbitrary")),
        name="adaln_mod",
    )(cc, ada_w, ada_b.reshape(depth, 1, n))


def _norm_mod(x, g, sc, sh):
    ms = jnp.mean(x * x, axis=-1, keepdims=True)
    return (x * lax.rsqrt(ms + EPS) * g) * (1.0 + sc) + sh


def _normproj_kernel(x_ref, g_ref, sc_ref, sh_ref, w_ref, *rest, rope, widths, vt_widths):
    if rope:
        t_ref = rest[0]
        rest = rest[1:]
    if vt_widths:
        wt_ref = rest[0]
        rest = rest[1:]
    outs = rest[:len(widths)]
    vt_outs = rest[len(widths):]
    h = _norm_mod(x_ref[...], g_ref[...], sc_ref[0], sh_ref[0]).astype(BF16)
    acc = jnp.dot(h, w_ref[...], preferred_element_type=F32)
    rw = sum(widths)
    off = 0
    for sec, (o_ref, w) in enumerate(zip(outs, widths)):
        if rope:
            cos = t_ref[:, (2 * sec) * LANES:(2 * sec + 1) * LANES]
            sin = t_ref[:, (2 * sec + 1) * LANES:(2 * sec + 2) * LANES]
            for j in range(w // LANES):
                c0 = off + j * LANES
                o_ref[:, j * LANES:(j + 1) * LANES] = (
                    acc[:, c0:c0 + LANES] * cos + acc[:, rw + c0:rw + c0 + LANES] * sin).astype(o_ref.dtype)
        else:
            o_ref[...] = acc[:, off:off + w].astype(o_ref.dtype)
        off += w
    off = 0
    for o_ref, w in zip(vt_outs, vt_widths):
        o_ref[...] = lax.dot_general(wt_ref[off:off + w, :], h, (((1,), (1,)), ((), ())),
                                     preferred_element_type=F32).astype(o_ref.dtype)
        off += w


def _normproj(xs, g, mod, sc_chunk, sh_chunk, w, widths, tm, n_lat_tiles, tiles_per_batch, nb, tables=None,
              wt=None, vt_widths=()):
    t, d = xs.shape
    rope = tables is not None

    def modrow(i):
        return jnp.where(i < n_lat_tiles, i // tiles_per_batch, nb)

    in_specs = [pl.BlockSpec((tm, d), lambda i: (i, 0)),
                pl.BlockSpec((1, d), lambda i: (0, 0)),
                pl.BlockSpec((1, 1, d), lambda i: (modrow(i), 0, sc_chunk)),
                pl.BlockSpec((1, 1, d), lambda i: (modrow(i), 0, sh_chunk)),
                pl.BlockSpec(w.shape, lambda i: (0, 0))]
    args = [xs, g.reshape(1, d), mod, mod, w]
    if rope:
        def tabrow(i):
            return jnp.where(i < n_lat_tiles, i % tiles_per_batch, tiles_per_batch)
        in_specs += [pl.BlockSpec((tm, tables.shape[1]), lambda i: (tabrow(i), 0))]
        args += [tables]
    if vt_widths:
        in_specs += [pl.BlockSpec(wt.shape, lambda i: (0, 0))]
        args += [wt]
    return pl.pallas_call(
        functools.partial(_normproj_kernel, rope=rope, widths=widths, vt_widths=vt_widths),
        grid=(t // tm,),
        in_specs=in_specs,
        out_specs=([pl.BlockSpec((tm, wd), lambda i: (i, 0)) for wd in widths]
                   + [pl.BlockSpec((wd, tm), lambda i: (0, i)) for wd in vt_widths]),
        out_shape=([jax.ShapeDtypeStruct((t, wd), BF16) for wd in widths]
                   + [jax.ShapeDtypeStruct((wd, t), BF16) for wd in vt_widths]),
        compiler_params=_cparams(("parallel",)),
        name="normproj_rope" if rope else "normproj_plain",
    )(*args)


def _diff_attn_kernel(*refs, lam_init, has_lat, tk):
    if has_lat:
        q_ref, kc_ref, vtc_ref, kl_ref, vtl_ref, lam_ref, g_ref, o_ref, m_scr, acc_scr, mc_scr, s_scr = refs
    else:
        q_ref, kc_ref, vtc_ref, lam_ref, g_ref, o_ref, m_scr, acc_scr, mc_scr, s_scr = refs
    q = q_ref[...]
    lane = lax.broadcasted_iota(jnp.int32, (1, LANES), 1)
    nt = (((1,), (1,)), ((), ()))
    qms = [jnp.where((lane >= i * A_QK) & (lane < (i + 1) * A_QK), q, jnp.zeros_like(q)) for i in range(4)]

    sub = 512

    def scores(k, slot, nk):
        for i in range(4):
            s = lax.dot_general(k, qms[i], nt, preferred_element_type=F32)
            s_scr[slot, i, 0:nk, :] = s
            mc_scr[slot, i] = jnp.max(s, axis=0, keepdims=True)

    def consume(vt_of, slot, nk, first):
        for i in range(4):
            m_cur = mc_scr[slot, i]
            if first:
                m_new = m_cur
            else:
                m_run = m_scr[i]
                m_new = jnp.maximum(m_run, m_cur)
            pv = None
            for t in range(nk // min(sub, nk)):
                w = min(sub, nk)
                p = jnp.exp2(s_scr[slot, i, t * w:(t + 1) * w, :] - m_new).astype(BF16)
                d = jnp.dot(vt_of(i // 2, t, w), p, preferred_element_type=F32)
                pv = d if pv is None else pv + d
            if first:
                acc_scr[i] = pv
            else:
                acc_scr[i] = jnp.exp2(m_run - m_new) * acc_scr[i] + pv
            m_scr[i] = m_new

    nctx = kc_ref.shape[0]
    scores(kc_ref[...], 1, nctx)

    def with_ones(vt):
        return jnp.concatenate([vt, jnp.ones((8, vt.shape[1]), vt.dtype)], axis=0)

    def vt_ctx(hh, t, w):
        return with_ones(vtc_ref[hh * A_V:(hh + 1) * A_V, t * w:(t + 1) * w])

    if not has_lat:
        consume(vt_ctx, 1, nctx, True)
    else:
        n_chunks = kl_ref.shape[0] // tk

        def k_lat(c):
            return kl_ref[pl.ds(pl.multiple_of(c * tk, tk), tk), :]

        def vt_lat(c):
            def get(hh, t, w):
                return with_ones(vtl_ref[hh * A_V:(hh + 1) * A_V, pl.ds(pl.multiple_of(c * tk + t * w, w), w)])
            return get

        scores(k_lat(0), 0, tk)
        consume(vt_ctx, 1, nctx, True)

        def body(j, carry):
            c = 2 * j
            scores(k_lat(c + 1), 1, tk)
            consume(vt_lat(c), 0, tk, False)
            scores(k_lat(c + 2), 0, tk)
            consume(vt_lat(c + 1), 1, tk, False)
            return carry
        pairs = (n_chunks - 1) // 2
        lax.fori_loop(0, pairs, body, 0)
        c_last = 2 * pairs
        if c_last + 1 < n_chunks:
            scores(k_lat(c_last + 1), 1, tk)
            consume(vt_lat(c_last), 0, tk, False)
            consume(vt_lat(c_last + 1), 1, tk, False)
        else:
            consume(vt_lat(c_last), 0, tk, False)

    lp = lam_ref[...]
    lam = (jnp.exp(jnp.sum(lp[0:1] * lp[1:2], axis=1, keepdims=True))
           - jnp.exp(jnp.sum(lp[2:3] * lp[3:4], axis=1, keepdims=True)) + lam_init)
    heads = []
    for hh in range(2):
        maps = []
        for m in range(2):
            acc = acc_scr[hh * 2 + m]
            maps.append(acc[0:A_V] / acc[A_V:A_V + 1])
        oh = maps[0] - lam * maps[1]
        ms = jnp.mean(oh * oh, axis=0, keepdims=True)
        heads.append(oh * lax.rsqrt(ms + EPS) * g_ref[...] * (1.0 - lam_init))
    o_ref[...] = jnp.concatenate(heads, axis=0).T.astype(o_ref.dtype)


def _diff_attn(aq, ak, avt, lam_p, norm_g, lam_init, nb, seq, ctx, latent):
    tq = 256
    tk = min(1024, seq)
    kern = functools.partial(_diff_attn_kernel, lam_init=lam_init, has_lat=latent, tk=tk)
    ctx_blk0 = nb * seq // ctx
    vtc_spec = pl.BlockSpec((2 * A_V, ctx), lambda b, p, i: (p, ctx_blk0 + b))
    kc_spec = pl.BlockSpec((ctx, LANES), lambda b, p, i: (ctx_blk0 + b, p))
    par_specs = [pl.BlockSpec((4, A_QK), lambda b, p, i: (0, 0)),
                 pl.BlockSpec((A_V, 1), lambda b, p, i: (0, 0))]
    if latent:
        nq = seq // tq
        in_specs = [pl.BlockSpec((tq, LANES), lambda b, p, i: (b * nq + i, p)), kc_spec, vtc_spec,
                    pl.BlockSpec((seq, LANES), lambda b, p, i: (b, p)),
                    pl.BlockSpec((2 * A_V, seq), lambda b, p, i: (p, b))] + par_specs
        args = (aq, ak, avt, ak, avt, lam_p, norm_g.reshape(A_V, 1))
        rows = nb * seq
    else:
        nq = ctx // tq
        q_blk0 = nb * seq // tq
        in_specs = [pl.BlockSpec((tq, LANES), lambda b, p, i: (q_blk0 + b * nq + i, p)),
                    kc_spec, vtc_spec] + par_specs
        args = (aq, ak, avt, lam_p, norm_g.reshape(A_V, 1))
        rows = nb * ctx
    return pl.pallas_call(
        kern,
        grid=(nb, 2, nq),
        in_specs=in_specs,
        out_specs=pl.BlockSpec((tq, LANES), lambda b, p, i: (b * nq + i, p)),
        out_shape=jax.ShapeDtypeStruct((rows, 2 * LANES), BF16),
        scratch_shapes=[pltpu.VMEM((4, 1, tq), F32), pltpu.VMEM((4, A_V + 8, tq), F32),
                        pltpu.VMEM((2, 4, 1, tq), F32),
                        pltpu.VMEM((2, 4, tk if latent else ctx, tq), F32)],
        compiler_params=_cparams(("parallel", "parallel", "arbitrary")),
        name="diff_attn_lat" if latent else "diff_attn_ctx",
    )(*args)


def _win_attn_kernel(*refs, with_window, seq, ctx):
    if with_window:
        q_ref, kc_ref, vtc_ref, kp_ref, kn0_ref, kn_ref, vtp_ref, vt0_ref, vtn_ref, sink_ref, o_ref = refs
        n = pl.program_id(1)
        k_all = jnp.concatenate([kc_ref[...], kp_ref[...], kn0_ref[...], kn_ref[...]], axis=0)
        vt_all = jnp.concatenate([vtc_ref[...], vtp_ref[...], vt0_ref[...], vtn_ref[...]], axis=1)
        nk = ctx + 3 * BLOCK
        r = lax.broadcasted_iota(jnp.int32, (nk, BLOCK), 0)
        c = lax.broadcasted_iota(jnp.int32, (nk, BLOCK), 1)
        krel = r - (ctx + BLOCK)
        in_win = ((jnp.abs(c - krel) <= WINDOW) & (krel >= -n * BLOCK) & (krel < seq - n * BLOCK))
        is_ctx = r < ctx
    else:
        q_ref, kc_ref, vtc_ref, sink_ref, o_ref = refs
        k_all = kc_ref[...]
        vt_all = vtc_ref[...]
    lane = lax.broadcasted_iota(jnp.int32, (1, LANES), 1)
    sink_row = sink_ref[...]
    nt = (((1,), (1,)), ((), ()))
    rep = B_HEADS // B_KV
    outs = []
    for j in range(B_HEADS):
        g = j // rep
        s = lax.dot_general(k_all, q_ref[:, j * LANES:(j + 1) * LANES], nt, preferred_element_type=F32)
        if with_window:
            s = jnp.where(is_ctx, s, jnp.where(in_win, s, NEG))
        sk = jnp.max(jnp.where(lane == j, sink_row, NEG), axis=1, keepdims=True)
        m = jnp.maximum(jnp.max(s, axis=0, keepdims=True), sk)
        p = jnp.exp2(s - m)
        l = jnp.sum(p, axis=0, keepdims=True) + jnp.exp2(sk - m)
        ot = jnp.dot(vt_all, p.astype(BF16), preferred_element_type=F32)
        outs.append(ot[g * B_HD:(g + 1) * B_HD] / l)
    for t in range(B_HEADS // 2):
        pair = jnp.concatenate([outs[2 * t], outs[2 * t + 1]], axis=0)
        o_ref[:, t * LANES:(t + 1) * LANES] = pair.T.astype(o_ref.dtype)


def _win_attn(bq, bk, bvt, sink_row, nb, seq, ctx, latent):
    nblk = seq // BLOCK
    ctx_blk0 = nb * seq // ctx
    kern = functools.partial(_win_attn_kernel, with_window=latent, seq=seq, ctx=ctx)
    kc_spec = pl.BlockSpec((ctx, LANES), lambda b, n: (ctx_blk0 + b, 0))
    vtc_spec = pl.BlockSpec((LANES, ctx), lambda b, n: (0, ctx_blk0 + b))
    sink_spec = pl.BlockSpec((1, LANES), lambda b, n: (0, 0))
    qw = B_HEADS * LANES
    if latent:
        def prev(n):
            return jnp.maximum(n - 1, 0)

        def nxt(n):
            return jnp.minimum(n + 1, nblk - 1)

        in_specs = [pl.BlockSpec((BLOCK, qw), lambda b, n: (b * nblk + n, 0)), kc_spec, vtc_spec,
                    pl.BlockSpec((BLOCK, LANES), lambda b, n: (b * nblk + prev(n), 0)),
                    pl.BlockSpec((BLOCK, LANES), lambda b, n: (b * nblk + n, 0)),
                    pl.BlockSpec((BLOCK, LANES), lambda b, n: (b * nblk + nxt(n), 0)),
                    pl.BlockSpec((LANES, BLOCK), lambda b, n: (0, b * nblk + prev(n))),
                    pl.BlockSpec((LANES, BLOCK), lambda b, n: (0, b * nblk + n)),
                    pl.BlockSpec((LANES, BLOCK), lambda b, n: (0, b * nblk + nxt(n))),
                    sink_spec]
        args = (bq, bk, bvt, bk, bk, bk, bvt, bvt, bvt, sink_row)
        grid = (nb, nblk)
        rows = nb * seq
        nq = nblk
    else:
        nq = ctx // BLOCK
        q_blk0 = nb * seq // BLOCK
        in_specs = [pl.BlockSpec((BLOCK, qw), lambda b, n: (q_blk0 + b * nq + n, 0)),
                    kc_spec, vtc_spec, sink_spec]
        args = (bq, bk, bvt, sink_row)
        grid = (nb, nq)
        rows = nb * ctx
    return pl.pallas_call(
        kern,
        grid=grid,
        in_specs=in_specs,
        out_specs=pl.BlockSpec((BLOCK, B_HEADS * B_HD), lambda b, n: (b * nq + n, 0)),
        out_shape=jax.ShapeDtypeStruct((rows, B_HEADS * B_HD), BF16),
        compiler_params=_cparams(("parallel", "arbitrary")),
        name="win_attn_lat" if latent else "win_attn_ctx",
    )(*args)


def _split3(x):
    hi = x.astype(BF16)
    r1 = x - hi.astype(F32)
    mid = r1.astype(BF16)
    lo = (r1 - mid.astype(F32)).astype(BF16)
    return hi, mid, lo


def _gla_kernel(*refs, reverse, final):
    if final:
        q_ref, k_ref, v_ref, vt_ref, gl_ref, w2_ref, bg_ref, of_ref, r_ref, ng_ref, o_ref, st_ref = refs
    else:
        q_ref, k_ref, v_ref, vt_ref, gl_ref, w2_ref, bg_ref, o_ref, st_ref = refs
    gsz = GLA_GROUP
    nch = gsz // C_CHUNK

    @pl.when(pl.program_id(1) == 0)
    def _():
        st_ref[...] = jnp.zeros(st_ref.shape, F32)

    q = q_ref[...].astype(F32)
    k = k_ref[...].astype(F32)
    pre = jnp.dot(gl_ref[...], w2_ref[...], preferred_element_type=F32) + bg_ref[...]
    la = (jnp.minimum(pre, 0.0) - jnp.log(1.0 + jnp.exp(-jnp.abs(pre)))) * (1.0 / C_GATE_NORM)

    r = lax.broadcasted_iota(jnp.int32, (gsz, gsz), 0)
    c = lax.broadcasted_iota(jnp.int32, (gsz, gsz), 1)
    same = (r // C_CHUNK) == (c // C_CHUNK)
    tri = same & ((c >= r) if reverse else (c <= r))
    tri_b = jnp.where(tri, 1.0, 0.0).astype(BF16)
    same_b = jnp.where(same, 1.0, 0.0).astype(BF16)
    parts = _split3(la)
    bcum = sum(jnp.dot(tri_b, p, preferred_element_type=F32) for p in parts)
    btot = sum(jnp.dot(same_b, p, preferred_element_type=F32) for p in parts)
    q_dec = q * (jnp.exp(bcum) * (C_DK ** -0.5))
    k_inv = k * jnp.exp(-bcum)
    k_end = k * jnp.exp(btot - bcum)
    dec = jnp.exp(btot)

    rowid = lax.broadcasted_iota(jnp.int32, (gsz, LANES), 0) // C_CHUNK
    nt = (((1,), (1,)), ((), ()))
    order = list(range(nch))[::-1] if reverse else list(range(nch))
    outs = []
    for h in range(C_HEADS):
        sl = slice(h * LANES, (h + 1) * LANES)
        qd = q_dec[:, sl].astype(BF16)
        ki = k_inv[:, sl].astype(BF16)
        ke = k_end[:, sl].astype(BF16)
        att = lax.dot_general(qd, ki, nt, preferred_element_type=F32)
        att = jnp.where(tri, att, 0.0).astype(BF16)
        o_h = jnp.dot(att, v_ref[:, sl], preferred_element_type=F32)
        vth = vt_ref[sl, :]
        st = st_ref[h]
        inter = [None] * nch
        for ci in order:
            rows = slice(ci * C_CHUNK, (ci + 1) * C_CHUNK)
            inter[ci] = lax.dot_general(qd[rows], st.astype(BF16), nt, preferred_element_type=F32)
            ke_c = jnp.where(rowid == ci, ke, jnp.zeros_like(ke))
            upd = jnp.dot(vth, ke_c, preferred_element_type=F32)
            st = dec[ci * C_CHUNK:ci * C_CHUNK + 1, sl] * st + upd
        st_ref[h] = st
        outs.append(o_h + jnp.concatenate(inter, axis=0))
    if not final:
        for h in range(C_HEADS):
            o_ref[:, h * LANES:(h + 1) * LANES] = outs[h]
    else:
        for h in range(C_HEADS):
            sl = slice(h * LANES, (h + 1) * LANES)
            o = outs[h] + of_ref[:, sl]
            ms = jnp.sum(o * o, axis=1, keepdims=True) * (1.0 / C_DV)
            y = o * lax.rsqrt(ms + EPS) * ng_ref[:, sl]
            rr = r_ref[:, sl].astype(F32)
            o_ref[:, sl] = (y * (rr * (1.0 / (1.0 + jnp.exp(-rr))))).astype(o_ref.dtype)


def _gla_dir(cq, ck, cv, cvt, cg, w2p, bgp, nb, seq, ctx, reverse, fwd_out=None, cr=None, ng=None):
    t = cq.shape[0]
    gsz = GLA_GROUP
    assert ctx == gsz
    n_lat_g = seq // gsz
    ctx_g0 = nb * n_lat_g
    final = fwd_out is not None

    def grp(b, i):
        lat = (n_lat_g - i) if reverse else (i - 1)
        return jnp.where(i == 0, ctx_g0 + b, b * n_lat_g + lat)

    w = C_HEADS * LANES
    row_spec = pl.BlockSpec((gsz, w), lambda b, i: (grp(b, i), 0))
    in_specs = [row_spec, row_spec, row_spec,
                pl.BlockSpec((w, gsz), lambda b, i: (0, grp(b, i))),
                pl.BlockSpec((gsz, LANES), lambda b, i: (grp(b, i), 0)),
                pl.BlockSpec((LANES, w), lambda b, i: (0, 0)),
                pl.BlockSpec((1, w), lambda b, i: (0, 0))]
    args = [cq, ck, cv, cvt, cg, w2p, bgp]
    if final:
        in_specs += [row_spec, row_spec, pl.BlockSpec((1, w), lambda b, i: (0, 0))]
        args += [fwd_out, cr, ng]
    return pl.pallas_call(
        functools.partial(_gla_kernel, reverse=reverse, final=final),
        grid=(nb, n_lat_g + 1),
        in_specs=in_specs,
        out_specs=row_spec,
        out_shape=jax.ShapeDtypeStruct((t, w), BF16 if final else F32),
        scratch_shapes=[pltpu.VMEM((C_HEADS, LANES, LANES), F32)],
        compiler_params=_cparams(("parallel", "arbitrary")),
        name="gla_bwd" if reverse else "gla_fwd",
    )(*args)


def _outproj_kernel(x_ref, a_ref, b_ref, c_ref, wa_ref, wb_ref, wc_ref, g1_ref, o_ref):
    y = jnp.dot(a_ref[...], wa_ref[...], preferred_element_type=F32)
    y += jnp.dot(b_ref[...], wb_ref[...], preferred_element_type=F32)
    y += jnp.dot(c_ref[...], wc_ref[...], preferred_element_type=F32)
    o_ref[...] = x_ref[...] + g1_ref[0] * y


def _outproj(xs, a, b, c, wa, wb, wc, mod, tm, modrow):
    t, d = xs.shape
    return pl.pallas_call(
        _outproj_kernel,
        grid=(t // tm,),
        in_specs=[pl.BlockSpec((tm, d), lambda i: (i, 0)),
                  pl.BlockSpec((tm, a.shape[1]), lambda i: (i, 0)),
                  pl.BlockSpec((tm, b.shape[1]), lambda i: (i, 0)),
                  pl.BlockSpec((tm, c.shape[1]), lambda i: (i, 0)),
                  pl.BlockSpec(wa.shape, lambda i: (0, 0)),
                  pl.BlockSpec(wb.shape, lambda i: (0, 0)),
                  pl.BlockSpec(wc.shape, lambda i: (0, 0)),
                  pl.BlockSpec((1, 1, d), lambda i: (modrow(i), 0, 2))],
        out_specs=pl.BlockSpec((tm, d), lambda i: (i, 0)),
        out_shape=jax.ShapeDtypeStruct((t, d), F32),
        compiler_params=_cparams(("parallel",)),
        name="outproj",
    )(xs, a, b, c, wa, wb, wc, mod)


def _ffn_kernel(x_ref, g_ref, sc_ref, sh_ref, gate_ref, wg_ref, wu_ref, wd_ref, o_ref, h_ref, acc_ref):
    f = pl.program_id(1)

    @pl.when(f == 0)
    def _():
        h_ref[...] = _norm_mod(x_ref[...], g_ref[...], sc_ref[0], sh_ref[0]).astype(BF16)
        acc_ref[...] = jnp.zeros(acc_ref.shape, F32)

    h = h_ref[...]
    a = jnp.dot(h, wg_ref[...].astype(BF16), preferred_element_type=F32)
    u = jnp.dot(h, wu_ref[...].astype(BF16), preferred_element_type=F32)
    act = a * (1.0 / (1.0 + jnp.exp(-a))) * u
    acc_ref[...] += jnp.dot(act.astype(BF16), wd_ref[...].astype(BF16), preferred_element_type=F32)

    @pl.when(f == pl.num_programs(1) - 1)
    def _():
        o_ref[...] = x_ref[...] + gate_ref[0] * acc_ref[...]


def _ffn(xs, g, mod, wg, wu, wd, tm, tf, modrow):
    t, d = xs.shape
    ff = wg.shape[1]
    return pl.pallas_call(
        _ffn_kernel,
        grid=(t // tm, ff // tf),
        in_specs=[pl.BlockSpec((tm, d), lambda i, f: (i, 0)),
                  pl.BlockSpec((1, d), lambda i, f: (0, 0)),
                  pl.BlockSpec((1, 1, d), lambda i, f: (modrow(i), 0, 4)),
                  pl.BlockSpec((1, 1, d), lambda i, f: (modrow(i), 0, 3)),
                  pl.BlockSpec((1, 1, d), lambda i, f: (modrow(i), 0, 5)),
                  pl.BlockSpec((d, tf), lambda i, f: (0, f)),
                  pl.BlockSpec((d, tf), lambda i, f: (0, f)),
                  pl.BlockSpec((tf, d), lambda i, f: (f, 0))],
        out_specs=pl.BlockSpec((tm, d), lambda i, f: (i, 0)),
        out_shape=jax.ShapeDtypeStruct((t, d), F32),
        scratch_shapes=[pltpu.VMEM((tm, d), BF16), pltpu.VMEM((tm, d), F32)],
        compiler_params=_cparams(("parallel", "arbitrary")),
        name="ffn_swiglu",
    )(xs, g.reshape(1, d), mod, mod, mod, wg, wu, wd)


def _router_kernel(x_ref, g_ref, sc_ref, sh_ref, rt_ref, h_ref, r_ref):
    lane = lax.broadcasted_iota(jnp.int32, (1, LANES), 1)
    hf = _norm_mod(x_ref[...], g_ref[...], sc_ref[0], sh_ref[0])
    h_ref[...] = hf
    logits = jnp.dot(hf, rt_ref[...], precision=lax.Precision.HIGHEST, preferred_element_type=F32)
    lanef = lane.astype(F32)
    lg = jnp.where(lane < N_EXPERTS, logits, NEG)
    m1 = jnp.max(lg, axis=1, keepdims=True)
    i1 = jnp.min(jnp.where(lg == m1, lanef, float(LANES)), axis=1, keepdims=True)
    lg2 = jnp.where(lanef == i1, NEG, lg)
    m2 = jnp.max(lg2, axis=1, keepdims=True)
    i2 = jnp.min(jnp.where(lg2 == m2, lanef, float(LANES)), axis=1, keepdims=True)
    e2 = jnp.exp(m2 - m1)
    w1 = 1.0 / (1.0 + e2)
    r_ref[...] = jnp.where(lane == 0, i1, jnp.where(lane == 1, i2, jnp.where(lane == 2, w1,
                           jnp.where(lane == 3, e2 * w1, 0.0))))


def _router(xs, g, mod, router_p, tm, modrow):
    t, d = xs.shape
    return pl.pallas_call(
        _router_kernel,
        grid=(t // tm,),
        in_specs=[pl.BlockSpec((tm, d), lambda i: (i, 0)),
                  pl.BlockSpec((1, d), lambda i: (0, 0)),
                  pl.BlockSpec((1, 1, d), lambda i: (modrow(i), 0, 4)),
                  pl.BlockSpec((1, 1, d), lambda i: (modrow(i), 0, 3)),
                  pl.BlockSpec((d, LANES), lambda i: (0, 0))],
        out_specs=[pl.BlockSpec((tm, d), lambda i: (i, 0)), pl.BlockSpec((tm, LANES), lambda i: (i, 0))],
        out_shape=[jax.ShapeDtypeStruct((t, d), F32), jax.ShapeDtypeStruct((t, LANES), F32)],
        compiler_params=_cparams(("parallel",)),
        name="moe_router",
    )(xs, g.reshape(1, d), mod, mod, router_p)


def _route_plan(rinfo, tg):
    t = rinfo.shape[0]
    n_tiles = -(-2 * t // tg) + N_EXPERTS
    e_flat = jnp.concatenate([rinfo[:, 0], rinfo[:, 1]]).astype(jnp.int32)
    onehot = (e_flat[:, None] == jnp.arange(N_EXPERTS, dtype=jnp.int32)[None, :]).astype(jnp.int32)
    csum = jnp.cumsum(onehot, axis=0)
    rank = jnp.sum(onehot * (csum - 1), axis=1)
    counts = csum[-1]
    padded = (counts + tg - 1) // tg * tg
    ends = jnp.cumsum(padded)
    pos = jnp.sum(onehot * (ends - padded)[None, :], axis=1) + rank
    tok = jnp.tile(jnp.arange(t, dtype=jnp.int32), 2)
    src_tok = jnp.zeros((n_tiles * tg,), jnp.int32).at[pos].set(tok)
    tile_start = jnp.arange(n_tiles, dtype=jnp.int32) * tg
    tile_expert = jnp.minimum(jnp.sum((tile_start[:, None] >= ends[None, :]).astype(jnp.int32), axis=1),
                              N_EXPERTS - 1)
    n_used = (ends[-1] // tg).reshape(1)
    return src_tok, tile_expert, n_used, pos


def _row_copy(src_ref, row, dst_ref, r, sem):
    return pltpu.make_async_copy(src_ref.at[pl.ds(row, 1), :], dst_ref.at[pl.ds(r, 1), :], sem)


def _moe_experts_kernel(te_ref, nu_ref, idx_ref, idxn_ref, x_ref, wg_ref, wu_ref, wd_ref, o_ref,
                        xbuf, h_ref, acc_ref, sems):
    i = pl.program_id(0)
    f = pl.program_id(1)
    last = pl.num_programs(1) - 1
    n_used = nu_ref[0]
    used = i < n_used
    slot = i % 2
    rows = h_ref.shape[0]

    def fetch(ids_ref, s):
        def body(r, carry):
            _row_copy(x_ref, ids_ref[0, 0, r], xbuf.at[s], r, sems.at[s]).start()
            return carry
        lax.fori_loop(0, rows, body, 0, unroll=8)

    @pl.when((i == 0) & (f == 0))
    def _():
        fetch(idx_ref, 0)

    @pl.when(used & (f == 0))
    def _():
        def drain(r, carry):
            _row_copy(x_ref, 0, xbuf.at[slot], r, sems.at[slot]).wait()
            return carry
        lax.fori_loop(0, rows, drain, 0, unroll=8)
        h_ref[...] = xbuf[slot].astype(BF16)
        acc_ref[...] = jnp.zeros(acc_ref.shape, F32)

        @pl.when(i + 1 < n_used)
        def _():
            fetch(idxn_ref, 1 - slot)

    @pl.when(used)
    def _():
        h = h_ref[...]
        a = jnp.dot(h, wg_ref[0].astype(BF16), preferred_element_type=F32)
        u = jnp.dot(h, wu_ref[0].astype(BF16), preferred_element_type=F32)
        act = a * (1.0 / (1.0 + jnp.exp(-a))) * u
        acc_ref[...] += jnp.dot(act.astype(BF16), wd_ref[0].astype(BF16), preferred_element_type=F32)

    @pl.when(used & (f == last))
    def _():
        o_ref[...] = acc_ref[...]

    @pl.when(jnp.logical_not(used) & (f == last))
    def _():
        o_ref[...] = jnp.zeros(o_ref.shape, F32)


def _moe_experts(h2, src_tok, tile_expert, n_used, wg, wu, wd, tg, tf):
    d = h2.shape[1]
    n_tiles = src_tok.shape[0] // tg
    ff = wg.shape[2]
    ids = src_tok.reshape(n_tiles, 1, tg)
    grid_spec = pltpu.PrefetchScalarGridSpec(
        num_scalar_prefetch=2,
        grid=(n_tiles, ff // tf),
        in_specs=[pl.BlockSpec((1, 1, tg), lambda i, f, te, nu: (i, 0, 0), memory_space=pltpu.SMEM),
                  pl.BlockSpec((1, 1, tg), lambda i, f, te, nu: (jnp.minimum(i + 1, n_tiles - 1), 0, 0),
                               memory_space=pltpu.SMEM),
                  pl.BlockSpec(memory_space=pl.ANY),
                  pl.BlockSpec((1, d, tf), lambda i, f, te, nu: (te[i], 0, f)),
                  pl.BlockSpec((1, d, tf), lambda i, f, te, nu: (te[i], 0, f)),
                  pl.BlockSpec((1, tf, d), lambda i, f, te, nu: (te[i], f, 0))],
        out_specs=pl.BlockSpec((tg, d), lambda i, f, te, nu: (i, 0)),
        scratch_shapes=[pltpu.VMEM((2, tg, d), F32), pltpu.VMEM((tg, d), BF16), pltpu.VMEM((tg, d), F32),
                        pltpu.SemaphoreType.DMA((2,))])
    return pl.pallas_call(
        _moe_experts_kernel,
        grid_spec=grid_spec,
        out_shape=jax.ShapeDtypeStruct((n_tiles * tg, d), F32),
        compiler_params=_cparams(("arbitrary", "arbitrary")),
        name="moe_experts",
    )(tile_expert, n_used, ids, ids, h2, wg, wu, wd)


def _moe_combine_kernel(i0_ref, i1_ref, y_ref, x_ref, r_ref, gate_ref, o_ref, b0_ref, b1_ref, sem):
    rows = o_ref.shape[0]

    def issue(r, carry):
        _row_copy(y_ref, i0_ref[0, 0, r], b0_ref, r, sem).start()
        _row_copy(y_ref, i1_ref[0, 0, r], b1_ref, r, sem).start()
        return carry
    lax.fori_loop(0, rows, issue, 0, unroll=8)

    def drain(r, carry):
        _row_copy(y_ref, 0, b0_ref, r, sem).wait()
        _row_copy(y_ref, 0, b1_ref, r, sem).wait()
        return carry
    lax.fori_loop(0, rows, drain, 0, unroll=8)
    y = r_ref[:, 2:3] * b0_ref[...] + r_ref[:, 3:4] * b1_ref[...]
    o_ref[...] = x_ref[...] + gate_ref[0] * y


def _moe_combine(xs, yg, pos, rinfo, mod, tm, modrow):
    t, d = xs.shape
    p3 = pos.reshape(2, t // tm, 1, tm)
    return pl.pallas_call(
        _moe_combine_kernel,
        grid=(t // tm,),
        in_specs=[pl.BlockSpec((1, 1, tm), lambda i: (i, 0, 0), memory_space=pltpu.SMEM),
                  pl.BlockSpec((1, 1, tm), lambda i: (i, 0, 0), memory_space=pltpu.SMEM),
                  pl.BlockSpec(memory_space=pl.ANY),
                  pl.BlockSpec((tm, d), lambda i: (i, 0)),
                  pl.BlockSpec((tm, LANES), lambda i: (i, 0)),
                  pl.BlockSpec((1, 1, d), lambda i: (modrow(i), 0, 5))],
        out_specs=pl.BlockSpec((tm, d), lambda i: (i, 0)),
        out_shape=jax.ShapeDtypeStruct((t, d), F32),
        scratch_shapes=[pltpu.VMEM((tm, d), F32), pltpu.VMEM((tm, d), F32), pltpu.SemaphoreType.DMA(())],
        compiler_params=_cparams(("arbitrary",)),
        name="moe_combine",
    )(p3[0], p3[1], yg, xs, rinfo, mod)


def _moe(xs, g, mod, router_p, wg, wu, wd, tm, tf, modrow):
    tg = tm
    h2, rinfo = _router(xs, g, mod, router_p, tm, modrow)
    src_tok, tile_expert, n_used, pos = _route_plan(rinfo, tg)
    yg = _moe_experts(h2, src_tok, tile_expert, n_used, wg, wu, wd, tg, tf)
    return _moe_combine(xs, yg, pos, rinfo, mod, tm, modrow)


def _final_norm_kernel(x_ref, g_ref, o_ref):
    x = x_ref[...]
    ms = jnp.mean(x * x, axis=-1, keepdims=True)
    o_ref[...] = x * lax.rsqrt(ms + EPS) * g_ref[...]


def _final_norm(xs, g, rows, tm):
    d = xs.shape[1]
    return pl.pallas_call(
        _final_norm_kernel,
        grid=(rows // tm,),
        in_specs=[pl.BlockSpec((tm, d), lambda i: (i, 0)), pl.BlockSpec((1, d), lambda i: (0, 0))],
        out_specs=pl.BlockSpec((tm, d), lambda i: (i, 0)),
        out_shape=jax.ShapeDtypeStruct((rows, d), F32),
        compiler_params=_cparams(("parallel",)),
        name="final_norm",
    )(xs, g.reshape(1, d))


def _pad_heads(v, width, used):
    lead = v.shape[:-1]
    v = v.reshape(lead + (C_HEADS, used))
    v = jnp.pad(v, [(0, 0)] * len(lead) + [(0, 0), (0, width - used)])
    return v.reshape(lead + (C_HEADS * width,))


def kernel(x, c, ctx, c_ctx, norm1_g, norm2_g, ada_w, ada_b, w_in, w_out, a_lambda, a_norm_g, b_sink,
           c_gate_w2, c_gate_b, c_norm_g, ffn_w_gate, ffn_w_up, ffn_w_down, moe_router, moe_w_gate,
           moe_w_up, moe_w_down, final_g):
    nb, seq, d = x.shape
    nctx = ctx.shape[1]
    depth = w_in.shape[0]
    n_lat = nb * seq
    tm = nb * nctx
    assert seq % tm == 0 and nb < 8
    tm_r = tm // 2
    n_lat_tiles = n_lat // tm

    def modrow(i):
        return jnp.where(i < n_lat_tiles, i // (seq // tm), nb)

    xs = jnp.concatenate([x.reshape(n_lat, d), ctx.reshape(nb * nctx, d)], axis=0)
    cc = jnp.zeros((8, d), F32).at[:nb].set(c).at[nb].set(c_ctx)
    mod_all = _modulation(cc, ada_w, ada_b).reshape(depth, 8, 1, 6 * d)
    tables = _rope_tables(seq, tm_r)

    for layer in range(depth):
        lam_init = 0.8 - 0.6 * math.exp(-0.3 * layer)
        mod = mod_all[layer]
        w_pad = jnp.concatenate([w_in[layer], jnp.zeros((d, 1), F32)], axis=1)
        w_rope = jnp.take(w_pad, _ROPE_COLS, axis=1).astype(BF16)
        w_plain = jnp.take(w_pad, _PLAIN_COLS, axis=1).astype(BF16)
        w_vt = jnp.take(w_pad, _VT_COLS, axis=1).T.astype(BF16)

        aq, ak, bq, bk = _normproj(xs, norm1_g[layer], mod, 1, 0, w_rope, ROPE_WIDTHS, tm_r,
                                   n_lat // tm_r, seq // tm_r, nb, tables)
        cq, ck, cv, cr, cg, avt, bvt, cvt = _normproj(xs, norm1_g[layer], mod, 1, 0, w_plain, PLAIN_WIDTHS, tm,
                                                      n_lat_tiles, seq // tm, nb, wt=w_vt, vt_widths=VT_WIDTHS)

        a_lat = _diff_attn(aq, ak, avt, a_lambda[layer], a_norm_g[layer], lam_init, nb, seq, nctx, True)
        a_ctx = _diff_attn(aq, ak, avt, a_lambda[layer], a_norm_g[layer], lam_init, nb, seq, nctx, False)
        sink_row = jnp.zeros((1, LANES), F32).at[0, :B_HEADS].set(b_sink[layer] * LOG2E)
        b_lat = _win_attn(bq, bk, bvt, sink_row, nb, seq, nctx, True)
        b_ctx = _win_attn(bq, bk, bvt, sink_row, nb, seq, nctx, False)

        w2 = c_gate_w2[layer]
        w2p = [jnp.zeros((LANES, C_HEADS * LANES), F32).at[dd * C_RANK:(dd + 1) * C_RANK].set(
            _pad_heads(w2[dd], LANES, C_DK)).astype(BF16) for dd in range(2)]
        bgp = [_pad_heads(c_gate_b[layer, dd], LANES, C_DK).reshape(1, -1) for dd in range(2)]
        ng = _pad_heads(jnp.tile(c_norm_g[layer], C_HEADS), LANES, C_DV).reshape(1, -1)
        o_f = _gla_dir(cq, ck, cv, cvt, cg, w2p[0], bgp[0], nb, seq, nctx, False)
        g_out = _gla_dir(cq, ck, cv, cvt, cg, w2p[1], bgp[1], nb, seq, nctx, True, o_f, cr, ng)

        a_all = jnp.concatenate([a_lat, a_ctx], axis=0)
        b_all = jnp.concatenate([b_lat, b_ctx], axis=0)
        wo = w_out[layer]
        wa = wo[:256].astype(BF16)
        wb = wo[256:640].astype(BF16)
        wc = jnp.pad(wo[640:].reshape(C_HEADS, C_DV, d), ((0, 0), (0, LANES - C_DV), (0, 0))).reshape(
            C_HEADS * LANES, d).astype(BF16)
        xs = _outproj(xs, a_all, b_all, g_out, wa, wb, wc, mod, tm, modrow)

        j = layer // 2
        if layer % 2 == 0:
            xs = _ffn(xs, norm2_g[layer], mod, ffn_w_gate[j], ffn_w_up[j], ffn_w_down[j], tm, 256, modrow)
        else:
            router_p = jnp.pad(moe_router[j], ((0, 0), (0, LANES - N_EXPERTS)))
            xs = _moe(xs, norm2_g[layer], mod, router_p, moe_w_gate[j], moe_w_up[j], moe_w_down[j],
                      tm, 512, modrow)

    return _final_norm(xs, final_g, n_lat, tm).reshape(nb, seq, d)
```

```python
import functools
import math

import numpy as np
import jax
import jax.numpy as jnp
from jax import lax
from jax.experimental import pallas as pl
from jax.experimental.pallas import tpu as pltpu

F32 = jnp.float32
BF16 = jnp.bfloat16

EPS = 1e-6
ROPE_BASE = 10000.0
GRID_W = 64
LANES = 128
LOG2E = math.log2(math.e)
NEG = -1e30

A_HEADS, A_QK, A_V = 4, 32, 64
B_HEADS, B_KV, B_HD, WINDOW, BLOCK = 6, 2, 64, 128, 128
C_HEADS, C_DK, C_DV, C_RANK, C_GATE_NORM, C_CHUNK = 4, 48, 96, 16, 16.0, 64
N_EXPERTS = 8
IN_SIZES = (256, 256, 256, 384, 128, 128, 192, 192, 384, 384, 32)
IN_W = sum(IN_SIZES)

ROPE_WIDTHS = (256, 256, 768, 128)
PLAIN_WIDTHS = (512, 512, 512, 512, 128)
VT_WIDTHS = (256, 128, 512)
ROPE_W = sum(ROPE_WIDTHS)
GLA_GROUP = 256
VMEM_LIMIT = 48 * 1024 * 1024


def _cparams(sem):
    return pltpu.CompilerParams(dimension_semantics=sem, vmem_limit_bytes=VMEM_LIMIT)


def _column_maps():
    off = np.concatenate([[0], np.cumsum(IN_SIZES)])
    aq0, ak0, av0, bq0, bk0, bv0, cq0, ck0, cv0, cr0, cg0 = [int(v) for v in off[:11]]
    zero = IN_W

    def a_partner(d):
        return d + 8 if d % 16 < 8 else d - 8

    def b_partner(d):
        return d + 16 if d % 32 < 16 else d - 16

    main, part, dim, dd, scale = [], [], [], [], []
    a_scale = A_QK ** -0.5 * LOG2E
    b_scale = B_HD ** -0.5 * LOG2E
    for base, sc in ((aq0, a_scale), (ak0, 1.0)):
        for j in range(256):
            d = j % 32
            main.append(base + j); part.append(base + j - d + a_partner(d))
            dim.append(32); dd.append(d); scale.append(sc)
    for t in range(B_HEADS):
        g = t // (B_HEADS // B_KV)
        for lane in range(LANES):
            d = lane % 64
            if lane // 64 == g:
                main.append(bq0 + t * 64 + d); part.append(bq0 + t * 64 + b_partner(d))
            else:
                main.append(zero); part.append(zero)
            dim.append(64); dd.append(d); scale.append(b_scale)
    for j in range(128):
        d = j % 64
        main.append(bk0 + j); part.append(bk0 + j - d + b_partner(d))
        dim.append(64); dd.append(d); scale.append(1.0)

    plain = []
    for base in (cq0, ck0):
        for h in range(C_HEADS):
            plain += [base + h * C_DK + d if d < C_DK else zero for d in range(LANES)]
    for base in (cv0, cr0):
        for h in range(C_HEADS):
            plain += [base + h * C_DV + d if d < C_DV else zero for d in range(LANES)]
    plain += [cg0 + d if d < 2 * C_RANK else zero for d in range(LANES)]
    vt = list(range(av0, av0 + 256)) + list(range(bv0, bv0 + 128))
    for h in range(C_HEADS):
        vt += [cv0 + h * C_DV + d if d < C_DV else zero for d in range(LANES)]
    return (np.array(main + part, np.int32), np.array(plain, np.int32), np.array(vt, np.int32),
            np.array(dim), np.array(dd), np.array(scale, np.float32))


_ROPE_COLS, _PLAIN_COLS, _VT_COLS, _R_DIM, _R_D, _R_SCALE = _column_maps()


def _rope_tables(seq, pad_rows):
    sec_start = np.cumsum((0,) + ROPE_WIDTHS[:-1])
    cols = np.concatenate([np.arange(s, s + LANES) for s in sec_start for _ in range(2)])
    is_sin = jnp.asarray(np.tile(np.repeat([False, True], LANES), len(ROPE_WIDTHS)))[None, :]
    r_dim, r_d, r_scale = _R_DIM[cols], _R_D[cols], _R_SCALE[cols]
    quarter = r_dim // 4
    half = r_dim // 2
    is_col = jnp.asarray((r_d % r_dim) >= half)
    ddh = r_d % half
    first = jnp.asarray(ddh < quarter)[None, :]
    f = (ddh % quarter).astype(np.float32)
    inv = jnp.asarray(ROPE_BASE, F32) ** (-jnp.asarray(f) / jnp.asarray(quarter.astype(np.float32)))
    scale = jnp.asarray(r_scale)[None, :]

    def trig(n):
        ang = jnp.arange(n, dtype=F32)[:, None] * inv[None, :]
        return jnp.where(is_sin, jnp.where(first, -jnp.sin(ang), jnp.sin(ang)), jnp.cos(ang)) * scale

    t_row = trig(seq // GRID_W)
    t_col = trig(GRID_W)
    tab = jnp.where(is_col[None, None, :], t_col[None, :, :], t_row[:, None, :]).reshape(seq, -1)
    ident = jnp.broadcast_to(jnp.where(is_sin, 0.0, scale), (pad_rows, tab.shape[1]))
    return jnp.concatenate([tab, ident], axis=0)


def _mod_kernel(c_ref, w_ref, b_ref, o_ref):
    c = c_ref[...]
    s = c * (1.0 / (1.0 + jnp.exp(-c)))
    o_ref[0] = jnp.dot(s, w_ref[0], precision=lax.Precision.HIGHEST,
                       preferred_element_type=F32) + b_ref[0]


def _modulation(cc, ada_w, ada_b):
    depth, d, n = ada_w.shape
    tn = n // 4
    return pl.pallas_call(
        _mod_kernel,
        grid=(depth, n // tn),
        in_specs=[pl.BlockSpec((8, d), lambda l, j: (0, 0)),
                  pl.BlockSpec((1, d, tn), lambda l, j: (l, 0, j)),
                  pl.BlockSpec((1, 1, tn), lambda l, j: (l, 0, j))],
        out_specs=pl.BlockSpec((1, 8, tn), lambda l, j: (l, 0, j)),
        out_shape=jax.ShapeDtypeStruct((depth, 8, n), F32),
        compiler_params=_cparams(("arbitrary", "arbitrary")),
        name="adaln_mod",
    )(cc, ada_w, ada_b.reshape(depth, 1, n))


def _norm_mod(x, g, sc, sh):
    ms = jnp.mean(x * x, axis=-1, keepdims=True)
    return (x * lax.rsqrt(ms + EPS) * g) * (1.0 + sc) + sh


def _normproj_kernel(x_ref, g_ref, sc_ref, sh_ref, w_ref, *rest, rope, widths, vt_widths):
    if rope:
        t_ref = rest[0]
        rest = rest[1:]
    if vt_widths:
        wt_ref = rest[0]
        rest = rest[1:]
    outs = rest[:len(widths)]
    vt_outs = rest[len(widths):]
    h = _norm_mod(x_ref[...], g_ref[...], sc_ref[0], sh_ref[0]).astype(BF16)
    acc = jnp.dot(h, w_ref[...], preferred_element_type=F32)
    rw = sum(widths)
    off = 0
    for sec, (o_ref, w) in enumerate(zip(outs, widths)):
        if rope:
            cos = t_ref[:, (2 * sec) * LANES:(2 * sec + 1) * LANES]
            sin = t_ref[:, (2 * sec + 1) * LANES:(2 * sec + 2) * LANES]
            for j in range(w // LANES):
                c0 = off + j * LANES
                o_ref[:, j * LANES:(j + 1) * LANES] = (
                    acc[:, c0:c0 + LANES] * cos + acc[:, rw + c0:rw + c0 + LANES] * sin).astype(o_ref.dtype)
        else:
            o_ref[...] = acc[:, off:off + w].astype(o_ref.dtype)
        off += w
    off = 0
    for o_ref, w in zip(vt_outs, vt_widths):
        o_ref[...] = lax.dot_general(wt_ref[off:off + w, :], h, (((1,), (1,)), ((), ())),
                                     preferred_element_type=F32).astype(o_ref.dtype)
        off += w


def _normproj(xs, g, mod, sc_chunk, sh_chunk, w, widths, tm, n_lat_tiles, tiles_per_batch, nb, tables=None,
              wt=None, vt_widths=()):
    t, d = xs.shape
    rope = tables is not None

    def modrow(i):
        return jnp.where(i < n_lat_tiles, i // tiles_per_batch, nb)

    in_specs = [pl.BlockSpec((tm, d), lambda i: (i, 0)),
                pl.BlockSpec((1, d), lambda i: (0, 0)),
                pl.BlockSpec((1, 1, d), lambda i: (modrow(i), 0, sc_chunk)),
                pl.BlockSpec((1, 1, d), lambda i: (modrow(i), 0, sh_chunk)),
                pl.BlockSpec(w.shape, lambda i: (0, 0))]
    args = [xs, g.reshape(1, d), mod, mod, w]
    if rope:
        def tabrow(i):
            return jnp.where(i < n_lat_tiles, i % tiles_per_batch, tiles_per_batch)
        in_specs += [pl.BlockSpec((tm, tables.shape[1]), lambda i: (tabrow(i), 0))]
        args += [tables]
    if vt_widths:
        in_specs += [pl.BlockSpec(wt.shape, lambda i: (0, 0))]
        args += [wt]
    return pl.pallas_call(
        functools.partial(_normproj_kernel, rope=rope, widths=widths, vt_widths=vt_widths),
        grid=(t // tm,),
        in_specs=in_specs,
        out_specs=([pl.BlockSpec((tm, wd), lambda i: (i, 0)) for wd in widths]
                   + [pl.BlockSpec((wd, tm), lambda i: (0, i)) for wd in vt_widths]),
        out_shape=([jax.ShapeDtypeStruct((t, wd), BF16) for wd in widths]
                   + [jax.ShapeDtypeStruct((wd, t), BF16) for wd in vt_widths]),
        compiler_params=_cparams(("parallel",)),
        name="normproj_rope" if rope else "normproj_plain",
    )(*args)


def _diff_attn_kernel(*refs, lam_init, has_lat, tk):
    if has_lat:
        q_ref, kc_ref, vtc_ref, kl_ref, vtl_ref, lam_ref, g_ref, o_ref, m_scr, acc_scr, mc_scr, s_scr = refs
    else:
        q_ref, kc_ref, vtc_ref, lam_ref, g_ref, o_ref, m_scr, acc_scr, mc_scr, s_scr = refs
    q = q_ref[...]
    lane = lax.broadcasted_iota(jnp.int32, (1, LANES), 1)
    nt = (((1,), (1,)), ((), ()))
    qms = [jnp.where((lane >= i * A_QK) & (lane < (i + 1) * A_QK), q, jnp.zeros_like(q)) for i in range(4)]

    sub = 512

    def scores(k, slot, nk):
        for i in range(4):
            s = lax.dot_general(k, qms[i], nt, preferred_element_type=F32)
            s_scr[slot, i, 0:nk, :] = s
            mc_scr[slot, i] = jnp.max(s, axis=0, keepdims=True)

    def consume(vt_of, slot, nk, first):
        for i in range(4):
            m_cur = mc_scr[slot, i]
            if first:
                m_new = m_cur
            else:
                m_run = m_scr[i]
                m_new = jnp.maximum(m_run, m_cur)
            pv = None
            for t in range(nk // min(sub, nk)):
                w = min(sub, nk)
                p = jnp.exp2(s_scr[slot, i, t * w:(t + 1) * w, :] - m_new).astype(BF16)
                d = jnp.dot(vt_of(i // 2, t, w), p, preferred_element_type=F32)
                pv = d if pv is None else pv + d
            if first:
                acc_scr[i] = pv
            else:
                acc_scr[i] = jnp.exp2(m_run - m_new) * acc_scr[i] + pv
            m_scr[i] = m_new

    nctx = kc_ref.shape[0]
    scores(kc_ref[...], 1, nctx)

    def with_ones(vt):
        return jnp.concatenate([vt, jnp.ones((8, vt.shape[1]), vt.dtype)], axis=0)

    def vt_ctx(hh, t, w):
        return with_ones(vtc_ref[hh * A_V:(hh + 1) * A_V, t * w:(t + 1) * w])

    if not has_lat:
        consume(vt_ctx, 1, nctx, True)
    else:
        n_chunks = kl_ref.shape[0] // tk

        def k_lat(c):
            return kl_ref[pl.ds(pl.multiple_of(c * tk, tk), tk), :]

        def vt_lat(c):
            def get(hh, t, w):
                return with_ones(vtl_ref[hh * A_V:(hh + 1) * A_V, pl.ds(pl.multiple_of(c * tk + t * w, w), w)])
            return get

        scores(k_lat(0), 0, tk)
        consume(vt_ctx, 1, nctx, True)

        def body(j, carry):
            c = 2 * j
            scores(k_lat(c + 1), 1, tk)
            consume(vt_lat(c), 0, tk, False)
            scores(k_lat(c + 2), 0, tk)
            consume(vt_lat(c + 1), 1, tk, False)
            return carry
        pairs = (n_chunks - 1) // 2
        lax.fori_loop(0, pairs, body, 0)
        c_last = 2 * pairs
        if c_last + 1 < n_chunks:
            scores(k_lat(c_last + 1), 1, tk)
            consume(vt_lat(c_last), 0, tk, False)
            consume(vt_lat(c_last + 1), 1, tk, False)
        else:
            consume(vt_lat(c_last), 0, tk, False)

    lp = lam_ref[...]
    lam = (jnp.exp(jnp.sum(lp[0:1] * lp[1:2], axis=1, keepdims=True))
           - jnp.exp(jnp.sum(lp[2:3] * lp[3:4], axis=1, keepdims=True)) + lam_init)
    heads = []
    for hh in range(2):
        maps = []
        for m in range(2):
            acc = acc_scr[hh * 2 + m]
            maps.append(acc[0:A_V] / acc[A_V:A_V + 1])
        oh = maps[0] - lam * maps[1]
        ms = jnp.mean(oh * oh, axis=0, keepdims=True)
        heads.append(oh * lax.rsqrt(ms + EPS) * g_ref[...] * (1.0 - lam_init))
    o_ref[...] = jnp.concatenate(heads, axis=0).T.astype(o_ref.dtype)


def _diff_attn(aq, ak, avt, lam_p, norm_g, lam_init, nb, seq, ctx, latent):
    tq = 256
    tk = min(1024, seq)
    kern = functools.partial(_diff_attn_kernel, lam_init=lam_init, has_lat=latent, tk=tk)
    ctx_blk0 = nb * seq // ctx
    vtc_spec = pl.BlockSpec((2 * A_V, ctx), lambda b, p, i: (p, ctx_blk0 + b))
    kc_spec = pl.BlockSpec((ctx, LANES), lambda b, p, i: (ctx_blk0 + b, p))
    par_specs = [pl.BlockSpec((4, A_QK), lambda b, p, i: (0, 0)),
                 pl.BlockSpec((A_V, 1), lambda b, p, i: (0, 0))]
    if latent:
        nq = seq // tq
        in_specs = [pl.BlockSpec((tq, LANES), lambda b, p, i: (b * nq + i, p)), kc_spec, vtc_spec,
                    pl.BlockSpec((seq, LANES), lambda b, p, i: (b, p)),
                    pl.BlockSpec((2 * A_V, seq), lambda b, p, i: (p, b))] + par_specs
        args = (aq, ak, avt, ak, avt, lam_p, norm_g.reshape(A_V, 1))
        rows = nb * seq
    else:
        nq = ctx // tq
        q_blk0 = nb * seq // tq
        in_specs = [pl.BlockSpec((tq, LANES), lambda b, p, i: (q_blk0 + b * nq + i, p)),
                    kc_spec, vtc_spec] + par_specs
        args = (aq, ak, avt, lam_p, norm_g.reshape(A_V, 1))
        rows = nb * ctx
    return pl.pallas_call(
        kern,
        grid=(nb, 2, nq),
        in_specs=in_specs,
        out_specs=pl.BlockSpec((tq, LANES), lambda b, p, i: (b * nq + i, p)),
        out_shape=jax.ShapeDtypeStruct((rows, 2 * LANES), BF16),
        scratch_shapes=[pltpu.VMEM((4, 1, tq), F32), pltpu.VMEM((4, A_V + 8, tq), F32),
                        pltpu.VMEM((2, 4, 1, tq), F32),
                        pltpu.VMEM((2, 4, tk if latent else ctx, tq), F32)],
        compiler_params=_cparams(("parallel", "parallel", "arbitrary")),
        name="diff_attn_lat" if latent else "diff_attn_ctx",
    )(*args)


def _win_attn_kernel(*refs, with_window, seq, ctx):
    if with_window:
        q_ref, kc_ref, vtc_ref, kp_ref, kn0_ref, kn_ref, vtp_ref, vt0_ref, vtn_ref, sink_ref, o_ref = refs
        n = pl.program_id(1)
        k_all = jnp.concatenate([kc_ref[...], kp_ref[...], kn0_ref[...], kn_ref[...]], axis=0)
        vt_all = jnp.concatenate([vtc_ref[...], vtp_ref[...], vt0_ref[...], vtn_ref[...]], axis=1)
        nk = ctx + 3 * BLOCK
        r = lax.broadcasted_iota(jnp.int32, (nk, BLOCK), 0)
        c = lax.broadcasted_iota(jnp.int32, (nk, BLOCK), 1)
        krel = r - (ctx + BLOCK)
        in_win = ((jnp.abs(c - krel) <= WINDOW) & (krel >= -n * BLOCK) & (krel < seq - n * BLOCK))
        is_ctx = r < ctx
    else:
        q_ref, kc_ref, vtc_ref, sink_ref, o_ref = refs
        k_all = kc_ref[...]
        vt_all = vtc_ref[...]
    lane = lax.broadcasted_iota(jnp.int32, (1, LANES), 1)
    sink_row = sink_ref[...]
    nt = (((1,), (1,)), ((), ()))
    rep = B_HEADS // B_KV
    outs = []
    for j in range(B_HEADS):
        g = j // rep
        s = lax.dot_general(k_all, q_ref[:, j * LANES:(j + 1) * LANES], nt, preferred_element_type=F32)
        if with_window:
            s = jnp.where(is_ctx, s, jnp.where(in_win, s, NEG))
        sk = jnp.max(jnp.where(lane == j, sink_row, NEG), axis=1, keepdims=True)
        m = jnp.maximum(jnp.max(s, axis=0, keepdims=True), sk)
        p = jnp.exp2(s - m)
        l = jnp.sum(p, axis=0, keepdims=True) + jnp.exp2(sk - m)
        ot = jnp.dot(vt_all, p.astype(BF16), preferred_element_type=F32)
        outs.append(ot[g * B_HD:(g + 1) * B_HD] / l)
    for t in range(B_HEADS // 2):
        pair = jnp.concatenate([outs[2 * t], outs[2 * t + 1]], axis=0)
        o_ref[:, t * LANES:(t + 1) * LANES] = pair.T.astype(o_ref.dtype)


def _win_attn(bq, bk, bvt, sink_row, nb, seq, ctx, latent):
    nblk = seq // BLOCK
    ctx_blk0 = nb * seq // ctx
    kern = functools.partial(_win_attn_kernel, with_window=latent, seq=seq, ctx=ctx)
    kc_spec = pl.BlockSpec((ctx, LANES), lambda b, n: (ctx_blk0 + b, 0))
    vtc_spec = pl.BlockSpec((LANES, ctx), lambda b, n: (0, ctx_blk0 + b))
    sink_spec = pl.BlockSpec((1, LANES), lambda b, n: (0, 0))
    qw = B_HEADS * LANES
    if latent:
        def prev(n):
            return jnp.maximum(n - 1, 0)

        def nxt(n):
            return jnp.minimum(n + 1, nblk - 1)

        in_specs = [pl.BlockSpec((BLOCK, qw), lambda b, n: (b * nblk + n, 0)), kc_spec, vtc_spec,
                    pl.BlockSpec((BLOCK, LANES), lambda b, n: (b * nblk + prev(n), 0)),
                    pl.BlockSpec((BLOCK, LANES), lambda b, n: (b * nblk + n, 0)),
                    pl.BlockSpec((BLOCK, LANES), lambda b, n: (b * nblk + nxt(n), 0)),
                    pl.BlockSpec((LANES, BLOCK), lambda b, n: (0, b * nblk + prev(n))),
                    pl.BlockSpec((LANES, BLOCK), lambda b, n: (0, b * nblk + n)),
                    pl.BlockSpec((LANES, BLOCK), lambda b, n: (0, b * nblk + nxt(n))),
                    sink_spec]
        args = (bq, bk, bvt, bk, bk, bk, bvt, bvt, bvt, sink_row)
        grid = (nb, nblk)
        rows = nb * seq
        nq = nblk
    else:
        nq = ctx // BLOCK
        q_blk0 = nb * seq // BLOCK
        in_specs = [pl.BlockSpec((BLOCK, qw), lambda b, n: (q_blk0 + b * nq + n, 0)),
                    kc_spec, vtc_spec, sink_spec]
        args = (bq, bk, bvt, sink_row)
        grid = (nb, nq)
        rows = nb * ctx
    return pl.pallas_call(
        kern,
        grid=grid,
        in_specs=in_specs,
        out_specs=pl.BlockSpec((BLOCK, B_HEADS * B_HD), lambda b, n: (b * nq + n, 0)),
        out_shape=jax.ShapeDtypeStruct((rows, B_HEADS * B_HD), BF16),
        compiler_params=_cparams(("parallel", "arbitrary")),
        name="win_attn_lat" if latent else "win_attn_ctx",
    )(*args)


def _split3(x):
    hi = x.astype(BF16)
    r1 = x - hi.astype(F32)
    mid = r1.astype(BF16)
    lo = (r1 - mid.astype(F32)).astype(BF16)
    return hi, mid, lo


def _gla_kernel(*refs, reverse, final):
    if final:
        q_ref, k_ref, v_ref, vt_ref, gl_ref, w2_ref, bg_ref, of_ref, r_ref, ng_ref, o_ref, st_ref = refs
    else:
        q_ref, k_ref, v_ref, vt_ref, gl_ref, w2_ref, bg_ref, o_ref, st_ref = refs
    gsz = GLA_GROUP
    nch = gsz // C_CHUNK

    @pl.when(pl.program_id(1) == 0)
    def _():
        st_ref[...] = jnp.zeros(st_ref.shape, F32)

    q = q_ref[...].astype(F32)
    k = k_ref[...].astype(F32)
    pre = jnp.dot(gl_ref[...], w2_ref[...], preferred_element_type=F32) + bg_ref[...]
    la = (jnp.minimum(pre, 0.0) - jnp.log(1.0 + jnp.exp(-jnp.abs(pre)))) * (1.0 / C_GATE_NORM)

    r = lax.broadcasted_iota(jnp.int32, (gsz, gsz), 0)
    c = lax.broadcasted_iota(jnp.int32, (gsz, gsz), 1)
    same = (r // C_CHUNK) == (c // C_CHUNK)
    tri = same & ((c >= r) if reverse else (c <= r))
    tri_b = jnp.where(tri, 1.0, 0.0).astype(BF16)
    parts = _split3(la)
    bcum = sum(jnp.dot(tri_b, p, preferred_element_type=F32) for p in parts)
    edge = 0 if reverse else C_CHUNK - 1
    btot = jnp.concatenate(
        [jnp.broadcast_to(bcum[ci * C_CHUNK + edge:ci * C_CHUNK + edge + 1], (C_CHUNK, bcum.shape[1]))
         for ci in range(nch)], axis=0)
    q_dec = q * (jnp.exp(bcum) * (C_DK ** -0.5))
    k_inv = k * jnp.exp(-bcum)
    k_end = k * jnp.exp(btot - bcum)
    dec = jnp.exp(btot)

    rowid = lax.broadcasted_iota(jnp.int32, (gsz, LANES), 0) // C_CHUNK
    nt = (((1,), (1,)), ((), ()))
    order = list(range(nch))[::-1] if reverse else list(range(nch))
    outs = []
    for h in range(C_HEADS):
        sl = slice(h * LANES, (h + 1) * LANES)
        qd = q_dec[:, sl].astype(BF16)
        ki = k_inv[:, sl].astype(BF16)
        ke = k_end[:, sl].astype(BF16)
        att = lax.dot_general(qd, ki, nt, preferred_element_type=F32)
        att = jnp.where(tri, att, 0.0).astype(BF16)
        o_h = jnp.dot(att, v_ref[:, sl], preferred_element_type=F32)
        vth = vt_ref[sl, :]
        st = st_ref[h]
        inter = [None] * nch
        for ci in order:
            rows = slice(ci * C_CHUNK, (ci + 1) * C_CHUNK)
            inter[ci] = lax.dot_general(qd[rows], st.astype(BF16), nt, preferred_element_type=F32)
            ke_c = jnp.where(rowid == ci, ke, jnp.zeros_like(ke))
            upd = jnp.dot(vth, ke_c, preferred_element_type=F32)
            st = dec[ci * C_CHUNK:ci * C_CHUNK + 1, sl] * st + upd
        st_ref[h] = st
        outs.append(o_h + jnp.concatenate(inter, axis=0))
    if not final:
        for h in range(C_HEADS):
            o_ref[:, h * LANES:(h + 1) * LANES] = outs[h]
    else:
        for h in range(C_HEADS):
            sl = slice(h * LANES, (h + 1) * LANES)
            o = outs[h] + of_ref[:, sl]
            ms = jnp.sum(o * o, axis=1, keepdims=True) * (1.0 / C_DV)
            y = o * lax.rsqrt(ms + EPS) * ng_ref[:, sl]
            rr = r_ref[:, sl].astype(F32)
            o_ref[:, sl] = (y * (rr * (1.0 / (1.0 + jnp.exp(-rr))))).astype(o_ref.dtype)


def _gla_dir(cq, ck, cv, cvt, cg, w2p, bgp, nb, seq, ctx, reverse, fwd_out=None, cr=None, ng=None):
    t = cq.shape[0]
    gsz = GLA_GROUP
    assert ctx == gsz
    n_lat_g = seq // gsz
    ctx_g0 = nb * n_lat_g
    final = fwd_out is not None

    def grp(b, i):
        lat = (n_lat_g - i) if reverse else (i - 1)
        return jnp.where(i == 0, ctx_g0 + b, b * n_lat_g + lat)

    w = C_HEADS * LANES
    row_spec = pl.BlockSpec((gsz, w), lambda b, i: (grp(b, i), 0))
    in_specs = [row_spec, row_spec, row_spec,
                pl.BlockSpec((w, gsz), lambda b, i: (0, grp(b, i))),
                pl.BlockSpec((gsz, LANES), lambda b, i: (grp(b, i), 0)),
                pl.BlockSpec((LANES, w), lambda b, i: (0, 0)),
                pl.BlockSpec((1, w), lambda b, i: (0, 0))]
    args = [cq, ck, cv, cvt, cg, w2p, bgp]
    if final:
        in_specs += [row_spec, row_spec, pl.BlockSpec((1, w), lambda b, i: (0, 0))]
        args += [fwd_out, cr, ng]
    return pl.pallas_call(
        functools.partial(_gla_kernel, reverse=reverse, final=final),
        grid=(nb, n_lat_g + 1),
        in_specs=in_specs,
        out_specs=row_spec,
        out_shape=jax.ShapeDtypeStruct((t, w), BF16 if final else F32),
        scratch_shapes=[pltpu.VMEM((C_HEADS, LANES, LANES), F32)],
        compiler_params=_cparams(("parallel", "arbitrary")),
        name="gla_bwd" if reverse else "gla_fwd",
    )(*args)


def _outproj_kernel(x_ref, a_ref, b_ref, c_ref, wa_ref, wb_ref, wc_ref, g1_ref, o_ref):
    y = jnp.dot(a_ref[...], wa_ref[...], preferred_element_type=F32)
    y += jnp.dot(b_ref[...], wb_ref[...], preferred_element_type=F32)
    y += jnp.dot(c_ref[...], wc_ref[...], preferred_element_type=F32)
    o_ref[...] = x_ref[...] + g1_ref[0] * y


def _outproj(xs, a, b, c, wa, wb, wc, mod, tm, modrow):
    t, d = xs.shape
    return pl.pallas_call(
        _outproj_kernel,
        grid=(t // tm,),
        in_specs=[pl.BlockSpec((tm, d), lambda i: (i, 0)),
                  pl.BlockSpec((tm, a.shape[1]), lambda i: (i, 0)),
                  pl.BlockSpec((tm, b.shape[1]), lambda i: (i, 0)),
                  pl.BlockSpec((tm, c.shape[1]), lambda i: (i, 0)),
                  pl.BlockSpec(wa.shape, lambda i: (0, 0)),
                  pl.BlockSpec(wb.shape, lambda i: (0, 0)),
                  pl.BlockSpec(wc.shape, lambda i: (0, 0)),
                  pl.BlockSpec((1, 1, d), lambda i: (modrow(i), 0, 2))],
        out_specs=pl.BlockSpec((tm, d), lambda i: (i, 0)),
        out_shape=jax.ShapeDtypeStruct((t, d), F32),
        compiler_params=_cparams(("parallel",)),
        name="outproj",
    )(xs, a, b, c, wa, wb, wc, mod)


def _ffn_kernel(x_ref, g_ref, sc_ref, sh_ref, gate_ref, wg_ref, wu_ref, wd_ref, o_ref, h_ref, acc_ref):
    f = pl.program_id(1)

    @pl.when(f == 0)
    def _():
        h_ref[...] = _norm_mod(x_ref[...], g_ref[...], sc_ref[0], sh_ref[0]).astype(BF16)
        acc_ref[...] = jnp.zeros(acc_ref.shape, F32)

    h = h_ref[...]
    a = jnp.dot(h, wg_ref[...].astype(BF16), preferred_element_type=F32)
    u = jnp.dot(h, wu_ref[...].astype(BF16), preferred_element_type=F32)
    act = a * (1.0 / (1.0 + jnp.exp(-a))) * u
    acc_ref[...] += jnp.dot(act.astype(BF16), wd_ref[...].astype(BF16), preferred_element_type=F32)

    @pl.when(f == pl.num_programs(1) - 1)
    def _():
        o_ref[...] = x_ref[...] + gate_ref[0] * acc_ref[...]


def _ffn(xs, g, mod, wg, wu, wd, tm, tf, modrow):
    t, d = xs.shape
    ff = wg.shape[1]
    return pl.pallas_call(
        _ffn_kernel,
        grid=(t // tm, ff // tf),
        in_specs=[pl.BlockSpec((tm, d), lambda i, f: (i, 0)),
                  pl.BlockSpec((1, d), lambda i, f: (0, 0)),
                  pl.BlockSpec((1, 1, d), lambda i, f: (modrow(i), 0, 4)),
                  pl.BlockSpec((1, 1, d), lambda i, f: (modrow(i), 0, 3)),
                  pl.BlockSpec((1, 1, d), lambda i, f: (modrow(i), 0, 5)),
                  pl.BlockSpec((d, tf), lambda i, f: (0, f)),
                  pl.BlockSpec((d, tf), lambda i, f: (0, f)),
                  pl.BlockSpec((tf, d), lambda i, f: (f, 0))],
        out_specs=pl.BlockSpec((tm, d), lambda i, f: (i, 0)),
        out_shape=jax.ShapeDtypeStruct((t, d), F32),
        scratch_shapes=[pltpu.VMEM((tm, d), BF16), pltpu.VMEM((tm, d), F32)],
        compiler_params=_cparams(("parallel", "arbitrary")),
        name="ffn_swiglu",
    )(xs, g.reshape(1, d), mod, mod, mod, wg, wu, wd)


def _router_kernel(x_ref, g_ref, sc_ref, sh_ref, rt_ref, h_ref, r_ref):
    lane = lax.broadcasted_iota(jnp.int32, (1, LANES), 1)
    hf = _norm_mod(x_ref[...], g_ref[...], sc_ref[0], sh_ref[0])
    h_ref[...] = hf
    logits = jnp.dot(hf, rt_ref[...], precision=lax.Precision.HIGHEST, preferred_element_type=F32)
    lanef = lane.astype(F32)
    lg = jnp.where(lane < N_EXPERTS, logits, NEG)
    m1 = jnp.max(lg, axis=1, keepdims=True)
    i1 = jnp.min(jnp.where(lg == m1, lanef, float(LANES)), axis=1, keepdims=True)
    lg2 = jnp.where(lanef == i1, NEG, lg)
    m2 = jnp.max(lg2, axis=1, keepdims=True)
    i2 = jnp.min(jnp.where(lg2 == m2, lanef, float(LANES)), axis=1, keepdims=True)
    e2 = jnp.exp(m2 - m1)
    w1 = 1.0 / (1.0 + e2)
    r_ref[...] = jnp.where(lane == 0, i1, jnp.where(lane == 1, i2, jnp.where(lane == 2, w1,
                           jnp.where(lane == 3, e2 * w1, 0.0))))


def _router(xs, g, mod, router_p, tm, modrow):
    t, d = xs.shape
    return pl.pallas_call(
        _router_kernel,
        grid=(t // tm,),
        in_specs=[pl.BlockSpec((tm, d), lambda i: (i, 0)),
                  pl.BlockSpec((1, d), lambda i: (0, 0)),
                  pl.BlockSpec((1, 1, d), lambda i: (modrow(i), 0, 4)),
                  pl.BlockSpec((1, 1, d), lambda i: (modrow(i), 0, 3)),
                  pl.BlockSpec((d, LANES), lambda i: (0, 0))],
        out_specs=[pl.BlockSpec((tm, d), lambda i: (i, 0)), pl.BlockSpec((tm, LANES), lambda i: (i, 0))],
        out_shape=[jax.ShapeDtypeStruct((t, d), F32), jax.ShapeDtypeStruct((t, LANES), F32)],
        compiler_params=_cparams(("parallel",)),
        name="moe_router",
    )(xs, g.reshape(1, d), mod, mod, router_p)


def _route_plan(rinfo, tg):
    t = rinfo.shape[0]
    n_tiles = -(-2 * t // tg) + N_EXPERTS
    e_flat = jnp.concatenate([rinfo[:, 0], rinfo[:, 1]]).astype(jnp.int32)
    onehot = (e_flat[:, None] == jnp.arange(N_EXPERTS, dtype=jnp.int32)[None, :]).astype(jnp.int32)
    csum = jnp.cumsum(onehot, axis=0)
    rank = jnp.sum(onehot * (csum - 1), axis=1)
    counts = csum[-1]
    padded = (counts + tg - 1) // tg * tg
    ends = jnp.cumsum(padded)
    pos = jnp.sum(onehot * (ends - padded)[None, :], axis=1) + rank
    tok = jnp.tile(jnp.arange(t, dtype=jnp.int32), 2)
    src_tok = jnp.zeros((n_tiles * tg,), jnp.int32).at[pos].set(tok)
    tile_start = jnp.arange(n_tiles, dtype=jnp.int32) * tg
    tile_expert = jnp.minimum(jnp.sum((tile_start[:, None] >= ends[None, :]).astype(jnp.int32), axis=1),
                              N_EXPERTS - 1)
    n_used = (ends[-1] // tg).reshape(1)
    return src_tok, tile_expert, n_used, pos


def _row_copy(src_ref, row, dst_ref, r, sem):
    return pltpu.make_async_copy(src_ref.at[pl.ds(row, 1), :], dst_ref.at[pl.ds(r, 1), :], sem)


def _moe_experts_kernel(te_ref, nu_ref, idx_ref, idxn_ref, x_ref, wg_ref, wu_ref, wd_ref, o_ref,
                        xbuf, h_ref, acc_ref, sems):
    i = pl.program_id(0)
    f = pl.program_id(1)
    last = pl.num_programs(1) - 1
    n_used = nu_ref[0]
    used = i < n_used
    slot = i % 2
    rows = h_ref.shape[0]

    def fetch(ids_ref, s):
        def body(r, carry):
            _row_copy(x_ref, ids_ref[0, 0, r], xbuf.at[s], r, sems.at[s]).start()
            return carry
        lax.fori_loop(0, rows, body, 0, unroll=8)

    @pl.when((i == 0) & (f == 0))
    def _():
        fetch(idx_ref, 0)

    @pl.when(used & (f == 0))
    def _():
        def drain(r, carry):
            _row_copy(x_ref, 0, xbuf.at[slot], r, sems.at[slot]).wait()
            return carry
        lax.fori_loop(0, rows, drain, 0, unroll=8)
        h_ref[...] = xbuf[slot].astype(BF16)
        acc_ref[...] = jnp.zeros(acc_ref.shape, F32)

        @pl.when(i + 1 < n_used)
        def _():
            fetch(idxn_ref, 1 - slot)

    @pl.when(used)
    def _():
        h = h_ref[...]
        a = jnp.dot(h, wg_ref[0, 0].astype(BF16), preferred_element_type=F32)
        u = jnp.dot(h, wu_ref[0, 0].astype(BF16), preferred_element_type=F32)
        act = a * (1.0 / (1.0 + jnp.exp(-a))) * u
        acc_ref[...] += jnp.dot(act.astype(BF16), wd_ref[0, 0].astype(BF16), preferred_element_type=F32)

    @pl.when(used & (f == last))
    def _():
        o_ref[...] = acc_ref[...]

    @pl.when(jnp.logical_not(used) & (f == last))
    def _():
        o_ref[...] = jnp.zeros(o_ref.shape, F32)


def _moe_experts(h2, src_tok, tile_expert, n_used, wg, wu, wd, layer, tg, tf):
    d = h2.shape[1]
    n_tiles = src_tok.shape[0] // tg
    ff = wg.shape[3]
    ids = src_tok.reshape(n_tiles, 1, tg)
    grid_spec = pltpu.PrefetchScalarGridSpec(
        num_scalar_prefetch=2,
        grid=(n_tiles, ff // tf),
        in_specs=[pl.BlockSpec((1, 1, tg), lambda i, f, te, nu: (i, 0, 0), memory_space=pltpu.SMEM),
                  pl.BlockSpec((1, 1, tg), lambda i, f, te, nu: (jnp.minimum(i + 1, n_tiles - 1), 0, 0),
                               memory_space=pltpu.SMEM),
                  pl.BlockSpec(memory_space=pl.ANY),
                  pl.BlockSpec((1, 1, d, tf), lambda i, f, te, nu: (layer, te[i], 0, f)),
                  pl.BlockSpec((1, 1, d, tf), lambda i, f, te, nu: (layer, te[i], 0, f)),
                  pl.BlockSpec((1, 1, tf, d), lambda i, f, te, nu: (layer, te[i], f, 0))],
        out_specs=pl.BlockSpec((tg, d), lambda i, f, te, nu: (i, 0)),
        scratch_shapes=[pltpu.VMEM((2, tg, d), F32), pltpu.VMEM((tg, d), BF16), pltpu.VMEM((tg, d), F32),
                        pltpu.SemaphoreType.DMA((2,))])
    return pl.pallas_call(
        _moe_experts_kernel,
        grid_spec=grid_spec,
        out_shape=jax.ShapeDtypeStruct((n_tiles * tg, d), F32),
        compiler_params=_cparams(("arbitrary", "arbitrary")),
        name="moe_experts",
    )(tile_expert, n_used, ids, ids, h2, wg, wu, wd)


def _moe_combine_kernel(i0_ref, i1_ref, n0_ref, n1_ref, y_ref, x_ref, r_ref, gate_ref, o_ref, buf, sems):
    i = pl.program_id(0)
    slot = i % 2
    rows = o_ref.shape[0]

    def fetch(a_ref, b_ref, s):
        def body(r, carry):
            _row_copy(y_ref, a_ref[0, 0, r], buf.at[s, 0], r, sems.at[s]).start()
            _row_copy(y_ref, b_ref[0, 0, r], buf.at[s, 1], r, sems.at[s]).start()
            return carry
        lax.fori_loop(0, rows, body, 0, unroll=8)

    @pl.when(i == 0)
    def _():
        fetch(i0_ref, i1_ref, 0)

    @pl.when(i + 1 < pl.num_programs(0))
    def _():
        fetch(n0_ref, n1_ref, 1 - slot)

    def drain(r, carry):
        _row_copy(y_ref, 0, buf.at[slot, 0], r, sems.at[slot]).wait()
        _row_copy(y_ref, 0, buf.at[slot, 1], r, sems.at[slot]).wait()
        return carry
    lax.fori_loop(0, rows, drain, 0, unroll=8)
    y = r_ref[:, 2:3] * buf[slot, 0] + r_ref[:, 3:4] * buf[slot, 1]
    o_ref[...] = x_ref[...] + gate_ref[0] * y


def _moe_combine(xs, yg, pos, rinfo, mod, tm, modrow):
    t, d = xs.shape
    nt = t // tm
    p3 = pos.reshape(2, nt, 1, tm)

    def cur(i):
        return (i, 0, 0)

    def nxt(i):
        return (jnp.minimum(i + 1, nt - 1), 0, 0)

    return pl.pallas_call(
        _moe_combine_kernel,
        grid=(nt,),
        in_specs=[pl.BlockSpec((1, 1, tm), cur, memory_space=pltpu.SMEM),
                  pl.BlockSpec((1, 1, tm), cur, memory_space=pltpu.SMEM),
                  pl.BlockSpec((1, 1, tm), nxt, memory_space=pltpu.SMEM),
                  pl.BlockSpec((1, 1, tm), nxt, memory_space=pltpu.SMEM),
                  pl.BlockSpec(memory_space=pl.ANY),
                  pl.BlockSpec((tm, d), lambda i: (i, 0)),
                  pl.BlockSpec((tm, LANES), lambda i: (i, 0)),
                  pl.BlockSpec((1, 1, d), lambda i: (modrow(i), 0, 5))],
        out_specs=pl.BlockSpec((tm, d), lambda i: (i, 0)),
        out_shape=jax.ShapeDtypeStruct((t, d), F32),
        scratch_shapes=[pltpu.VMEM((2, 2, tm, d), F32), pltpu.SemaphoreType.DMA((2,))],
        compiler_params=_cparams(("arbitrary",)),
        name="moe_combine",
    )(p3[0], p3[1], p3[0], p3[1], yg, xs, rinfo, mod)


def _moe(xs, g, mod, router_p, wg, wu, wd, layer, tm, tf, modrow):
    tg = tm
    h2, rinfo = _router(xs, g, mod, router_p, tm, modrow)
    src_tok, tile_expert, n_used, pos = _route_plan(rinfo, tg)
    yg = _moe_experts(h2, src_tok, tile_expert, n_used, wg, wu, wd, layer, tg, tf)
    return _moe_combine(xs, yg, pos, rinfo, mod, tm, modrow)


def _final_norm_kernel(x_ref, g_ref, o_ref):
    x = x_ref[...]
    ms = jnp.mean(x * x, axis=-1, keepdims=True)
    o_ref[...] = x * lax.rsqrt(ms + EPS) * g_ref[...]


def _final_norm(xs, g, rows, tm):
    d = xs.shape[1]
    return pl.pallas_call(
        _final_norm_kernel,
        grid=(rows // tm,),
        in_specs=[pl.BlockSpec((tm, d), lambda i: (i, 0)), pl.BlockSpec((1, d), lambda i: (0, 0))],
        out_specs=pl.BlockSpec((tm, d), lambda i: (i, 0)),
        out_shape=jax.ShapeDtypeStruct((rows, d), F32),
        compiler_params=_cparams(("parallel",)),
        name="final_norm",
    )(xs, g.reshape(1, d))


def _pad_heads(v, width, used):
    lead = v.shape[:-1]
    v = v.reshape(lead + (C_HEADS, used))
    v = jnp.pad(v, [(0, 0)] * len(lead) + [(0, 0), (0, width - used)])
    return v.reshape(lead + (C_HEADS * width,))


def kernel(x, c, ctx, c_ctx, norm1_g, norm2_g, ada_w, ada_b, w_in, w_out, a_lambda, a_norm_g, b_sink,
           c_gate_w2, c_gate_b, c_norm_g, ffn_w_gate, ffn_w_up, ffn_w_down, moe_router, moe_w_gate,
           moe_w_up, moe_w_down, final_g):
    nb, seq, d = x.shape
    nctx = ctx.shape[1]
    depth = w_in.shape[0]
    n_lat = nb * seq
    tm = nb * nctx
    assert seq % tm == 0 and nb < 8
    tm_r = tm // 2
    n_lat_tiles = n_lat // tm

    def modrow(i):
        return jnp.where(i < n_lat_tiles, i // (seq // tm), nb)

    xs = jnp.concatenate([x.reshape(n_lat, d), ctx.reshape(nb * nctx, d)], axis=0)
    cc = jnp.zeros((8, d), F32).at[:nb].set(c).at[nb].set(c_ctx)
    mod_all = _modulation(cc, ada_w, ada_b).reshape(depth, 8, 1, 6 * d)
    tables = _rope_tables(seq, tm_r)

    for layer in range(depth):
        lam_init = 0.8 - 0.6 * math.exp(-0.3 * layer)
        mod = mod_all[layer]
        w_pad = jnp.concatenate([w_in[layer], jnp.zeros((d, 1), F32)], axis=1)
        w_rope = jnp.take(w_pad, _ROPE_COLS, axis=1).astype(BF16)
        w_plain = jnp.take(w_pad, _PLAIN_COLS, axis=1).astype(BF16)
        w_vt = jnp.take(w_pad, _VT_COLS, axis=1).T.astype(BF16)

        aq, ak, bq, bk = _normproj(xs, norm1_g[layer], mod, 1, 0, w_rope, ROPE_WIDTHS, tm_r,
                                   n_lat // tm_r, seq // tm_r, nb, tables)
        cq, ck, cv, cr, cg, avt, bvt, cvt = _normproj(xs, norm1_g[layer], mod, 1, 0, w_plain, PLAIN_WIDTHS, tm,
                                                      n_lat_tiles, seq // tm, nb, wt=w_vt, vt_widths=VT_WIDTHS)

        a_lat = _diff_attn(aq, ak, avt, a_lambda[layer], a_norm_g[layer], lam_init, nb, seq, nctx, True)
        a_ctx = _diff_attn(aq, ak, avt, a_lambda[layer], a_norm_g[layer], lam_init, nb, seq, nctx, False)
        sink_row = jnp.zeros((1, LANES), F32).at[0, :B_HEADS].set(b_sink[layer] * LOG2E)
        b_lat = _win_attn(bq, bk, bvt, sink_row, nb, seq, nctx, True)
        b_ctx = _win_attn(bq, bk, bvt, sink_row, nb, seq, nctx, False)

        w2 = c_gate_w2[layer]
        w2p = [jnp.zeros((LANES, C_HEADS * LANES), F32).at[dd * C_RANK:(dd + 1) * C_RANK].set(
            _pad_heads(w2[dd], LANES, C_DK)).astype(BF16) for dd in range(2)]
        bgp = [_pad_heads(c_gate_b[layer, dd], LANES, C_DK).reshape(1, -1) for dd in range(2)]
        ng = _pad_heads(jnp.tile(c_norm_g[layer], C_HEADS), LANES, C_DV).reshape(1, -1)
        o_f = _gla_dir(cq, ck, cv, cvt, cg, w2p[0], bgp[0], nb, seq, nctx, False)
        g_out = _gla_dir(cq, ck, cv, cvt, cg, w2p[1], bgp[1], nb, seq, nctx, True, o_f, cr, ng)

        a_all = jnp.concatenate([a_lat, a_ctx], axis=0)
        b_all = jnp.concatenate([b_lat, b_ctx], axis=0)
        wo = w_out[layer]
        wa = wo[:256].astype(BF16)
        wb = wo[256:640].astype(BF16)
        wc = jnp.pad(wo[640:].reshape(C_HEADS, C_DV, d), ((0, 0), (0, LANES - C_DV), (0, 0))).reshape(
            C_HEADS * LANES, d).astype(BF16)
        xs = _outproj(xs, a_all, b_all, g_out, wa, wb, wc, mod, tm, modrow)

        j = layer // 2
        if layer % 2 == 0:
            xs = _ffn(xs, norm2_g[layer], mod, ffn_w_gate[j], ffn_w_up[j], ffn_w_down[j], tm, 256, modrow)
        else:
            router_p = jnp.pad(moe_router[j], ((0, 0), (0, LANES - N_EXPERTS)))
            xs = _moe(xs, norm2_g[layer], mod, router_p, moe_w_gate, moe_w_up, moe_w_down, j,
                      tm, 512, modrow)

    return _final_norm(xs, final_g, n_lat, tm).reshape(nb, seq, d)
```

```python
import functools
import math

import numpy as np
import jax
import jax.numpy as jnp
from jax import lax
from jax.experimental import pallas as pl
from jax.experimental.pallas import tpu as pltpu

F32 = jnp.float32
BF16 = jnp.bfloat16

EPS = 1e-6
ROPE_BASE = 10000.0
GRID_W = 64
LANES = 128
LOG2E = math.log2(math.e)
NEG = -1e30

A_HEADS, A_QK, A_V = 4, 32, 64
B_HEADS, B_KV, B_HD, WINDOW, BLOCK = 6, 2, 64, 128, 128
C_HEADS, C_DK, C_DV, C_RANK, C_GATE_NORM, C_CHUNK = 4, 48, 96, 16, 16.0, 64
N_EXPERTS = 8
IN_SIZES = (256, 256, 256, 384, 128, 128, 192, 192, 384, 384, 32)
IN_W = sum(IN_SIZES)

ROPE_WIDTHS = (256, 256, 768, 128)
PLAIN_WIDTHS = (512, 512, 512, 512, 128)
VT_WIDTHS = (256, 128, 512)
ROPE_W = sum(ROPE_WIDTHS)
ROPE_QUARTERS = (A_QK // 4, A_QK // 4, B_HD // 4, B_HD // 4)
GLA_GROUP = 256
VMEM_LIMIT = 48 * 1024 * 1024


def _cparams(sem):
    return pltpu.CompilerParams(dimension_semantics=sem, vmem_limit_bytes=VMEM_LIMIT)


def _column_maps():
    off = np.concatenate([[0], np.cumsum(IN_SIZES)])
    aq0, ak0, av0, bq0, bk0, bv0, cq0, ck0, cv0, cr0, cg0 = [int(v) for v in off[:11]]
    zero = IN_W

    main, dim, dd, scale = [], [], [], []
    a_scale = A_QK ** -0.5 * LOG2E
    b_scale = B_HD ** -0.5 * LOG2E
    for base, sc in ((aq0, a_scale), (ak0, 1.0)):
        for j in range(256):
            main.append(base + j)
            dim.append(32); dd.append(j % 32); scale.append(sc)
    for t in range(B_HEADS):
        g = t // (B_HEADS // B_KV)
        for lane in range(LANES):
            d = lane % 64
            main.append(bq0 + t * 64 + d if lane // 64 == g else zero)
            dim.append(64); dd.append(d); scale.append(b_scale)
    for j in range(128):
        main.append(bk0 + j)
        dim.append(64); dd.append(j % 64); scale.append(1.0)

    plain = []
    for base in (cq0, ck0):
        for h in range(C_HEADS):
            plain += [base + h * C_DK + d if d < C_DK else zero for d in range(LANES)]
    for base in (cv0, cr0):
        for h in range(C_HEADS):
            plain += [base + h * C_DV + d if d < C_DV else zero for d in range(LANES)]
    plain += [cg0 + d if d < 2 * C_RANK else zero for d in range(LANES)]
    vt = list(range(av0, av0 + 256)) + list(range(bv0, bv0 + 128))
    for h in range(C_HEADS):
        vt += [cv0 + h * C_DV + d if d < C_DV else zero for d in range(LANES)]
    return (np.array(main, np.int32), np.array(plain, np.int32), np.array(vt, np.int32),
            np.array(dim), np.array(dd), np.array(scale, np.float32))


_ROPE_COLS, _PLAIN_COLS, _VT_COLS, _R_DIM, _R_D, _R_SCALE = _column_maps()


def _rope_tables(seq, pad_rows):
    sec_start = np.cumsum((0,) + ROPE_WIDTHS[:-1])
    cols = np.concatenate([np.arange(s, s + LANES) for s in sec_start for _ in range(2)])
    is_sin = jnp.asarray(np.tile(np.repeat([False, True], LANES), len(ROPE_WIDTHS)))[None, :]
    r_dim, r_d, r_scale = _R_DIM[cols], _R_D[cols], _R_SCALE[cols]
    quarter = r_dim // 4
    half = r_dim // 2
    is_col = jnp.asarray((r_d % r_dim) >= half)
    ddh = r_d % half
    first = jnp.asarray(ddh < quarter)[None, :]
    f = (ddh % quarter).astype(np.float32)
    inv = jnp.asarray(ROPE_BASE, F32) ** (-jnp.asarray(f) / jnp.asarray(quarter.astype(np.float32)))
    scale = jnp.asarray(r_scale)[None, :]

    def trig(n):
        ang = jnp.arange(n, dtype=F32)[:, None] * inv[None, :]
        return jnp.where(is_sin, jnp.where(first, -jnp.sin(ang), jnp.sin(ang)), jnp.cos(ang)) * scale

    t_row = trig(seq // GRID_W)
    t_col = trig(GRID_W)
    tab = jnp.where(is_col[None, None, :], t_col[None, :, :], t_row[:, None, :]).reshape(seq, -1)
    ident = jnp.broadcast_to(jnp.where(is_sin, 0.0, scale), (pad_rows, tab.shape[1]))
    return jnp.concatenate([tab, ident], axis=0)


def _mod_kernel(c_ref, w_ref, b_ref, o_ref):
    c = c_ref[...]
    s = c * (1.0 / (1.0 + jnp.exp(-c)))
    o_ref[0] = jnp.dot(s, w_ref[0], precision=lax.Precision.HIGHEST,
                       preferred_element_type=F32) + b_ref[0]


def _modulation(cc, ada_w, ada_b):
    depth, d, n = ada_w.shape
    tn = n // 4
    return pl.pallas_call(
        _mod_kernel,
        grid=(depth, n // tn),
        in_specs=[pl.BlockSpec((8, d), lambda l, j: (0, 0)),
                  pl.BlockSpec((1, d, tn), lambda l, j: (l, 0, j)),
                  pl.BlockSpec((1, 1, tn), lambda l, j: (l, 0, j))],
        out_specs=pl.BlockSpec((1, 8, tn), lambda l, j: (l, 0, j)),
        out_shape=jax.ShapeDtypeStruct((depth, 8, n), F32),
        compiler_params=_cparams(("arbitrary", "arbitrary")),
        name="adaln_mod",
    )(cc, ada_w, ada_b.reshape(depth, 1, n))


def _norm_mod(x, g, sc, sh):
    ms = jnp.mean(x * x, axis=-1, keepdims=True)
    return (x * lax.rsqrt(ms + EPS) * g) * (1.0 + sc) + sh


def _normproj_kernel(x_ref, g_ref, sc_ref, sh_ref, w_ref, *rest, rope, widths, vt_widths):
    if rope:
        t_ref = rest[0]
        rest = rest[1:]
    if vt_widths:
        wt_ref = rest[0]
        rest = rest[1:]
    outs = rest[:len(widths)]
    vt_outs = rest[len(widths):]
    h = _norm_mod(x_ref[...], g_ref[...], sc_ref[0], sh_ref[0]).astype(BF16)
    acc = jnp.dot(h, w_ref[...], preferred_element_type=F32)
    lane = lax.broadcasted_iota(jnp.int32, (1, LANES), 1)
    off = 0
    for sec, (o_ref, w) in enumerate(zip(outs, widths)):
        if rope:
            cos = t_ref[:, (2 * sec) * LANES:(2 * sec + 1) * LANES]
            sin = t_ref[:, (2 * sec + 1) * LANES:(2 * sec + 2) * LANES]
            qd = rope[sec]
            first = (lane & (2 * qd - 1)) < qd
            for j in range(w // LANES):
                x = acc[:, off + j * LANES:off + (j + 1) * LANES]
                partner = jnp.where(first, pltpu.roll(x, LANES - qd, axis=1), pltpu.roll(x, qd, axis=1))
                o_ref[:, j * LANES:(j + 1) * LANES] = (x * cos + partner * sin).astype(o_ref.dtype)
        else:
            o_ref[...] = acc[:, off:off + w].astype(o_ref.dtype)
        off += w
    off = 0
    for o_ref, w in zip(vt_outs, vt_widths):
        o_ref[...] = lax.dot_general(wt_ref[off:off + w, :], h, (((1,), (1,)), ((), ())),
                                     preferred_element_type=F32).astype(o_ref.dtype)
        off += w


def _normproj(xs, g, mod, sc_chunk, sh_chunk, w, widths, tm, n_lat_tiles, tiles_per_batch, nb, tables=None,
              wt=None, vt_widths=()):
    t, d = xs.shape
    rope = ROPE_QUARTERS if tables is not None else ()

    def modrow(i):
        return jnp.where(i < n_lat_tiles, i // tiles_per_batch, nb)

    in_specs = [pl.BlockSpec((tm, d), lambda i: (i, 0)),
                pl.BlockSpec((1, d), lambda i: (0, 0)),
                pl.BlockSpec((1, 1, d), lambda i: (modrow(i), 0, sc_chunk)),
                pl.BlockSpec((1, 1, d), lambda i: (modrow(i), 0, sh_chunk)),
                pl.BlockSpec(w.shape, lambda i: (0, 0))]
    args = [xs, g.reshape(1, d), mod, mod, w]
    if rope:
        def tabrow(i):
            return jnp.where(i < n_lat_tiles, i % tiles_per_batch, tiles_per_batch)
        in_specs += [pl.BlockSpec((tm, tables.shape[1]), lambda i: (tabrow(i), 0))]
        args += [tables]
    if vt_widths:
        in_specs += [pl.BlockSpec(wt.shape, lambda i: (0, 0))]
        args += [wt]
    return pl.pallas_call(
        functools.partial(_normproj_kernel, rope=rope, widths=widths, vt_widths=vt_widths),
        grid=(t // tm,),
        in_specs=in_specs,
        out_specs=([pl.BlockSpec((tm, wd), lambda i: (i, 0)) for wd in widths]
                   + [pl.BlockSpec((wd, tm), lambda i: (0, i)) for wd in vt_widths]),
        out_shape=([jax.ShapeDtypeStruct((t, wd), BF16) for wd in widths]
                   + [jax.ShapeDtypeStruct((wd, t), BF16) for wd in vt_widths]),
        compiler_params=_cparams(("parallel",)),
        name="normproj_rope" if rope else "normproj_plain",
    )(*args)


def _diff_attn_kernel(*refs, lam_init, has_lat, tk):
    if has_lat:
        q_ref, kc_ref, vtc_ref, kl_ref, vtl_ref, lam_ref, g_ref, o_ref, m_scr, acc_scr, mc_scr, s_scr = refs
    else:
        q_ref, kc_ref, vtc_ref, lam_ref, g_ref, o_ref, m_scr, acc_scr, mc_scr, s_scr = refs
    q = q_ref[...]
    lane = lax.broadcasted_iota(jnp.int32, (1, LANES), 1)
    nt = (((1,), (1,)), ((), ()))
    qms = [jnp.where((lane >= i * A_QK) & (lane < (i + 1) * A_QK), q, jnp.zeros_like(q)) for i in range(4)]

    sub = 512

    def scores(k, slot, nk):
        for i in range(4):
            s = lax.dot_general(k, qms[i], nt, preferred_element_type=F32)
            s_scr[slot, i, 0:nk, :] = s
            mc_scr[slot, i] = jnp.max(s, axis=0, keepdims=True)

    def consume(vt_of, slot, nk, first):
        for i in range(4):
            m_cur = mc_scr[slot, i]
            if first:
                m_new = m_cur
            else:
                m_run = m_scr[i]
                m_new = jnp.maximum(m_run, m_cur)
            pv = None
            for t in range(nk // min(sub, nk)):
                w = min(sub, nk)
                p = jnp.exp2(s_scr[slot, i, t * w:(t + 1) * w, :] - m_new).astype(BF16)
                d = jnp.dot(vt_of(i // 2, t, w), p, preferred_element_type=F32)
                pv = d if pv is None else pv + d
            if first:
                acc_scr[i] = pv
            else:
                acc_scr[i] = jnp.exp2(m_run - m_new) * acc_scr[i] + pv
            m_scr[i] = m_new

    nctx = kc_ref.shape[0]
    scores(kc_ref[...], 1, nctx)

    def with_ones(vt):
        return jnp.concatenate([vt, jnp.ones((8, vt.shape[1]), vt.dtype)], axis=0)

    def vt_ctx(hh, t, w):
        return with_ones(vtc_ref[hh * A_V:(hh + 1) * A_V, t * w:(t + 1) * w])

    if not has_lat:
        consume(vt_ctx, 1, nctx, True)
    else:
        n_chunks = kl_ref.shape[0] // tk

        def k_lat(c):
            return kl_ref[pl.ds(pl.multiple_of(c * tk, tk), tk), :]

        def vt_lat(c):
            def get(hh, t, w):
                return with_ones(vtl_ref[hh * A_V:(hh + 1) * A_V, pl.ds(pl.multiple_of(c * tk + t * w, w), w)])
            return get

        scores(k_lat(0), 0, tk)
        consume(vt_ctx, 1, nctx, True)

        def body(j, carry):
            c = 2 * j
            scores(k_lat(c + 1), 1, tk)
            consume(vt_lat(c), 0, tk, False)
            scores(k_lat(c + 2), 0, tk)
            consume(vt_lat(c + 1), 1, tk, False)
            return carry
        pairs = (n_chunks - 1) // 2
        lax.fori_loop(0, pairs, body, 0)
        c_last = 2 * pairs
        if c_last + 1 < n_chunks:
            scores(k_lat(c_last + 1), 1, tk)
            consume(vt_lat(c_last), 0, tk, False)
            consume(vt_lat(c_last + 1), 1, tk, False)
        else:
            consume(vt_lat(c_last), 0, tk, False)

    lp = lam_ref[...]
    lam = (jnp.exp(jnp.sum(lp[0:1] * lp[1:2], axis=1, keepdims=True))
           - jnp.exp(jnp.sum(lp[2:3] * lp[3:4], axis=1, keepdims=True)) + lam_init)
    heads = []
    for hh in range(2):
        maps = []
        for m in range(2):
            acc = acc_scr[hh * 2 + m]
            maps.append(acc[0:A_V] / acc[A_V:A_V + 1])
        oh = maps[0] - lam * maps[1]
        ms = jnp.mean(oh * oh, axis=0, keepdims=True)
        heads.append(oh * lax.rsqrt(ms + EPS) * g_ref[...] * (1.0 - lam_init))
    o_ref[...] = jnp.concatenate(heads, axis=0).T.astype(o_ref.dtype)


def _diff_attn(aq, ak, avt, lam_p, norm_g, lam_init, nb, seq, ctx, latent):
    tq = 256
    tk = min(1024, seq)
    kern = functools.partial(_diff_attn_kernel, lam_init=lam_init, has_lat=latent, tk=tk)
    ctx_blk0 = nb * seq // ctx
    vtc_spec = pl.BlockSpec((2 * A_V, ctx), lambda b, p, i: (p, ctx_blk0 + b))
    kc_spec = pl.BlockSpec((ctx, LANES), lambda b, p, i: (ctx_blk0 + b, p))
    par_specs = [pl.BlockSpec((4, A_QK), lambda b, p, i: (0, 0)),
                 pl.BlockSpec((A_V, 1), lambda b, p, i: (0, 0))]
    if latent:
        nq = seq // tq
        in_specs = [pl.BlockSpec((tq, LANES), lambda b, p, i: (b * nq + i, p)), kc_spec, vtc_spec,
                    pl.BlockSpec((seq, LANES), lambda b, p, i: (b, p)),
                    pl.BlockSpec((2 * A_V, seq), lambda b, p, i: (p, b))] + par_specs
        args = (aq, ak, avt, ak, avt, lam_p, norm_g.reshape(A_V, 1))
        rows = nb * seq
    else:
        nq = ctx // tq
        q_blk0 = nb * seq // tq
        in_specs = [pl.BlockSpec((tq, LANES), lambda b, p, i: (q_blk0 + b * nq + i, p)),
                    kc_spec, vtc_spec] + par_specs
        args = (aq, ak, avt, lam_p, norm_g.reshape(A_V, 1))
        rows = nb * ctx
    return pl.pallas_call(
        kern,
        grid=(nb, 2, nq),
        in_specs=in_specs,
        out_specs=pl.BlockSpec((tq, LANES), lambda b, p, i: (b * nq + i, p)),
        out_shape=jax.ShapeDtypeStruct((rows, 2 * LANES), BF16),
        scratch_shapes=[pltpu.VMEM((4, 1, tq), F32), pltpu.VMEM((4, A_V + 8, tq), F32),
                        pltpu.VMEM((2, 4, 1, tq), F32),
                        pltpu.VMEM((2, 4, tk if latent else ctx, tq), F32)],
        compiler_params=_cparams(("parallel", "parallel", "arbitrary")),
        name="diff_attn_lat" if latent else "diff_attn_ctx",
    )(*args)


def _win_attn_kernel(*refs, with_window, seq, ctx):
    if with_window:
        q_ref, kc_ref, vtc_ref, kp_ref, kn0_ref, kn_ref, vtp_ref, vt0_ref, vtn_ref, sink_ref, o_ref = refs
        n = pl.program_id(1)
        k_all = jnp.concatenate([kc_ref[...], kp_ref[...], kn0_ref[...], kn_ref[...]], axis=0)
        vt_all = jnp.concatenate([vtc_ref[...], vtp_ref[...], vt0_ref[...], vtn_ref[...]], axis=1)
        nk = ctx + 3 * BLOCK
        r = lax.broadcasted_iota(jnp.int32, (nk, BLOCK), 0)
        c = lax.broadcasted_iota(jnp.int32, (nk, BLOCK), 1)
        krel = r - (ctx + BLOCK)
        in_win = ((jnp.abs(c - krel) <= WINDOW) & (krel >= -n * BLOCK) & (krel < seq - n * BLOCK))
        is_ctx = r < ctx
    else:
        q_ref, kc_ref, vtc_ref, sink_ref, o_ref = refs
        k_all = kc_ref[...]
        vt_all = vtc_ref[...]
    lane = lax.broadcasted_iota(jnp.int32, (1, LANES), 1)
    sink_row = sink_ref[...]
    nt = (((1,), (1,)), ((), ()))
    rep = B_HEADS // B_KV
    outs = []
    for j in range(B_HEADS):
        g = j // rep
        s = lax.dot_general(k_all, q_ref[:, j * LANES:(j + 1) * LANES], nt, preferred_element_type=F32)
        if with_window:
            s = jnp.where(is_ctx, s, jnp.where(in_win, s, NEG))
        sk = jnp.max(jnp.where(lane == j, sink_row, NEG), axis=1, keepdims=True)
        m = jnp.maximum(jnp.max(s, axis=0, keepdims=True), sk)
        p = jnp.exp2(s - m)
        l = jnp.sum(p, axis=0, keepdims=True) + jnp.exp2(sk - m)
        ot = jnp.dot(vt_all, p.astype(BF16), preferred_element_type=F32)
        outs.append(ot[g * B_HD:(g + 1) * B_HD] / l)
    for t in range(B_HEADS // 2):
        pair = jnp.concatenate([outs[2 * t], outs[2 * t + 1]], axis=0)
        o_ref[:, t * LANES:(t + 1) * LANES] = pair.T.astype(o_ref.dtype)


def _win_attn(bq, bk, bvt, sink_row, nb, seq, ctx, latent):
    nblk = seq // BLOCK
    ctx_blk0 = nb * seq // ctx
    kern = functools.partial(_win_attn_kernel, with_window=latent, seq=seq, ctx=ctx)
    kc_spec = pl.BlockSpec((ctx, LANES), lambda b, n: (ctx_blk0 + b, 0))
    vtc_spec = pl.BlockSpec((LANES, ctx), lambda b, n: (0, ctx_blk0 + b))
    sink_spec = pl.BlockSpec((1, LANES), lambda b, n: (0, 0))
    qw = B_HEADS * LANES
    if latent:
        def prev(n):
            return jnp.maximum(n - 1, 0)

        def nxt(n):
            return jnp.minimum(n + 1, nblk - 1)

        in_specs = [pl.BlockSpec((BLOCK, qw), lambda b, n: (b * nblk + n, 0)), kc_spec, vtc_spec,
                    pl.BlockSpec((BLOCK, LANES), lambda b, n: (b * nblk + prev(n), 0)),
                    pl.BlockSpec((BLOCK, LANES), lambda b, n: (b * nblk + n, 0)),
                    pl.BlockSpec((BLOCK, LANES), lambda b, n: (b * nblk + nxt(n), 0)),
                    pl.BlockSpec((LANES, BLOCK), lambda b, n: (0, b * nblk + prev(n))),
                    pl.BlockSpec((LANES, BLOCK), lambda b, n: (0, b * nblk + n)),
                    pl.BlockSpec((LANES, BLOCK), lambda b, n: (0, b * nblk + nxt(n))),
                    sink_spec]
        args = (bq, bk, bvt, bk, bk, bk, bvt, bvt, bvt, sink_row)
        grid = (nb, nblk)
        rows = nb * seq
        nq = nblk
    else:
        nq = ctx // BLOCK
        q_blk0 = nb * seq // BLOCK
        in_specs = [pl.BlockSpec((BLOCK, qw), lambda b, n: (q_blk0 + b * nq + n, 0)),
                    kc_spec, vtc_spec, sink_spec]
        args = (bq, bk, bvt, sink_row)
        grid = (nb, nq)
        rows = nb * ctx
    return pl.pallas_call(
        kern,
        grid=grid,
        in_specs=in_specs,
        out_specs=pl.BlockSpec((BLOCK, B_HEADS * B_HD), lambda b, n: (b * nq + n, 0)),
        out_shape=jax.ShapeDtypeStruct((rows, B_HEADS * B_HD), BF16),
        compiler_params=_cparams(("parallel", "arbitrary")),
        name="win_attn_lat" if latent else "win_attn_ctx",
    )(*args)


def _split3(x):
    hi = x.astype(BF16)
    r1 = x - hi.astype(F32)
    mid = r1.astype(BF16)
    lo = (r1 - mid.astype(F32)).astype(BF16)
    return hi, mid, lo


def _gla_kernel(*refs, reverse, final):
    if final:
        q_ref, k_ref, v_ref, vt_ref, gl_ref, w2_ref, bg_ref, of_ref, r_ref, ng_ref, o_ref, st_ref = refs
    else:
        q_ref, k_ref, v_ref, vt_ref, gl_ref, w2_ref, bg_ref, o_ref, st_ref = refs
    gsz = GLA_GROUP
    nch = gsz // C_CHUNK

    @pl.when(pl.program_id(1) == 0)
    def _():
        st_ref[...] = jnp.zeros(st_ref.shape, F32)

    q = q_ref[...].astype(F32)
    k = k_ref[...].astype(F32)
    pre = jnp.dot(gl_ref[...], w2_ref[...], preferred_element_type=F32) + bg_ref[...]
    la = (jnp.minimum(pre, 0.0) - jnp.log(1.0 + jnp.exp(-jnp.abs(pre)))) * (1.0 / C_GATE_NORM)

    r = lax.broadcasted_iota(jnp.int32, (gsz, gsz), 0)
    c = lax.broadcasted_iota(jnp.int32, (gsz, gsz), 1)
    same = (r // C_CHUNK) == (c // C_CHUNK)
    tri = same & ((c >= r) if reverse else (c <= r))
    tri_b = jnp.where(tri, 1.0, 0.0).astype(BF16)
    parts = _split3(la)
    bcum = sum(jnp.dot(tri_b, p, preferred_element_type=F32) for p in parts)
    edge = 0 if reverse else C_CHUNK - 1
    btot = jnp.concatenate(
        [jnp.broadcast_to(bcum[ci * C_CHUNK + edge:ci * C_CHUNK + edge + 1], (C_CHUNK, bcum.shape[1]))
         for ci in range(nch)], axis=0)
    q_dec = q * (jnp.exp(bcum) * (C_DK ** -0.5))
    k_inv = k * jnp.exp(-bcum)
    k_end = k * jnp.exp(btot - bcum)
    dec = jnp.exp(btot)

    rowid = lax.broadcasted_iota(jnp.int32, (gsz, LANES), 0) // C_CHUNK
    nt = (((1,), (1,)), ((), ()))
    order = list(range(nch))[::-1] if reverse else list(range(nch))
    outs = []
    for h in range(C_HEADS):
        sl = slice(h * LANES, (h + 1) * LANES)
        qd = q_dec[:, sl].astype(BF16)
        ki = k_inv[:, sl].astype(BF16)
        ke = k_end[:, sl].astype(BF16)
        att = lax.dot_general(qd, ki, nt, preferred_element_type=F32)
        att = jnp.where(tri, att, 0.0).astype(BF16)
        o_h = jnp.dot(att, v_ref[:, sl], preferred_element_type=F32)
        vth = vt_ref[sl, :]
        st = st_ref[h]
        inter = [None] * nch
        for ci in order:
            rows = slice(ci * C_CHUNK, (ci + 1) * C_CHUNK)
            inter[ci] = lax.dot_general(qd[rows], st.astype(BF16), nt, preferred_element_type=F32)
            ke_c = jnp.where(rowid == ci, ke, jnp.zeros_like(ke))
            upd = jnp.dot(vth, ke_c, preferred_element_type=F32)
            st = dec[ci * C_CHUNK:ci * C_CHUNK + 1, sl] * st + upd
        st_ref[h] = st
        outs.append(o_h + jnp.concatenate(inter, axis=0))
    if not final:
        for h in range(C_HEADS):
            o_ref[:, h * LANES:(h + 1) * LANES] = outs[h]
    else:
        for h in range(C_HEADS):
            sl = slice(h * LANES, (h + 1) * LANES)
            o = outs[h] + of_ref[:, sl]
            ms = jnp.sum(o * o, axis=1, keepdims=True) * (1.0 / C_DV)
            y = o * lax.rsqrt(ms + EPS) * ng_ref[:, sl]
            rr = r_ref[:, sl].astype(F32)
            o_ref[:, sl] = (y * (rr * (1.0 / (1.0 + jnp.exp(-rr))))).astype(o_ref.dtype)


def _gla_dir(cq, ck, cv, cvt, cg, w2p, bgp, nb, seq, ctx, reverse, fwd_out=None, cr=None, ng=None):
    t = cq.shape[0]
    gsz = GLA_GROUP
    assert ctx == gsz
    n_lat_g = seq // gsz
    ctx_g0 = nb * n_lat_g
    final = fwd_out is not None

    def grp(b, i):
        lat = (n_lat_g - i) if reverse else (i - 1)
        return jnp.where(i == 0, ctx_g0 + b, b * n_lat_g + lat)

    w = C_HEADS * LANES
    row_spec = pl.BlockSpec((gsz, w), lambda b, i: (grp(b, i), 0))
    in_specs = [row_spec, row_spec, row_spec,
                pl.BlockSpec((w, gsz), lambda b, i: (0, grp(b, i))),
                pl.BlockSpec((gsz, LANES), lambda b, i: (grp(b, i), 0)),
                pl.BlockSpec((LANES, w), lambda b, i: (0, 0)),
                pl.BlockSpec((1, w), lambda b, i: (0, 0))]
    args = [cq, ck, cv, cvt, cg, w2p, bgp]
    if final:
        in_specs += [row_spec, row_spec, pl.BlockSpec((1, w), lambda b, i: (0, 0))]
        args += [fwd_out, cr, ng]
    return pl.pallas_call(
        functools.partial(_gla_kernel, reverse=reverse, final=final),
        grid=(nb, n_lat_g + 1),
        in_specs=in_specs,
        out_specs=row_spec,
        out_shape=jax.ShapeDtypeStruct((t, w), BF16 if final else F32),
        scratch_shapes=[pltpu.VMEM((C_HEADS, LANES, LANES), F32)],
        compiler_params=_cparams(("parallel", "arbitrary")),
        name="gla_bwd" if reverse else "gla_fwd",
    )(*args)


def _outproj_kernel(x_ref, a_ref, b_ref, c_ref, wa_ref, wb_ref, wc_ref, g1_ref, o_ref):
    y = jnp.dot(a_ref[...], wa_ref[...], preferred_element_type=F32)
    y += jnp.dot(b_ref[...], wb_ref[...], preferred_element_type=F32)
    y += jnp.dot(c_ref[...], wc_ref[...], preferred_element_type=F32)
    o_ref[...] = x_ref[...] + g1_ref[0] * y


def _outproj(xs, a, b, c, wa, wb, wc, mod, tm, modrow):
    t, d = xs.shape
    return pl.pallas_call(
        _outproj_kernel,
        grid=(t // tm,),
        in_specs=[pl.BlockSpec((tm, d), lambda i: (i, 0)),
                  pl.BlockSpec((tm, a.shape[1]), lambda i: (i, 0)),
                  pl.BlockSpec((tm, b.shape[1]), lambda i: (i, 0)),
                  pl.BlockSpec((tm, c.shape[1]), lambda i: (i, 0)),
                  pl.BlockSpec(wa.shape, lambda i: (0, 0)),
                  pl.BlockSpec(wb.shape, lambda i: (0, 0)),
                  pl.BlockSpec(wc.shape, lambda i: (0, 0)),
                  pl.BlockSpec((1, 1, d), lambda i: (modrow(i), 0, 2))],
        out_specs=pl.BlockSpec((tm, d), lambda i: (i, 0)),
        out_shape=jax.ShapeDtypeStruct((t, d), F32),
        compiler_params=_cparams(("parallel",)),
        name="outproj",
    )(xs, a, b, c, wa, wb, wc, mod)


def _ffn_kernel(x_ref, g_ref, sc_ref, sh_ref, gate_ref, wg_ref, wu_ref, wd_ref, o_ref, h_ref, acc_ref):
    f = pl.program_id(1)

    @pl.when(f == 0)
    def _():
        h_ref[...] = _norm_mod(x_ref[...], g_ref[...], sc_ref[0], sh_ref[0]).astype(BF16)
        acc_ref[...] = jnp.zeros(acc_ref.shape, F32)

    h = h_ref[...]
    a = jnp.dot(h, wg_ref[...].astype(BF16), preferred_element_type=F32)
    u = jnp.dot(h, wu_ref[...].astype(BF16), preferred_element_type=F32)
    act = a * (1.0 / (1.0 + jnp.exp(-a))) * u
    acc_ref[...] += jnp.dot(act.astype(BF16), wd_ref[...].astype(BF16), preferred_element_type=F32)

    @pl.when(f == pl.num_programs(1) - 1)
    def _():
        o_ref[...] = x_ref[...] + gate_ref[0] * acc_ref[...]


def _ffn(xs, g, mod, wg, wu, wd, tm, tf, modrow):
    t, d = xs.shape
    ff = wg.shape[1]
    return pl.pallas_call(
        _ffn_kernel,
        grid=(t // tm, ff // tf),
        in_specs=[pl.BlockSpec((tm, d), lambda i, f: (i, 0)),
                  pl.BlockSpec((1, d), lambda i, f: (0, 0)),
                  pl.BlockSpec((1, 1, d), lambda i, f: (modrow(i), 0, 4)),
                  pl.BlockSpec((1, 1, d), lambda i, f: (modrow(i), 0, 3)),
                  pl.BlockSpec((1, 1, d), lambda i, f: (modrow(i), 0, 5)),
                  pl.BlockSpec((d, tf), lambda i, f: (0, f)),
                  pl.BlockSpec((d, tf), lambda i, f: (0, f)),
                  pl.BlockSpec((tf, d), lambda i, f: (f, 0))],
        out_specs=pl.BlockSpec((tm, d), lambda i, f: (i, 0)),
        out_shape=jax.ShapeDtypeStruct((t, d), F32),
        scratch_shapes=[pltpu.VMEM((tm, d), BF16), pltpu.VMEM((tm, d), F32)],
        compiler_params=_cparams(("parallel", "arbitrary")),
        name="ffn_swiglu",
    )(xs, g.reshape(1, d), mod, mod, mod, wg, wu, wd)


def _router_kernel(x_ref, g_ref, sc_ref, sh_ref, rt_ref, h_ref, r_ref):
    lane = lax.broadcasted_iota(jnp.int32, (1, LANES), 1)
    hf = _norm_mod(x_ref[...], g_ref[...], sc_ref[0], sh_ref[0])
    h_ref[...] = hf
    logits = jnp.dot(hf, rt_ref[...], precision=lax.Precision.HIGHEST, preferred_element_type=F32)
    lanef = lane.astype(F32)
    lg = jnp.where(lane < N_EXPERTS, logits, NEG)
    m1 = jnp.max(lg, axis=1, keepdims=True)
    i1 = jnp.min(jnp.where(lg == m1, lanef, float(LANES)), axis=1, keepdims=True)
    lg2 = jnp.where(lanef == i1, NEG, lg)
    m2 = jnp.max(lg2, axis=1, keepdims=True)
    i2 = jnp.min(jnp.where(lg2 == m2, lanef, float(LANES)), axis=1, keepdims=True)
    e2 = jnp.exp(m2 - m1)
    w1 = 1.0 / (1.0 + e2)
    r_ref[...] = jnp.where(lane == 0, i1, jnp.where(lane == 1, i2, jnp.where(lane == 2, w1,
                           jnp.where(lane == 3, e2 * w1, 0.0))))


def _router(xs, g, mod, router_p, tm, modrow):
    t, d = xs.shape
    return pl.pallas_call(
        _router_kernel,
        grid=(t // tm,),
        in_specs=[pl.BlockSpec((tm, d), lambda i: (i, 0)),
                  pl.BlockSpec((1, d), lambda i: (0, 0)),
                  pl.BlockSpec((1, 1, d), lambda i: (modrow(i), 0, 4)),
                  pl.BlockSpec((1, 1, d), lambda i: (modrow(i), 0, 3)),
                  pl.BlockSpec((d, LANES), lambda i: (0, 0))],
        out_specs=[pl.BlockSpec((tm, d), lambda i: (i, 0)), pl.BlockSpec((tm, LANES), lambda i: (i, 0))],
        out_shape=[jax.ShapeDtypeStruct((t, d), F32), jax.ShapeDtypeStruct((t, LANES), F32)],
        compiler_params=_cparams(("parallel",)),
        name="moe_router",
    )(xs, g.reshape(1, d), mod, mod, router_p)


def _route_plan(rinfo, tg):
    t = rinfo.shape[0]
    n_tiles = -(-2 * t // tg) + N_EXPERTS
    e_flat = jnp.concatenate([rinfo[:, 0], rinfo[:, 1]]).astype(jnp.int32)
    onehot = (e_flat[:, None] == jnp.arange(N_EXPERTS, dtype=jnp.int32)[None, :]).astype(jnp.int32)
    csum = jnp.cumsum(onehot, axis=0)
    rank = jnp.sum(onehot * (csum - 1), axis=1)
    counts = csum[-1]
    padded = (counts + tg - 1) // tg * tg
    ends = jnp.cumsum(padded)
    pos = jnp.sum(onehot * (ends - padded)[None, :], axis=1) + rank
    tok = jnp.tile(jnp.arange(t, dtype=jnp.int32), 2)
    src_tok = jnp.zeros((n_tiles * tg,), jnp.int32).at[pos].set(tok)
    tile_start = jnp.arange(n_tiles, dtype=jnp.int32) * tg
    tile_expert = jnp.minimum(jnp.sum((tile_start[:, None] >= ends[None, :]).astype(jnp.int32), axis=1),
                              N_EXPERTS - 1)
    n_used = (ends[-1] // tg).reshape(1)
    return src_tok, tile_expert, n_used, pos


def _row_copy(src_ref, row, dst_ref, r, sem):
    return pltpu.make_async_copy(src_ref.at[pl.ds(row, 1), :], dst_ref.at[pl.ds(r, 1), :], sem)


def _moe_experts_kernel(te_ref, nu_ref, idx_ref, idxn_ref, x_ref, wg_ref, wu_ref, wd_ref, o_ref,
                        xbuf, h_ref, acc_ref, sems, *, rows_per_step, n_steps):
    i = pl.program_id(0)
    f = pl.program_id(1)
    nf = pl.num_programs(1)
    last = nf - 1
    n_tiles = pl.num_programs(0)
    n_used = nu_ref[0]
    used = i < n_used
    slot = i % 2
    rows = h_ref.shape[0]
    per_step = rows_per_step
    n_dma = per_step * n_steps

    def drain(s):
        def body(r, carry):
            _row_copy(x_ref, 0, xbuf.at[s], r, sems.at[s]).wait()
            return carry
        lax.fori_loop(0, n_dma, body, 0, unroll=n_steps)

    @pl.when((i == 0) & (f == 0))
    def _():
        def body(r, carry):
            _row_copy(x_ref, idx_ref[0, 0, r], xbuf.at[0], r, sems.at[0]).start()
            return carry
        lax.fori_loop(0, n_dma, body, 0, unroll=n_steps)

    @pl.when((i <= n_used) & (f == 0))
    def _():
        drain(slot)

    @pl.when(used & (f == 0))
    def _():
        h_ref[...] = xbuf[slot, 0:rows].astype(BF16)
        acc_ref[...] = jnp.zeros(acc_ref.shape, F32)

    @pl.when(used)
    def _():
        for k in range(per_step):
            r = f * per_step + k
            _row_copy(x_ref, idxn_ref[0, 0, r], xbuf.at[1 - slot], r, sems.at[1 - slot]).start()
        h = h_ref[...]
        a = jnp.dot(h, wg_ref[0, 0].astype(BF16), preferred_element_type=F32)
        u = jnp.dot(h, wu_ref[0, 0].astype(BF16), preferred_element_type=F32)
        act = a * (1.0 / (1.0 + jnp.exp(-a))) * u
        acc_ref[...] += jnp.dot(act.astype(BF16), wd_ref[0, 0].astype(BF16), preferred_element_type=F32)

    @pl.when(used & (f == last))
    def _():
        o_ref[...] = acc_ref[...]

    @pl.when(used & (i == n_tiles - 1) & (f == last))
    def _():
        drain(1 - slot)

    @pl.when(jnp.logical_not(used) & (f == last))
    def _():
        o_ref[...] = jnp.zeros(o_ref.shape, F32)


def _moe_experts(h2, src_tok, tile_expert, n_used, wg, wu, wd, layer, tg, tf):
    d = h2.shape[1]
    n_tiles = src_tok.shape[0] // tg
    ff = wg.shape[3]
    nf = ff // tf
    per_step = -(-tg // nf)
    n_dma = per_step * nf
    idw = -(-n_dma // LANES) * LANES
    ids = jnp.pad(src_tok.reshape(n_tiles, 1, tg), ((0, 0), (0, 0), (0, idw - tg)))
    grid_spec = pltpu.PrefetchScalarGridSpec(
        num_scalar_prefetch=2,
        grid=(n_tiles, nf),
        in_specs=[pl.BlockSpec((1, 1, idw), lambda i, f, te, nu: (i, 0, 0), memory_space=pltpu.SMEM),
                  pl.BlockSpec((1, 1, idw), lambda i, f, te, nu: (jnp.minimum(i + 1, n_tiles - 1), 0, 0),
                               memory_space=pltpu.SMEM),
                  pl.BlockSpec(memory_space=pl.ANY),
                  pl.BlockSpec((1, 1, d, tf), lambda i, f, te, nu: (layer, te[i], 0, f)),
                  pl.BlockSpec((1, 1, d, tf), lambda i, f, te, nu: (layer, te[i], 0, f)),
                  pl.BlockSpec((1, 1, tf, d), lambda i, f, te, nu: (layer, te[i], f, 0))],
        out_specs=pl.BlockSpec((tg, d), lambda i, f, te, nu: (i, 0)),
        scratch_shapes=[pltpu.VMEM((2, -(-n_dma // 8) * 8, d), F32), pltpu.VMEM((tg, d), BF16),
                        pltpu.VMEM((tg, d), F32), pltpu.SemaphoreType.DMA((2,))])
    return pl.pallas_call(
        functools.partial(_moe_experts_kernel, rows_per_step=per_step, n_steps=nf),
        grid_spec=grid_spec,
        out_shape=jax.ShapeDtypeStruct((n_tiles * tg, d), F32),
        compiler_params=_cparams(("arbitrary", "arbitrary")),
        name="moe_experts",
    )(tile_expert, n_used, ids, ids, h2, wg, wu, wd)


def _moe_combine_kernel(i0_ref, i1_ref, n0_ref, n1_ref, y_ref, x_ref, r_ref, gate_ref, o_ref, buf, sems):
    i = pl.program_id(0)
    slot = i % 2
    rows = o_ref.shape[0]

    def fetch(a_ref, b_ref, s):
        def body(r, carry):
            _row_copy(y_ref, a_ref[0, 0, r], buf.at[s, 0], r, sems.at[s]).start()
            _row_copy(y_ref, b_ref[0, 0, r], buf.at[s, 1], r, sems.at[s]).start()
            return carry
        lax.fori_loop(0, rows, body, 0, unroll=8)

    @pl.when(i == 0)
    def _():
        fetch(i0_ref, i1_ref, 0)

    @pl.when(i + 1 < pl.num_programs(0))
    def _():
        fetch(n0_ref, n1_ref, 1 - slot)

    def drain(r, carry):
        _row_copy(y_ref, 0, buf.at[slot, 0], r, sems.at[slot]).wait()
        _row_copy(y_ref, 0, buf.at[slot, 1], r, sems.at[slot]).wait()
        return carry
    lax.fori_loop(0, rows, drain, 0, unroll=8)
    y = r_ref[:, 2:3] * buf[slot, 0] + r_ref[:, 3:4] * buf[slot, 1]
    o_ref[...] = x_ref[...] + gate_ref[0] * y


def _moe_combine(xs, yg, pos, rinfo, mod, tm, modrow):
    t, d = xs.shape
    nt = t // tm
    p3 = pos.reshape(2, nt, 1, tm)

    def cur(i):
        return (i, 0, 0)

    def nxt(i):
        return (jnp.minimum(i + 1, nt - 1), 0, 0)

    return pl.pallas_call(
        _moe_combine_kernel,
        grid=(nt,),
        in_specs=[pl.BlockSpec((1, 1, tm), cur, memory_space=pltpu.SMEM),
                  pl.BlockSpec((1, 1, tm), cur, memory_space=pltpu.SMEM),
                  pl.BlockSpec((1, 1, tm), nxt, memory_space=pltpu.SMEM),
                  pl.BlockSpec((1, 1, tm), nxt, memory_space=pltpu.SMEM),
                  pl.BlockSpec(memory_space=pl.ANY),
                  pl.BlockSpec((tm, d), lambda i: (i, 0)),
                  pl.BlockSpec((tm, LANES), lambda i: (i, 0)),
                  pl.BlockSpec((1, 1, d), lambda i: (modrow(i), 0, 5))],
        out_specs=pl.BlockSpec((tm, d), lambda i: (i, 0)),
        out_shape=jax.ShapeDtypeStruct((t, d), F32),
        scratch_shapes=[pltpu.VMEM((2, 2, tm, d), F32), pltpu.SemaphoreType.DMA((2,))],
        compiler_params=_cparams(("arbitrary",)),
        name="moe_combine",
    )(p3[0], p3[1], p3[0], p3[1], yg, xs, rinfo, mod)


def _moe(xs, g, mod, router_p, wg, wu, wd, layer, tm, tf, modrow):
    tg = tm
    h2, rinfo = _router(xs, g, mod, router_p, tm, modrow)
    src_tok, tile_expert, n_used, pos = _route_plan(rinfo, tg)
    yg = _moe_experts(h2, src_tok, tile_expert, n_used, wg, wu, wd, layer, tg, tf)
    return _moe_combine(xs, yg, pos, rinfo, mod, tm, modrow)


def _final_norm_kernel(x_ref, g_ref, o_ref):
    x = x_ref[...]
    ms = jnp.mean(x * x, axis=-1, keepdims=True)
    o_ref[...] = x * lax.rsqrt(ms + EPS) * g_ref[...]


def _final_norm(xs, g, rows, tm):
    d = xs.shape[1]
    return pl.pallas_call(
        _final_norm_kernel,
        grid=(rows // tm,),
        in_specs=[pl.BlockSpec((tm, d), lambda i: (i, 0)), pl.BlockSpec((1, d), lambda i: (0, 0))],
        out_specs=pl.BlockSpec((tm, d), lambda i: (i, 0)),
        out_shape=jax.ShapeDtypeStruct((rows, d), F32),
        compiler_params=_cparams(("parallel",)),
        name="final_norm",
    )(xs, g.reshape(1, d))


def _pad_heads(v, width, used):
    lead = v.shape[:-1]
    v = v.reshape(lead + (C_HEADS, used))
    v = jnp.pad(v, [(0, 0)] * len(lead) + [(0, 0), (0, width - used)])
    return v.reshape(lead + (C_HEADS * width,))


def kernel(x, c, ctx, c_ctx, norm1_g, norm2_g, ada_w, ada_b, w_in, w_out, a_lambda, a_norm_g, b_sink,
           c_gate_w2, c_gate_b, c_norm_g, ffn_w_gate, ffn_w_up, ffn_w_down, moe_router, moe_w_gate,
           moe_w_up, moe_w_down, final_g):
    nb, seq, d = x.shape
    nctx = ctx.shape[1]
    depth = w_in.shape[0]
    n_lat = nb * seq
    tm = nb * nctx
    assert seq % tm == 0 and nb < 8
    tm_r = tm // 2
    n_lat_tiles = n_lat // tm

    def modrow(i):
        return jnp.where(i < n_lat_tiles, i // (seq // tm), nb)

    xs = jnp.concatenate([x.reshape(n_lat, d), ctx.reshape(nb * nctx, d)], axis=0)
    cc = jnp.zeros((8, d), F32).at[:nb].set(c).at[nb].set(c_ctx)
    mod_all = _modulation(cc, ada_w, ada_b).reshape(depth, 8, 1, 6 * d)
    tables = _rope_tables(seq, tm_r)

    for layer in range(depth):
        lam_init = 0.8 - 0.6 * math.exp(-0.3 * layer)
        mod = mod_all[layer]
        w_pad = jnp.concatenate([w_in[layer], jnp.zeros((d, 1), F32)], axis=1)
        w_rope = jnp.take(w_pad, _ROPE_COLS, axis=1).astype(BF16)
        w_plain = jnp.take(w_pad, _PLAIN_COLS, axis=1).astype(BF16)
        w_vt = jnp.take(w_pad, _VT_COLS, axis=1).T.astype(BF16)

        aq, ak, bq, bk = _normproj(xs, norm1_g[layer], mod, 1, 0, w_rope, ROPE_WIDTHS, tm_r,
                                   n_lat // tm_r, seq // tm_r, nb, tables)
        cq, ck, cv, cr, cg, avt, bvt, cvt = _normproj(xs, norm1_g[layer], mod, 1, 0, w_plain, PLAIN_WIDTHS, tm,
                                                      n_lat_tiles, seq // tm, nb, wt=w_vt, vt_widths=VT_WIDTHS)

        a_lat = _diff_attn(aq, ak, avt, a_lambda[layer], a_norm_g[layer], lam_init, nb, seq, nctx, True)
        a_ctx = _diff_attn(aq, ak, avt, a_lambda[layer], a_norm_g[layer], lam_init, nb, seq, nctx, False)
        sink_row = jnp.zeros((1, LANES), F32).at[0, :B_HEADS].set(b_sink[layer] * LOG2E)
        b_lat = _win_attn(bq, bk, bvt, sink_row, nb, seq, nctx, True)
        b_ctx = _win_attn(bq, bk, bvt, sink_row, nb, seq, nctx, False)

        w2 = c_gate_w2[layer]
        w2p = [jnp.zeros((LANES, C_HEADS * LANES), F32).at[dd * C_RANK:(dd + 1) * C_RANK].set(
            _pad_heads(w2[dd], LANES, C_DK)).astype(BF16) for dd in range(2)]
        bgp = [_pad_heads(c_gate_b[layer, dd], LANES, C_DK).reshape(1, -1) for dd in range(2)]
        ng = _pad_heads(jnp.tile(c_norm_g[layer], C_HEADS), LANES, C_DV).reshape(1, -1)
        o_f = _gla_dir(cq, ck, cv, cvt, cg, w2p[0], bgp[0], nb, seq, nctx, False)
        g_out = _gla_dir(cq, ck, cv, cvt, cg, w2p[1], bgp[1], nb, seq, nctx, True, o_f, cr, ng)

        a_all = jnp.concatenate([a_lat, a_ctx], axis=0)
        b_all = jnp.concatenate([b_lat, b_ctx], axis=0)
        wo = w_out[layer]
        wa = wo[:256].astype(BF16)
        wb = wo[256:640].astype(BF16)
        wc = jnp.pad(wo[640:].reshape(C_HEADS, C_DV, d), ((0, 0), (0, LANES - C_DV), (0, 0))).reshape(
            C_HEADS * LANES, d).astype(BF16)
        xs = _outproj(xs, a_all, b_all, g_out, wa, wb, wc, mod, tm, modrow)

        j = layer // 2
        if layer % 2 == 0:
            xs = _ffn(xs, norm2_g[layer], mod, ffn_w_gate[j], ffn_w_up[j], ffn_w_down[j], tm, 256, modrow)
        else:
            router_p = jnp.pad(moe_router[j], ((0, 0), (0, LANES - N_EXPERTS)))
            xs = _moe(xs, norm2_g[layer], mod, router_p, moe_w_gate, moe_w_up, moe_w_down, j,
                      tm, 512, modrow)

    return _final_norm(xs, final_g, n_lat, tm).reshape(nb, seq, d)
```

```python
import functools
import math

import numpy as np
import jax
import jax.numpy as jnp
from jax import lax
from jax.experimental import pallas as pl
from jax.experimental.pallas import tpu as pltpu

F32 = jnp.float32
BF16 = jnp.bfloat16

EPS = 1e-6
ROPE_BASE = 10000.0
GRID_W = 64
LANES = 128
LOG2E = math.log2(math.e)
NEG = -1e30

A_HEADS, A_QK, A_V = 4, 32, 64
B_HEADS, B_KV, B_HD, WINDOW, BLOCK = 6, 2, 64, 128, 128
C_HEADS, C_DK, C_DV, C_RANK, C_GATE_NORM, C_CHUNK = 4, 48, 96, 16, 16.0, 64
N_EXPERTS = 8
IN_SIZES = (256, 256, 256, 384, 128, 128, 192, 192, 384, 384, 32)
IN_W = sum(IN_SIZES)

ROPE_WIDTHS = (256, 256, 768, 128)
PLAIN_WIDTHS = (512, 512, 512, 512, 128)
VT_WIDTHS = (256, 128, 512)
ROPE_W = sum(ROPE_WIDTHS)
ROPE_QUARTERS = (A_QK // 4, A_QK // 4, B_HD // 4, B_HD // 4)
GLA_GROUP = 256
VMEM_LIMIT = 48 * 1024 * 1024


def _cparams(sem):
    return pltpu.CompilerParams(dimension_semantics=sem, vmem_limit_bytes=VMEM_LIMIT)


def _column_maps():
    off = np.concatenate([[0], np.cumsum(IN_SIZES)])
    aq0, ak0, av0, bq0, bk0, bv0, cq0, ck0, cv0, cr0, cg0 = [int(v) for v in off[:11]]
    zero = IN_W

    main, dim, dd, scale = [], [], [], []
    a_scale = A_QK ** -0.5 * LOG2E
    b_scale = B_HD ** -0.5 * LOG2E
    for base, sc in ((aq0, a_scale), (ak0, 1.0)):
        for j in range(256):
            main.append(base + j)
            dim.append(32); dd.append(j % 32); scale.append(sc)
    for t in range(B_HEADS):
        g = t // (B_HEADS // B_KV)
        for lane in range(LANES):
            d = lane % 64
            main.append(bq0 + t * 64 + d if lane // 64 == g else zero)
            dim.append(64); dd.append(d); scale.append(b_scale)
    for j in range(128):
        main.append(bk0 + j)
        dim.append(64); dd.append(j % 64); scale.append(1.0)

    plain = []
    for base in (cq0, ck0):
        for h in range(C_HEADS):
            plain += [base + h * C_DK + d if d < C_DK else zero for d in range(LANES)]
    for base in (cv0, cr0):
        for h in range(C_HEADS):
            plain += [base + h * C_DV + d if d < C_DV else zero for d in range(LANES)]
    plain += [cg0 + d if d < 2 * C_RANK else zero for d in range(LANES)]
    vt = list(range(av0, av0 + 256)) + list(range(bv0, bv0 + 128))
    for h in range(C_HEADS):
        vt += [cv0 + h * C_DV + d if d < C_DV else zero for d in range(LANES)]
    return (np.array(main, np.int32), np.array(plain, np.int32), np.array(vt, np.int32),
            np.array(dim), np.array(dd), np.array(scale, np.float32))


_ROPE_COLS, _PLAIN_COLS, _VT_COLS, _R_DIM, _R_D, _R_SCALE = _column_maps()


def _rope_tables(seq, pad_rows):
    sec_start = np.cumsum((0,) + ROPE_WIDTHS[:-1])
    cols = np.concatenate([np.arange(s, s + LANES) for s in sec_start for _ in range(2)])
    is_sin = jnp.asarray(np.tile(np.repeat([False, True], LANES), len(ROPE_WIDTHS)))[None, :]
    r_dim, r_d, r_scale = _R_DIM[cols], _R_D[cols], _R_SCALE[cols]
    quarter = r_dim // 4
    half = r_dim // 2
    is_col = jnp.asarray((r_d % r_dim) >= half)
    ddh = r_d % half
    first = jnp.asarray(ddh < quarter)[None, :]
    f = (ddh % quarter).astype(np.float32)
    inv = jnp.asarray(ROPE_BASE, F32) ** (-jnp.asarray(f) / jnp.asarray(quarter.astype(np.float32)))
    scale = jnp.asarray(r_scale)[None, :]

    def trig(n):
        ang = jnp.arange(n, dtype=F32)[:, None] * inv[None, :]
        return jnp.where(is_sin, jnp.where(first, -jnp.sin(ang), jnp.sin(ang)), jnp.cos(ang)) * scale

    t_row = trig(seq // GRID_W)
    t_col = trig(GRID_W)
    tab = jnp.where(is_col[None, None, :], t_col[None, :, :], t_row[:, None, :]).reshape(seq, -1)
    ident = jnp.broadcast_to(jnp.where(is_sin, 0.0, scale), (pad_rows, tab.shape[1]))
    return jnp.concatenate([tab, ident], axis=0)


def _mod_kernel(c_ref, w_ref, b_ref, o_ref):
    c = c_ref[...]
    s = c * (1.0 / (1.0 + jnp.exp(-c)))
    o_ref[0] = jnp.dot(s, w_ref[0], precision=lax.Precision.HIGHEST,
                       preferred_element_type=F32) + b_ref[0]


def _modulation(cc, ada_w, ada_b):
    depth, d, n = ada_w.shape
    tn = n // 4
    return pl.pallas_call(
        _mod_kernel,
        grid=(depth, n // tn),
        in_specs=[pl.BlockSpec((8, d), lambda l, j: (0, 0)),
                  pl.BlockSpec((1, d, tn), lambda l, j: (l, 0, j)),
                  pl.BlockSpec((1, 1, tn), lambda l, j: (l, 0, j))],
        out_specs=pl.BlockSpec((1, 8, tn), lambda l, j: (l, 0, j)),
        out_shape=jax.ShapeDtypeStruct((depth, 8, n), F32),
        compiler_params=_cparams(("arbitrary", "arbitrary")),
        name="adaln_mod",
    )(cc, ada_w, ada_b.reshape(depth, 1, n))


def _norm_mod(x, g, sc, sh):
    ms = jnp.mean(x * x, axis=-1, keepdims=True)
    return (x * lax.rsqrt(ms + EPS) * g) * (1.0 + sc) + sh


def _normproj_kernel(x_ref, g_ref, sc_ref, sh_ref, w_ref, *rest, rope, widths, vt_widths):
    if rope:
        t_ref = rest[0]
        rest = rest[1:]
    if vt_widths:
        wt_ref = rest[0]
        rest = rest[1:]
    outs = rest[:len(widths)]
    vt_outs = rest[len(widths):]
    h = _norm_mod(x_ref[...], g_ref[...], sc_ref[0], sh_ref[0]).astype(BF16)
    acc = jnp.dot(h, w_ref[...], preferred_element_type=F32)
    lane = lax.broadcasted_iota(jnp.int32, (1, LANES), 1)
    off = 0
    for sec, (o_ref, w) in enumerate(zip(outs, widths)):
        if rope:
            cos = t_ref[:, (2 * sec) * LANES:(2 * sec + 1) * LANES]
            sin = t_ref[:, (2 * sec + 1) * LANES:(2 * sec + 2) * LANES]
            qd = rope[sec]
            first = (lane & (2 * qd - 1)) < qd
            for j in range(w // LANES):
                x = acc[:, off + j * LANES:off + (j + 1) * LANES]
                partner = jnp.where(first, pltpu.roll(x, LANES - qd, axis=1), pltpu.roll(x, qd, axis=1))
                o_ref[:, j * LANES:(j + 1) * LANES] = (x * cos + partner * sin).astype(o_ref.dtype)
        else:
            o_ref[...] = acc[:, off:off + w].astype(o_ref.dtype)
        off += w
    off = 0
    for o_ref, w in zip(vt_outs, vt_widths):
        o_ref[...] = lax.dot_general(wt_ref[off:off + w, :], h, (((1,), (1,)), ((), ())),
                                     preferred_element_type=F32).astype(o_ref.dtype)
        off += w


def _normproj(xs, g, mod, sc_chunk, sh_chunk, w, widths, tm, n_lat_tiles, tiles_per_batch, nb, tables=None,
              wt=None, vt_widths=()):
    t, d = xs.shape
    rope = ROPE_QUARTERS if tables is not None else ()

    def modrow(i):
        return jnp.where(i < n_lat_tiles, i // tiles_per_batch, nb)

    in_specs = [pl.BlockSpec((tm, d), lambda i: (i, 0)),
                pl.BlockSpec((1, d), lambda i: (0, 0)),
                pl.BlockSpec((1, 1, d), lambda i: (modrow(i), 0, sc_chunk)),
                pl.BlockSpec((1, 1, d), lambda i: (modrow(i), 0, sh_chunk)),
                pl.BlockSpec(w.shape, lambda i: (0, 0))]
    args = [xs, g.reshape(1, d), mod, mod, w]
    if rope:
        def tabrow(i):
            return jnp.where(i < n_lat_tiles, i % tiles_per_batch, tiles_per_batch)
        in_specs += [pl.BlockSpec((tm, tables.shape[1]), lambda i: (tabrow(i), 0))]
        args += [tables]
    if vt_widths:
        in_specs += [pl.BlockSpec(wt.shape, lambda i: (0, 0))]
        args += [wt]
    return pl.pallas_call(
        functools.partial(_normproj_kernel, rope=rope, widths=widths, vt_widths=vt_widths),
        grid=(t // tm,),
        in_specs=in_specs,
        out_specs=([pl.BlockSpec((tm, wd), lambda i: (i, 0)) for wd in widths]
                   + [pl.BlockSpec((wd, tm), lambda i: (0, i)) for wd in vt_widths]),
        out_shape=([jax.ShapeDtypeStruct((t, wd), BF16) for wd in widths]
                   + [jax.ShapeDtypeStruct((wd, t), BF16) for wd in vt_widths]),
        compiler_params=_cparams(("parallel",)),
        name="normproj_rope" if rope else "normproj_plain",
    )(*args)


def _diff_attn_kernel(*refs, lam_init, has_lat, tk):
    if has_lat:
        q_ref, kc_ref, vtc_ref, kl_ref, vtl_ref, lam_ref, g_ref, o_ref, m_scr, acc_scr, mc_scr, s_scr = refs
    else:
        q_ref, kc_ref, vtc_ref, lam_ref, g_ref, o_ref, m_scr, acc_scr, mc_scr, s_scr = refs
    q = q_ref[...]
    lane = lax.broadcasted_iota(jnp.int32, (1, LANES), 1)
    nt = (((1,), (1,)), ((), ()))
    qms = [jnp.where((lane >= i * A_QK) & (lane < (i + 1) * A_QK), q, jnp.zeros_like(q)) for i in range(4)]

    sub = 512

    def scores(k, slot, nk):
        for i in range(4):
            s = lax.dot_general(k, qms[i], nt, preferred_element_type=F32)
            s_scr[slot, i, 0:nk, :] = s
            mc_scr[slot, i] = jnp.max(s, axis=0, keepdims=True)

    def consume(vt_of, slot, nk, first):
        for i in range(4):
            m_cur = mc_scr[slot, i]
            if first:
                m_new = m_cur
            else:
                m_run = m_scr[i]
                m_new = jnp.maximum(m_run, m_cur)
            pv = None
            for t in range(nk // min(sub, nk)):
                w = min(sub, nk)
                p = jnp.exp2(s_scr[slot, i, t * w:(t + 1) * w, :] - m_new).astype(BF16)
                d = jnp.dot(vt_of(i // 2, t, w), p, preferred_element_type=F32)
                pv = d if pv is None else pv + d
            if first:
                acc_scr[i] = pv
            else:
                acc_scr[i] = jnp.exp2(m_run - m_new) * acc_scr[i] + pv
            m_scr[i] = m_new

    nctx = kc_ref.shape[0]
    scores(kc_ref[...], 1, nctx)

    def with_ones(vt):
        return jnp.concatenate([vt, jnp.ones((8, vt.shape[1]), vt.dtype)], axis=0)

    def vt_ctx(hh, t, w):
        return with_ones(vtc_ref[hh * A_V:(hh + 1) * A_V, t * w:(t + 1) * w])

    if not has_lat:
        consume(vt_ctx, 1, nctx, True)
    else:
        n_chunks = kl_ref.shape[0] // tk

        def k_lat(c):
            return kl_ref[pl.ds(pl.multiple_of(c * tk, tk), tk), :]

        def vt_lat(c):
            def get(hh, t, w):
                return with_ones(vtl_ref[hh * A_V:(hh + 1) * A_V, pl.ds(pl.multiple_of(c * tk + t * w, w), w)])
            return get

        scores(k_lat(0), 0, tk)
        consume(vt_ctx, 1, nctx, True)

        def body(j, carry):
            c = 2 * j
            scores(k_lat(c + 1), 1, tk)
            consume(vt_lat(c), 0, tk, False)
            scores(k_lat(c + 2), 0, tk)
            consume(vt_lat(c + 1), 1, tk, False)
            return carry
        pairs = (n_chunks - 1) // 2
        lax.fori_loop(0, pairs, body, 0)
        c_last = 2 * pairs
        if c_last + 1 < n_chunks:
            scores(k_lat(c_last + 1), 1, tk)
            consume(vt_lat(c_last), 0, tk, False)
            consume(vt_lat(c_last + 1), 1, tk, False)
        else:
            consume(vt_lat(c_last), 0, tk, False)

    lp = lam_ref[...]
    lam = (jnp.exp(jnp.sum(lp[0:1] * lp[1:2], axis=1, keepdims=True))
           - jnp.exp(jnp.sum(lp[2:3] * lp[3:4], axis=1, keepdims=True)) + lam_init)
    heads = []
    for hh in range(2):
        maps = []
        for m in range(2):
            acc = acc_scr[hh * 2 + m]
            maps.append(acc[0:A_V] / acc[A_V:A_V + 1])
        oh = maps[0] - lam * maps[1]
        ms = jnp.mean(oh * oh, axis=0, keepdims=True)
        heads.append(oh * lax.rsqrt(ms + EPS) * g_ref[...] * (1.0 - lam_init))
    o_ref[...] = jnp.concatenate(heads, axis=0).T.astype(o_ref.dtype)


def _diff_attn(aq, ak, avt, lam_p, norm_g, lam_init, nb, seq, ctx, latent):
    tq = 256
    tk = min(1024, seq)
    kern = functools.partial(_diff_attn_kernel, lam_init=lam_init, has_lat=latent, tk=tk)
    ctx_blk0 = nb * seq // ctx
    vtc_spec = pl.BlockSpec((2 * A_V, ctx), lambda b, p, i: (p, ctx_blk0 + b))
    kc_spec = pl.BlockSpec((ctx, LANES), lambda b, p, i: (ctx_blk0 + b, p))
    par_specs = [pl.BlockSpec((4, A_QK), lambda b, p, i: (0, 0)),
                 pl.BlockSpec((A_V, 1), lambda b, p, i: (0, 0))]
    if latent:
        nq = seq // tq
        in_specs = [pl.BlockSpec((tq, LANES), lambda b, p, i: (b * nq + i, p)), kc_spec, vtc_spec,
                    pl.BlockSpec((seq, LANES), lambda b, p, i: (b, p)),
                    pl.BlockSpec((2 * A_V, seq), lambda b, p, i: (p, b))] + par_specs
        args = (aq, ak, avt, ak, avt, lam_p, norm_g.reshape(A_V, 1))
        rows = nb * seq
    else:
        nq = ctx // tq
        q_blk0 = nb * seq // tq
        in_specs = [pl.BlockSpec((tq, LANES), lambda b, p, i: (q_blk0 + b * nq + i, p)),
                    kc_spec, vtc_spec] + par_specs
        args = (aq, ak, avt, lam_p, norm_g.reshape(A_V, 1))
        rows = nb * ctx
    return pl.pallas_call(
        kern,
        grid=(nb, 2, nq),
        in_specs=in_specs,
        out_specs=pl.BlockSpec((tq, LANES), lambda b, p, i: (b * nq + i, p)),
        out_shape=jax.ShapeDtypeStruct((rows, 2 * LANES), BF16),
        scratch_shapes=[pltpu.VMEM((4, 1, tq), F32), pltpu.VMEM((4, A_V + 8, tq), F32),
                        pltpu.VMEM((2, 4, 1, tq), F32),
                        pltpu.VMEM((2, 4, tk if latent else ctx, tq), F32)],
        compiler_params=_cparams(("parallel", "parallel", "arbitrary")),
        name="diff_attn_lat" if latent else "diff_attn_ctx",
    )(*args)


def _win_attn_kernel(*refs, with_window, seq, ctx):
    if with_window:
        q_ref, kc_ref, vtc_ref, kp_ref, kn0_ref, kn_ref, vtp_ref, vt0_ref, vtn_ref, sink_ref, o_ref = refs
        n = pl.program_id(1)
        k_all = jnp.concatenate([kc_ref[...], kp_ref[...], kn0_ref[...], kn_ref[...]], axis=0)
        vt_all = jnp.concatenate([vtc_ref[...], vtp_ref[...], vt0_ref[...], vtn_ref[...]], axis=1)
        nk = ctx + 3 * BLOCK
        r = lax.broadcasted_iota(jnp.int32, (nk, BLOCK), 0)
        c = lax.broadcasted_iota(jnp.int32, (nk, BLOCK), 1)
        krel = r - (ctx + BLOCK)
        in_win = ((jnp.abs(c - krel) <= WINDOW) & (krel >= -n * BLOCK) & (krel < seq - n * BLOCK))
        is_ctx = r < ctx
    else:
        q_ref, kc_ref, vtc_ref, sink_ref, o_ref = refs
        k_all = kc_ref[...]
        vt_all = vtc_ref[...]
    lane = lax.broadcasted_iota(jnp.int32, (1, LANES), 1)
    sink_row = sink_ref[...]
    nt = (((1,), (1,)), ((), ()))
    rep = B_HEADS // B_KV
    outs = []
    for j in range(B_HEADS):
        g = j // rep
        s = lax.dot_general(k_all, q_ref[:, j * LANES:(j + 1) * LANES], nt, preferred_element_type=F32)
        if with_window:
            s = jnp.where(is_ctx, s, jnp.where(in_win, s, NEG))
        sk = jnp.max(jnp.where(lane == j, sink_row, NEG), axis=1, keepdims=True)
        m = jnp.maximum(jnp.max(s, axis=0, keepdims=True), sk)
        p = jnp.exp2(s - m)
        l = jnp.sum(p, axis=0, keepdims=True) + jnp.exp2(sk - m)
        ot = jnp.dot(vt_all, p.astype(BF16), preferred_element_type=F32)
        outs.append(ot[g * B_HD:(g + 1) * B_HD] / l)
    for t in range(B_HEADS // 2):
        pair = jnp.concatenate([outs[2 * t], outs[2 * t + 1]], axis=0)
        o_ref[:, t * LANES:(t + 1) * LANES] = pair.T.astype(o_ref.dtype)


def _win_attn(bq, bk, bvt, sink_row, nb, seq, ctx, latent):
    nblk = seq // BLOCK
    ctx_blk0 = nb * seq // ctx
    kern = functools.partial(_win_attn_kernel, with_window=latent, seq=seq, ctx=ctx)
    kc_spec = pl.BlockSpec((ctx, LANES), lambda b, n: (ctx_blk0 + b, 0))
    vtc_spec = pl.BlockSpec((LANES, ctx), lambda b, n: (0, ctx_blk0 + b))
    sink_spec = pl.BlockSpec((1, LANES), lambda b, n: (0, 0))
    qw = B_HEADS * LANES
    if latent:
        def prev(n):
            return jnp.maximum(n - 1, 0)

        def nxt(n):
            return jnp.minimum(n + 1, nblk - 1)

        in_specs = [pl.BlockSpec((BLOCK, qw), lambda b, n: (b * nblk + n, 0)), kc_spec, vtc_spec,
                    pl.BlockSpec((BLOCK, LANES), lambda b, n: (b * nblk + prev(n), 0)),
                    pl.BlockSpec((BLOCK, LANES), lambda b, n: (b * nblk + n, 0)),
                    pl.BlockSpec((BLOCK, LANES), lambda b, n: (b * nblk + nxt(n), 0)),
                    pl.BlockSpec((LANES, BLOCK), lambda b, n: (0, b * nblk + prev(n))),
                    pl.BlockSpec((LANES, BLOCK), lambda b, n: (0, b * nblk + n)),
                    pl.BlockSpec((LANES, BLOCK), lambda b, n: (0, b * nblk + nxt(n))),
                    sink_spec]
        args = (bq, bk, bvt, bk, bk, bk, bvt, bvt, bvt, sink_row)
        grid = (nb, nblk)
        rows = nb * seq
        nq = nblk
    else:
        nq = ctx // BLOCK
        q_blk0 = nb * seq // BLOCK
        in_specs = [pl.BlockSpec((BLOCK, qw), lambda b, n: (q_blk0 + b * nq + n, 0)),
                    kc_spec, vtc_spec, sink_spec]
        args = (bq, bk, bvt, sink_row)
        grid = (nb, nq)
        rows = nb * ctx
    return pl.pallas_call(
        kern,
        grid=grid,
        in_specs=in_specs,
        out_specs=pl.BlockSpec((BLOCK, B_HEADS * B_HD), lambda b, n: (b * nq + n, 0)),
        out_shape=jax.ShapeDtypeStruct((rows, B_HEADS * B_HD), BF16),
        compiler_params=_cparams(("parallel", "arbitrary")),
        name="win_attn_lat" if latent else "win_attn_ctx",
    )(*args)


def _split3(x):
    hi = x.astype(BF16)
    r1 = x - hi.astype(F32)
    mid = r1.astype(BF16)
    lo = (r1 - mid.astype(F32)).astype(BF16)
    return hi, mid, lo


def _gla_kernel(*refs, reverse, final, n_sub):
    if final:
        (q_ref, k_ref, v_ref, vt_ref, gl_ref, w2_ref, bg_ref, sin_ref, of_ref, r_ref, ng_ref,
         o_ref, sout_ref, st_ref, qd_scr, ke_scr, dec_scr, o_scr) = refs
    else:
        (q_ref, k_ref, v_ref, vt_ref, gl_ref, w2_ref, bg_ref, sin_ref,
         o_ref, sout_ref, st_ref, qd_scr, ke_scr, dec_scr, o_scr) = refs
    gsz = GLA_GROUP
    nch = gsz // C_CHUNK
    step = pl.program_id(1)

    @pl.when(step == 0)
    def _():
        st_ref[...] = sin_ref[0]

    r = lax.broadcasted_iota(jnp.int32, (gsz, gsz), 0)
    c = lax.broadcasted_iota(jnp.int32, (gsz, gsz), 1)
    same = (r // C_CHUNK) == (c // C_CHUNK)
    tri = same & ((c >= r) if reverse else (c <= r))
    tri_b = jnp.where(tri, 1.0, 0.0).astype(BF16)
    edge = 0 if reverse else C_CHUNK - 1
    nt = (((1,), (1,)), ((), ()))

    for sg in range(n_sub):
        rs = slice(sg * gsz, (sg + 1) * gsz)
        q = q_ref[rs, :].astype(F32)
        k = k_ref[rs, :].astype(F32)
        pre = jnp.dot(gl_ref[rs, :], w2_ref[...], preferred_element_type=F32) + bg_ref[...]
        la = (jnp.minimum(pre, 0.0) - jnp.log(1.0 + jnp.exp(-jnp.abs(pre)))) * (1.0 / C_GATE_NORM)
        parts = _split3(la)
        bcum = sum(jnp.dot(tri_b, p, preferred_element_type=F32) for p in parts)
        btot = jnp.concatenate(
            [jnp.broadcast_to(bcum[ci * C_CHUNK + edge:ci * C_CHUNK + edge + 1], (C_CHUNK, bcum.shape[1]))
             for ci in range(nch)], axis=0)
        qd = (q * (jnp.exp(bcum) * (C_DK ** -0.5))).astype(BF16)
        ki = (k * jnp.exp(-bcum)).astype(BF16)
        qd_scr[rs, :] = qd
        ke_scr[rs, :] = (k * jnp.exp(btot - bcum)).astype(BF16)
        for ci in range(nch):
            dec_scr[sg * nch + ci:sg * nch + ci + 1, :] = jnp.exp(bcum[ci * C_CHUNK + edge:ci * C_CHUNK + edge + 1])
        for h in range(C_HEADS):
            sl = slice(h * LANES, (h + 1) * LANES)
            att = lax.dot_general(qd[:, sl], ki[:, sl], nt, preferred_element_type=F32)
            att = jnp.where(tri, att, 0.0).astype(BF16)
            o_scr[rs, sl] = jnp.dot(att, v_ref[rs, sl], preferred_element_type=F32)

    rowid = lax.broadcasted_iota(jnp.int32, (gsz, LANES), 0) // C_CHUNK
    sub_order = list(range(n_sub))[::-1] if reverse else list(range(n_sub))
    chunk_order = list(range(nch))[::-1] if reverse else list(range(nch))
    for h in range(C_HEADS):
        sl = slice(h * LANES, (h + 1) * LANES)
        st = st_ref[h]
        for sg in sub_order:
            r0 = sg * gsz
            ke = ke_scr[r0:r0 + gsz, sl]
            vth = vt_ref[sl, r0:r0 + gsz]
            for ci in chunk_order:
                rows = slice(r0 + ci * C_CHUNK, r0 + (ci + 1) * C_CHUNK)
                o_scr[rows, sl] += lax.dot_general(qd_scr[rows, sl], st.astype(BF16), nt,
                                                   preferred_element_type=F32)
                ke_c = jnp.where(rowid == ci, ke, jnp.zeros_like(ke))
                upd = jnp.dot(vth, ke_c, preferred_element_type=F32)
                st = dec_scr[sg * nch + ci:sg * nch + ci + 1, sl] * st + upd
        st_ref[h] = st

    if not final:
        o_ref[...] = o_scr[...]
    else:
        for h in range(C_HEADS):
            sl = slice(h * LANES, (h + 1) * LANES)
            o = o_scr[:, sl] + of_ref[:, sl]
            ms = jnp.sum(o * o, axis=1, keepdims=True) * (1.0 / C_DV)
            y = o * lax.rsqrt(ms + EPS) * ng_ref[:, sl]
            rr = r_ref[:, sl].astype(F32)
            o_ref[:, sl] = (y * (rr * (1.0 / (1.0 + jnp.exp(-rr))))).astype(o_ref.dtype)

    @pl.when(step == pl.num_programs(1) - 1)
    def _():
        sout_ref[0] = st_ref[...]


def _gla_dir(cq, ck, cv, cvt, cg, w2p, bgp, st_in, nb, seq, ctx, reverse, latent, fwd_out=None, cr=None,
             ng=None):
    gsz = GLA_GROUP
    assert ctx == gsz
    final = fwd_out is not None
    w = C_HEADS * LANES
    if latent:
        n_sub = 4
        rows = n_sub * gsz
        nt_ = seq // rows

        def blk(b, i):
            return b * nt_ + ((nt_ - 1 - i) if reverse else i)
        out_blk = blk
        grid = (nb, nt_)
        n_out = nb * seq
    else:
        n_sub = 1
        rows = gsz
        blk0 = nb * seq // gsz

        def blk(b, i):
            return blk0 + b

        def out_blk(b, i):
            return b
        grid = (nb, 1)
        n_out = nb * ctx
    row_spec = pl.BlockSpec((rows, w), lambda b, i: (blk(b, i), 0))
    out_spec = pl.BlockSpec((rows, w), lambda b, i: (out_blk(b, i), 0))
    st_spec = pl.BlockSpec((1, C_HEADS, LANES, LANES), lambda b, i: (b, 0, 0, 0))
    in_specs = [row_spec, row_spec, row_spec,
                pl.BlockSpec((w, rows), lambda b, i: (0, blk(b, i))),
                pl.BlockSpec((rows, LANES), lambda b, i: (blk(b, i), 0)),
                pl.BlockSpec((LANES, w), lambda b, i: (0, 0)),
                pl.BlockSpec((1, w), lambda b, i: (0, 0)),
                st_spec]
    args = [cq, ck, cv, cvt, cg, w2p, bgp, st_in]
    if final:
        in_specs += [out_spec, row_spec, pl.BlockSpec((1, w), lambda b, i: (0, 0))]
        args += [fwd_out, cr, ng]
    name = ("gla_bwd" if reverse else "gla_fwd") + ("_lat" if latent else "_ctx")
    return pl.pallas_call(
        functools.partial(_gla_kernel, reverse=reverse, final=final, n_sub=n_sub),
        grid=grid,
        in_specs=in_specs,
        out_specs=[out_spec, st_spec],
        out_shape=[jax.ShapeDtypeStruct((n_out, w), BF16 if final else F32),
                   jax.ShapeDtypeStruct(st_in.shape, F32)],
        scratch_shapes=[pltpu.VMEM((C_HEADS, LANES, LANES), F32),
                        pltpu.VMEM((rows, w), BF16), pltpu.VMEM((rows, w), BF16),
                        pltpu.VMEM((max(8, n_sub * (gsz // C_CHUNK)), w), F32),
                        pltpu.VMEM((rows, w), F32)],
        compiler_params=_cparams(("parallel", "arbitrary")),
        name=name,
    )(*args)


def _outproj_kernel(x_ref, a_ref, b_ref, c_ref, wa_ref, wb_ref, wc_ref, g1_ref, o_ref):
    y = jnp.dot(a_ref[...], wa_ref[...], preferred_element_type=F32)
    y += jnp.dot(b_ref[...], wb_ref[...], preferred_element_type=F32)
    y += jnp.dot(c_ref[...], wc_ref[...], preferred_element_type=F32)
    o_ref[...] = x_ref[...] + g1_ref[0] * y


def _outproj(xs, a, b, c, wa, wb, wc, mod, tm, modrow):
    t, d = xs.shape
    return pl.pallas_call(
        _outproj_kernel,
        grid=(t // tm,),
        in_specs=[pl.BlockSpec((tm, d), lambda i: (i, 0)),
                  pl.BlockSpec((tm, a.shape[1]), lambda i: (i, 0)),
                  pl.BlockSpec((tm, b.shape[1]), lambda i: (i, 0)),
                  pl.BlockSpec((tm, c.shape[1]), lambda i: (i, 0)),
                  pl.BlockSpec(wa.shape, lambda i: (0, 0)),
                  pl.BlockSpec(wb.shape, lambda i: (0, 0)),
                  pl.BlockSpec(wc.shape, lambda i: (0, 0)),
                  pl.BlockSpec((1, 1, d), lambda i: (modrow(i), 0, 2))],
        out_specs=pl.BlockSpec((tm, d), lambda i: (i, 0)),
        out_shape=jax.ShapeDtypeStruct((t, d), F32),
        compiler_params=_cparams(("parallel",)),
        name="outproj",
    )(xs, a, b, c, wa, wb, wc, mod)


def _ffn_kernel(x_ref, g_ref, sc_ref, sh_ref, gate_ref, wg_ref, wu_ref, wd_ref, o_ref, h_ref, acc_ref):
    f = pl.program_id(1)

    @pl.when(f == 0)
    def _():
        h_ref[...] = _norm_mod(x_ref[...], g_ref[...], sc_ref[0], sh_ref[0]).astype(BF16)
        acc_ref[...] = jnp.zeros(acc_ref.shape, F32)

    h = h_ref[...]
    a = jnp.dot(h, wg_ref[...].astype(BF16), preferred_element_type=F32)
    u = jnp.dot(h, wu_ref[...].astype(BF16), preferred_element_type=F32)
    act = a * (1.0 / (1.0 + jnp.exp(-a))) * u
    acc_ref[...] += jnp.dot(act.astype(BF16), wd_ref[...].astype(BF16), preferred_element_type=F32)

    @pl.when(f == pl.num_programs(1) - 1)
    def _():
        o_ref[...] = x_ref[...] + gate_ref[0] * acc_ref[...]


def _ffn(xs, g, mod, wg, wu, wd, tm, tf, modrow):
    t, d = xs.shape
    ff = wg.shape[1]
    return pl.pallas_call(
        _ffn_kernel,
        grid=(t // tm, ff // tf),
        in_specs=[pl.BlockSpec((tm, d), lambda i, f: (i, 0)),
                  pl.BlockSpec((1, d), lambda i, f: (0, 0)),
                  pl.BlockSpec((1, 1, d), lambda i, f: (modrow(i), 0, 4)),
                  pl.BlockSpec((1, 1, d), lambda i, f: (modrow(i), 0, 3)),
                  pl.BlockSpec((1, 1, d), lambda i, f: (modrow(i), 0, 5)),
                  pl.BlockSpec((d, tf), lambda i, f: (0, f)),
                  pl.BlockSpec((d, tf), lambda i, f: (0, f)),
                  pl.BlockSpec((tf, d), lambda i, f: (f, 0))],
        out_specs=pl.BlockSpec((tm, d), lambda i, f: (i, 0)),
        out_shape=jax.ShapeDtypeStruct((t, d), F32),
        scratch_shapes=[pltpu.VMEM((tm, d), BF16), pltpu.VMEM((tm, d), F32)],
        compiler_params=_cparams(("parallel", "arbitrary")),
        name="ffn_swiglu",
    )(xs, g.reshape(1, d), mod, mod, mod, wg, wu, wd)


def _router_kernel(x_ref, g_ref, sc_ref, sh_ref, rt_ref, h_ref, r_ref):
    lane = lax.broadcasted_iota(jnp.int32, (1, LANES), 1)
    hf = _norm_mod(x_ref[...], g_ref[...], sc_ref[0], sh_ref[0])
    h_ref[...] = hf
    logits = jnp.dot(hf, rt_ref[...], precision=lax.Precision.HIGHEST, preferred_element_type=F32)
    lanef = lane.astype(F32)
    lg = jnp.where(lane < N_EXPERTS, logits, NEG)
    m1 = jnp.max(lg, axis=1, keepdims=True)
    i1 = jnp.min(jnp.where(lg == m1, lanef, float(LANES)), axis=1, keepdims=True)
    lg2 = jnp.where(lanef == i1, NEG, lg)
    m2 = jnp.max(lg2, axis=1, keepdims=True)
    i2 = jnp.min(jnp.where(lg2 == m2, lanef, float(LANES)), axis=1, keepdims=True)
    e2 = jnp.exp(m2 - m1)
    w1 = 1.0 / (1.0 + e2)
    r_ref[...] = jnp.where(lane == 0, i1, jnp.where(lane == 1, i2, jnp.where(lane == 2, w1,
                           jnp.where(lane == 3, e2 * w1, 0.0))))


def _router(xs, g, mod, router_p, tm, modrow):
    t, d = xs.shape
    return pl.pallas_call(
        _router_kernel,
        grid=(t // tm,),
        in_specs=[pl.BlockSpec((tm, d), lambda i: (i, 0)),
                  pl.BlockSpec((1, d), lambda i: (0, 0)),
                  pl.BlockSpec((1, 1, d), lambda i: (modrow(i), 0, 4)),
                  pl.BlockSpec((1, 1, d), lambda i: (modrow(i), 0, 3)),
                  pl.BlockSpec((d, LANES), lambda i: (0, 0))],
        out_specs=[pl.BlockSpec((tm, d), lambda i: (i, 0)), pl.BlockSpec((tm, LANES), lambda i: (i, 0))],
        out_shape=[jax.ShapeDtypeStruct((t, d), F32), jax.ShapeDtypeStruct((t, LANES), F32)],
        compiler_params=_cparams(("parallel",)),
        name="moe_router",
    )(xs, g.reshape(1, d), mod, mod, router_p)


def _route_plan(rinfo, tg):
    t = rinfo.shape[0]
    n_tiles = -(-2 * t // tg) + N_EXPERTS
    e_flat = jnp.concatenate([rinfo[:, 0], rinfo[:, 1]]).astype(jnp.int32)
    onehot = (e_flat[:, None] == jnp.arange(N_EXPERTS, dtype=jnp.int32)[None, :]).astype(jnp.int32)
    csum = jnp.cumsum(onehot, axis=0)
    rank = jnp.sum(onehot * (csum - 1), axis=1)
    counts = csum[-1]
    padded = (counts + tg - 1) // tg * tg
    ends = jnp.cumsum(padded)
    pos = jnp.sum(onehot * (ends - padded)[None, :], axis=1) + rank
    tok = jnp.tile(jnp.arange(t, dtype=jnp.int32), 2)
    src_tok = jnp.zeros((n_tiles * tg,), jnp.int32).at[pos].set(tok)
    tile_start = jnp.arange(n_tiles, dtype=jnp.int32) * tg
    tile_expert = jnp.minimum(jnp.sum((tile_start[:, None] >= ends[None, :]).astype(jnp.int32), axis=1),
                              N_EXPERTS - 1)
    n_used = (ends[-1] // tg).reshape(1)
    return src_tok, tile_expert, n_used, pos


def _row_copy(src_ref, row, dst_ref, r, sem):
    return pltpu.make_async_copy(src_ref.at[pl.ds(row, 1), :], dst_ref.at[pl.ds(r, 1), :], sem)


def _moe_experts_kernel(te_ref, nu_ref, idx_ref, idxn_ref, x_ref, wg_ref, wu_ref, wd_ref, o_ref,
                        xbuf, h_ref, acc_ref, sems, *, rows_per_step, n_steps):
    i = pl.program_id(0)
    f = pl.program_id(1)
    nf = pl.num_programs(1)
    last = nf - 1
    n_tiles = pl.num_programs(0)
    n_used = nu_ref[0]
    used = i < n_used
    slot = i % 2
    rows = h_ref.shape[0]
    per_step = rows_per_step
    n_dma = per_step * n_steps

    def drain(s):
        def body(r, carry):
            _row_copy(x_ref, 0, xbuf.at[s], r, sems.at[s]).wait()
            return carry
        lax.fori_loop(0, n_dma, body, 0, unroll=n_steps)

    @pl.when((i == 0) & (f == 0))
    def _():
        def body(r, carry):
            _row_copy(x_ref, idx_ref[0, 0, r], xbuf.at[0], r, sems.at[0]).start()
            return carry
        lax.fori_loop(0, n_dma, body, 0, unroll=n_steps)

    @pl.when((i <= n_used) & (f == 0))
    def _():
        drain(slot)

    @pl.when(used & (f == 0))
    def _():
        h_ref[...] = xbuf[slot, 0:rows].astype(BF16)
        acc_ref[...] = jnp.zeros(acc_ref.shape, F32)

    @pl.when(used)
    def _():
        for k in range(per_step):
            r = f * per_step + k
            _row_copy(x_ref, idxn_ref[0, 0, r], xbuf.at[1 - slot], r, sems.at[1 - slot]).start()
        h = h_ref[...]
        a = jnp.dot(h, wg_ref[0, 0].astype(BF16), preferred_element_type=F32)
        u = jnp.dot(h, wu_ref[0, 0].astype(BF16), preferred_element_type=F32)
        act = a * (1.0 / (1.0 + jnp.exp(-a))) * u
        acc_ref[...] += jnp.dot(act.astype(BF16), wd_ref[0, 0].astype(BF16), preferred_element_type=F32)

    @pl.when(used & (f == last))
    def _():
        o_ref[...] = acc_ref[...]

    @pl.when(used & (i == n_tiles - 1) & (f == last))
    def _():
        drain(1 - slot)

    @pl.when(jnp.logical_not(used) & (f == last))
    def _():
        o_ref[...] = jnp.zeros(o_ref.shape, F32)


def _moe_experts(h2, src_tok, tile_expert, n_used, wg, wu, wd, layer, tg, tf):
    d = h2.shape[1]
    n_tiles = src_tok.shape[0] // tg
    ff = wg.shape[3]
    nf = ff // tf
    per_step = -(-tg // nf)
    n_dma = per_step * nf
    idw = -(-n_dma // LANES) * LANES
    ids = jnp.pad(src_tok.reshape(n_tiles, 1, tg), ((0, 0), (0, 0), (0, idw - tg)))
    grid_spec = pltpu.PrefetchScalarGridSpec(
        num_scalar_prefetch=2,
        grid=(n_tiles, nf),
        in_specs=[pl.BlockSpec((1, 1, idw), lambda i, f, te, nu: (i, 0, 0), memory_space=pltpu.SMEM),
                  pl.BlockSpec((1, 1, idw), lambda i, f, te, nu: (jnp.minimum(i + 1, n_tiles - 1), 0, 0),
                               memory_space=pltpu.SMEM),
                  pl.BlockSpec(memory_space=pl.ANY),
                  pl.BlockSpec((1, 1, d, tf), lambda i, f, te, nu: (layer, te[i], 0, f)),
                  pl.BlockSpec((1, 1, d, tf), lambda i, f, te, nu: (layer, te[i], 0, f)),
                  pl.BlockSpec((1, 1, tf, d), lambda i, f, te, nu: (layer, te[i], f, 0))],
        out_specs=pl.BlockSpec((tg, d), lambda i, f, te, nu: (i, 0)),
        scratch_shapes=[pltpu.VMEM((2, -(-n_dma // 8) * 8, d), F32), pltpu.VMEM((tg, d), BF16),
                        pltpu.VMEM((tg, d), F32), pltpu.SemaphoreType.DMA((2,))])
    return pl.pallas_call(
        functools.partial(_moe_experts_kernel, rows_per_step=per_step, n_steps=nf),
        grid_spec=grid_spec,
        out_shape=jax.ShapeDtypeStruct((n_tiles * tg, d), F32),
        compiler_params=_cparams(("arbitrary", "arbitrary")),
        name="moe_experts",
    )(tile_expert, n_used, ids, ids, h2, wg, wu, wd)


def _moe_combine_kernel(i0_ref, i1_ref, n0_ref, n1_ref, y_ref, x_ref, r_ref, gate_ref, o_ref, buf, sems):
    i = pl.program_id(0)
    slot = i % 2
    rows = o_ref.shape[0]

    def fetch(a_ref, b_ref, s):
        def body(r, carry):
            _row_copy(y_ref, a_ref[0, 0, r], buf.at[s, 0], r, sems.at[s]).start()
            _row_copy(y_ref, b_ref[0, 0, r], buf.at[s, 1], r, sems.at[s]).start()
            return carry
        lax.fori_loop(0, rows, body, 0, unroll=8)

    @pl.when(i == 0)
    def _():
        fetch(i0_ref, i1_ref, 0)

    @pl.when(i + 1 < pl.num_programs(0))
    def _():
        fetch(n0_ref, n1_ref, 1 - slot)

    def drain(r, carry):
        _row_copy(y_ref, 0, buf.at[slot, 0], r, sems.at[slot]).wait()
        _row_copy(y_ref, 0, buf.at[slot, 1], r, sems.at[slot]).wait()
        return carry
    lax.fori_loop(0, rows, drain, 0, unroll=8)
    y = r_ref[:, 2:3] * buf[slot, 0] + r_ref[:, 3:4] * buf[slot, 1]
    o_ref[...] = x_ref[...] + gate_ref[0] * y


def _moe_combine(xs, yg, pos, rinfo, mod, tm, modrow):
    t, d = xs.shape
    nt = t // tm
    p3 = pos.reshape(2, nt, 1, tm)

    def cur(i):
        return (i, 0, 0)

    def nxt(i):
        return (jnp.minimum(i + 1, nt - 1), 0, 0)

    return pl.pallas_call(
        _moe_combine_kernel,
        grid=(nt,),
        in_specs=[pl.BlockSpec((1, 1, tm), cur, memory_space=pltpu.SMEM),
                  pl.BlockSpec((1, 1, tm), cur, memory_space=pltpu.SMEM),
                  pl.BlockSpec((1, 1, tm), nxt, memory_space=pltpu.SMEM),
                  pl.BlockSpec((1, 1, tm), nxt, memory_space=pltpu.SMEM),
                  pl.BlockSpec(memory_space=pl.ANY),
                  pl.BlockSpec((tm, d), lambda i: (i, 0)),
                  pl.BlockSpec((tm, LANES), lambda i: (i, 0)),
                  pl.BlockSpec((1, 1, d), lambda i: (modrow(i), 0, 5))],
        out_specs=pl.BlockSpec((tm, d), lambda i: (i, 0)),
        out_shape=jax.ShapeDtypeStruct((t, d), F32),
        scratch_shapes=[pltpu.VMEM((2, 2, tm, d), F32), pltpu.SemaphoreType.DMA((2,))],
        compiler_params=_cparams(("arbitrary",)),
        name="moe_combine",
    )(p3[0], p3[1], p3[0], p3[1], yg, xs, rinfo, mod)


def _moe(xs, g, mod, router_p, wg, wu, wd, layer, tm, tf, modrow):
    tg = tm
    h2, rinfo = _router(xs, g, mod, router_p, tm, modrow)
    src_tok, tile_expert, n_used, pos = _route_plan(rinfo, tg)
    yg = _moe_experts(h2, src_tok, tile_expert, n_used, wg, wu, wd, layer, tg, tf)
    return _moe_combine(xs, yg, pos, rinfo, mod, tm, modrow)


def _final_norm_kernel(x_ref, g_ref, o_ref):
    x = x_ref[...]
    ms = jnp.mean(x * x, axis=-1, keepdims=True)
    o_ref[...] = x * lax.rsqrt(ms + EPS) * g_ref[...]


def _final_norm(xs, g, rows, tm):
    d = xs.shape[1]
    return pl.pallas_call(
        _final_norm_kernel,
        grid=(rows // tm,),
        in_specs=[pl.BlockSpec((tm, d), lambda i: (i, 0)), pl.BlockSpec((1, d), lambda i: (0, 0))],
        out_specs=pl.BlockSpec((tm, d), lambda i: (i, 0)),
        out_shape=jax.ShapeDtypeStruct((rows, d), F32),
        compiler_params=_cparams(("parallel",)),
        name="final_norm",
    )(xs, g.reshape(1, d))


def _pad_heads(v, width, used):
    lead = v.shape[:-1]
    v = v.reshape(lead + (C_HEADS, used))
    v = jnp.pad(v, [(0, 0)] * len(lead) + [(0, 0), (0, width - used)])
    return v.reshape(lead + (C_HEADS * width,))


def kernel(x, c, ctx, c_ctx, norm1_g, norm2_g, ada_w, ada_b, w_in, w_out, a_lambda, a_norm_g, b_sink,
           c_gate_w2, c_gate_b, c_norm_g, ffn_w_gate, ffn_w_up, ffn_w_down, moe_router, moe_w_gate,
           moe_w_up, moe_w_down, final_g):
    nb, seq, d = x.shape
    nctx = ctx.shape[1]
    depth = w_in.shape[0]
    n_lat = nb * seq
    tm = nb * nctx
    assert seq % tm == 0 and nb < 8
    tm_r = tm // 2
    n_lat_tiles = n_lat // tm

    def modrow(i):
        return jnp.where(i < n_lat_tiles, i // (seq // tm), nb)

    xs = jnp.concatenate([x.reshape(n_lat, d), ctx.reshape(nb * nctx, d)], axis=0)
    cc = jnp.zeros((8, d), F32).at[:nb].set(c).at[nb].set(c_ctx)
    mod_all = _modulation(cc, ada_w, ada_b).reshape(depth, 8, 1, 6 * d)
    tables = _rope_tables(seq, tm_r)

    for layer in range(depth):
        lam_init = 0.8 - 0.6 * math.exp(-0.3 * layer)
        mod = mod_all[layer]
        w_pad = jnp.concatenate([w_in[layer], jnp.zeros((d, 1), F32)], axis=1)
        w_rope = jnp.take(w_pad, _ROPE_COLS, axis=1).astype(BF16)
        w_plain = jnp.take(w_pad, _PLAIN_COLS, axis=1).astype(BF16)
        w_vt = jnp.take(w_pad, _VT_COLS, axis=1).T.astype(BF16)

        aq, ak, bq, bk = _normproj(xs, norm1_g[layer], mod, 1, 0, w_rope, ROPE_WIDTHS, tm_r,
                                   n_lat // tm_r, seq // tm_r, nb, tables)
        cq, ck, cv, cr, cg, avt, bvt, cvt = _normproj(xs, norm1_g[layer], mod, 1, 0, w_plain, PLAIN_WIDTHS, tm,
                                                      n_lat_tiles, seq // tm, nb, wt=w_vt, vt_widths=VT_WIDTHS)

        a_lat = _diff_attn(aq, ak, avt, a_lambda[layer], a_norm_g[layer], lam_init, nb, seq, nctx, True)
        a_ctx = _diff_attn(aq, ak, avt, a_lambda[layer], a_norm_g[layer], lam_init, nb, seq, nctx, False)
        sink_row = jnp.zeros((1, LANES), F32).at[0, :B_HEADS].set(b_sink[layer] * LOG2E)
        b_lat = _win_attn(bq, bk, bvt, sink_row, nb, seq, nctx, True)
        b_ctx = _win_attn(bq, bk, bvt, sink_row, nb, seq, nctx, False)

        w2 = c_gate_w2[layer]
        w2p = [jnp.zeros((LANES, C_HEADS * LANES), F32).at[dd * C_RANK:(dd + 1) * C_RANK].set(
            _pad_heads(w2[dd], LANES, C_DK)).astype(BF16) for dd in range(2)]
        bgp = [_pad_heads(c_gate_b[layer, dd], LANES, C_DK).reshape(1, -1) for dd in range(2)]
        ng = _pad_heads(jnp.tile(c_norm_g[layer], C_HEADS), LANES, C_DV).reshape(1, -1)
        st0 = jnp.zeros((nb, C_HEADS, LANES, LANES), F32)
        gla_in = (cq, ck, cv, cvt, cg)
        of_ctx, st_f = _gla_dir(*gla_in, w2p[0], bgp[0], st0, nb, seq, nctx, False, False)
        of_lat, _ = _gla_dir(*gla_in, w2p[0], bgp[0], st_f, nb, seq, nctx, False, True)
        g_ctx, st_b = _gla_dir(*gla_in, w2p[1], bgp[1], st0, nb, seq, nctx, True, False, of_ctx, cr, ng)
        g_lat, _ = _gla_dir(*gla_in, w2p[1], bgp[1], st_b, nb, seq, nctx, True, True, of_lat, cr, ng)
        g_out = jnp.concatenate([g_lat, g_ctx], axis=0)

        a_all = jnp.concatenate([a_lat, a_ctx], axis=0)
        b_all = jnp.concatenate([b_lat, b_ctx], axis=0)
        wo = w_out[layer]
        wa = wo[:256].astype(BF16)
        wb = wo[256:640].astype(BF16)
        wc = jnp.pad(wo[640:].reshape(C_HEADS, C_DV, d), ((0, 0), (0, LANES - C_DV), (0, 0))).reshape(
            C_HEADS * LANES, d).astype(BF16)
        xs = _outproj(xs, a_all, b_all, g_out, wa, wb, wc, mod, tm, modrow)

        j = layer // 2
        if layer % 2 == 0:
            xs = _ffn(xs, norm2_g[layer], mod, ffn_w_gate[j], ffn_w_up[j], ffn_w_down[j], tm, 256, modrow)
        else:
            router_p = jnp.pad(moe_router[j], ((0, 0), (0, LANES - N_EXPERTS)))
            xs = _moe(xs, norm2_g[layer], mod, router_p, moe_w_gate, moe_w_up, moe_w_down, j,
                      tm, 512, modrow)

    return _final_norm(xs, final_g, n_lat, tm).reshape(nb, seq, d)
```

```python
import functools
import math

import numpy as np
import jax
import jax.numpy as jnp
from jax import lax
from jax.experimental import pallas as pl
from jax.experimental.pallas import tpu as pltpu

F32 = jnp.float32
BF16 = jnp.bfloat16

EPS = 1e-6
ROPE_BASE = 10000.0
GRID_W = 64
LANES = 128
LOG2E = math.log2(math.e)
NEG = -1e30

A_HEADS, A_QK, A_V = 4, 32, 64
B_HEADS, B_KV, B_HD, WINDOW, BLOCK = 6, 2, 64, 128, 128
C_HEADS, C_DK, C_DV, C_RANK, C_GATE_NORM, C_CHUNK = 4, 48, 96, 16, 16.0, 64
N_EXPERTS = 8
IN_SIZES = (256, 256, 256, 384, 128, 128, 192, 192, 384, 384, 32)
IN_W = sum(IN_SIZES)

ROPE_WIDTHS = (256, 256, 768, 128)
PLAIN_WIDTHS = (512, 512, 512, 512, 128)
VT_WIDTHS = (256, 128, 512)
ROPE_W = sum(ROPE_WIDTHS)
ROPE_QUARTERS = (A_QK // 4, A_QK // 4, B_HD // 4, B_HD // 4)
GLA_GROUP = 256
VMEM_LIMIT = 48 * 1024 * 1024


def _cparams(sem):
    return pltpu.CompilerParams(dimension_semantics=sem, vmem_limit_bytes=VMEM_LIMIT)


def _column_maps():
    off = np.concatenate([[0], np.cumsum(IN_SIZES)])
    aq0, ak0, av0, bq0, bk0, bv0, cq0, ck0, cv0, cr0, cg0 = [int(v) for v in off[:11]]
    zero = IN_W

    main, dim, dd, scale = [], [], [], []
    a_scale = A_QK ** -0.5 * LOG2E
    b_scale = B_HD ** -0.5 * LOG2E
    for base, sc in ((aq0, a_scale), (ak0, 1.0)):
        for j in range(256):
            main.append(base + j)
            dim.append(32); dd.append(j % 32); scale.append(sc)
    for t in range(B_HEADS):
        g = t // (B_HEADS // B_KV)
        for lane in range(LANES):
            d = lane % 64
            main.append(bq0 + t * 64 + d if lane // 64 == g else zero)
            dim.append(64); dd.append(d); scale.append(b_scale)
    for j in range(128):
        main.append(bk0 + j)
        dim.append(64); dd.append(j % 64); scale.append(1.0)

    plain = []
    for base in (cq0, ck0):
        for h in range(C_HEADS):
            plain += [base + h * C_DK + d if d < C_DK else zero for d in range(LANES)]
    for base in (cv0, cr0):
        for h in range(C_HEADS):
            plain += [base + h * C_DV + d if d < C_DV else zero for d in range(LANES)]
    plain += [cg0 + d if d < 2 * C_RANK else zero for d in range(LANES)]
    vt = list(range(av0, av0 + 256)) + list(range(bv0, bv0 + 128))
    for h in range(C_HEADS):
        vt += [cv0 + h * C_DV + d if d < C_DV else zero for d in range(LANES)]
    return (np.array(main, np.int32), np.array(plain, np.int32), np.array(vt, np.int32),
            np.array(dim), np.array(dd), np.array(scale, np.float32))


_ROPE_COLS, _PLAIN_COLS, _VT_COLS, _R_DIM, _R_D, _R_SCALE = _column_maps()


def _rope_tables(seq, pad_rows):
    sec_start = np.cumsum((0,) + ROPE_WIDTHS[:-1])
    cols = np.concatenate([np.arange(s, s + LANES) for s in sec_start for _ in range(2)])
    is_sin = jnp.asarray(np.tile(np.repeat([False, True], LANES), len(ROPE_WIDTHS)))[None, :]
    r_dim, r_d, r_scale = _R_DIM[cols], _R_D[cols], _R_SCALE[cols]
    quarter = r_dim // 4
    half = r_dim // 2
    is_col = jnp.asarray((r_d % r_dim) >= half)
    ddh = r_d % half
    first = jnp.asarray(ddh < quarter)[None, :]
    f = (ddh % quarter).astype(np.float32)
    inv = jnp.asarray(ROPE_BASE, F32) ** (-jnp.asarray(f) / jnp.asarray(quarter.astype(np.float32)))
    scale = jnp.asarray(r_scale)[None, :]

    def trig(n):
        ang = jnp.arange(n, dtype=F32)[:, None] * inv[None, :]
        return jnp.where(is_sin, jnp.where(first, -jnp.sin(ang), jnp.sin(ang)), jnp.cos(ang)) * scale

    t_row = trig(seq // GRID_W)
    t_col = trig(GRID_W)
    tab = jnp.where(is_col[None, None, :], t_col[None, :, :], t_row[:, None, :]).reshape(seq, -1)
    ident = jnp.broadcast_to(jnp.where(is_sin, 0.0, scale), (pad_rows, tab.shape[1]))
    return jnp.concatenate([tab, ident], axis=0)


def _mod_kernel(c_ref, w_ref, b_ref, o_ref):
    c = c_ref[...]
    s = c * (1.0 / (1.0 + jnp.exp(-c)))
    o_ref[0] = jnp.dot(s, w_ref[0], precision=lax.Precision.HIGHEST,
                       preferred_element_type=F32) + b_ref[0]


def _modulation(cc, ada_w, ada_b):
    depth, d, n = ada_w.shape
    tn = n // 4
    return pl.pallas_call(
        _mod_kernel,
        grid=(depth, n // tn),
        in_specs=[pl.BlockSpec((8, d), lambda l, j: (0, 0)),
                  pl.BlockSpec((1, d, tn), lambda l, j: (l, 0, j)),
                  pl.BlockSpec((1, 1, tn), lambda l, j: (l, 0, j))],
        out_specs=pl.BlockSpec((1, 8, tn), lambda l, j: (l, 0, j)),
        out_shape=jax.ShapeDtypeStruct((depth, 8, n), F32),
        compiler_params=_cparams(("arbitrary", "arbitrary")),
        name="adaln_mod",
    )(cc, ada_w, ada_b.reshape(depth, 1, n))


def _norm_mod(x, g, sc, sh):
    ms = jnp.mean(x * x, axis=-1, keepdims=True)
    return (x * lax.rsqrt(ms + EPS) * g) * (1.0 + sc) + sh


def _normproj_kernel(x_ref, g_ref, sc_ref, sh_ref, w_ref, *rest, rope, widths, vt_widths):
    if rope:
        t_ref = rest[0]
        rest = rest[1:]
    if vt_widths:
        wt_ref = rest[0]
        rest = rest[1:]
    outs = rest[:len(widths)]
    vt_outs = rest[len(widths):]
    h = _norm_mod(x_ref[...], g_ref[...], sc_ref[0], sh_ref[0]).astype(BF16)
    acc = jnp.dot(h, w_ref[...], preferred_element_type=F32)
    lane = lax.broadcasted_iota(jnp.int32, (1, LANES), 1)
    off = 0
    for sec, (o_ref, w) in enumerate(zip(outs, widths)):
        if rope:
            cos = t_ref[:, (2 * sec) * LANES:(2 * sec + 1) * LANES]
            sin = t_ref[:, (2 * sec + 1) * LANES:(2 * sec + 2) * LANES]
            qd = rope[sec]
            first = (lane & (2 * qd - 1)) < qd
            for j in range(w // LANES):
                x = acc[:, off + j * LANES:off + (j + 1) * LANES]
                partner = jnp.where(first, pltpu.roll(x, LANES - qd, axis=1), pltpu.roll(x, qd, axis=1))
                o_ref[:, j * LANES:(j + 1) * LANES] = (x * cos + partner * sin).astype(o_ref.dtype)
        else:
            o_ref[...] = acc[:, off:off + w].astype(o_ref.dtype)
        off += w
    off = 0
    for o_ref, w in zip(vt_outs, vt_widths):
        o_ref[...] = lax.dot_general(wt_ref[off:off + w, :], h, (((1,), (1,)), ((), ())),
                                     preferred_element_type=F32).astype(o_ref.dtype)
        off += w


def _normproj(xs, g, mod, sc_chunk, sh_chunk, w, widths, tm, n_lat_tiles, tiles_per_batch, nb, tables=None,
              wt=None, vt_widths=()):
    t, d = xs.shape
    rope = ROPE_QUARTERS if tables is not None else ()

    def modrow(i):
        return jnp.where(i < n_lat_tiles, i // tiles_per_batch, nb)

    in_specs = [pl.BlockSpec((tm, d), lambda i: (i, 0)),
                pl.BlockSpec((1, d), lambda i: (0, 0)),
                pl.BlockSpec((1, 1, d), lambda i: (modrow(i), 0, sc_chunk)),
                pl.BlockSpec((1, 1, d), lambda i: (modrow(i), 0, sh_chunk)),
                pl.BlockSpec(w.shape, lambda i: (0, 0))]
    args = [xs, g.reshape(1, d), mod, mod, w]
    if rope:
        def tabrow(i):
            return jnp.where(i < n_lat_tiles, i % tiles_per_batch, tiles_per_batch)
        in_specs += [pl.BlockSpec((tm, tables.shape[1]), lambda i: (tabrow(i), 0))]
        args += [tables]
    if vt_widths:
        in_specs += [pl.BlockSpec(wt.shape, lambda i: (0, 0))]
        args += [wt]
    return pl.pallas_call(
        functools.partial(_normproj_kernel, rope=rope, widths=widths, vt_widths=vt_widths),
        grid=(t // tm,),
        in_specs=in_specs,
        out_specs=([pl.BlockSpec((tm, wd), lambda i: (i, 0)) for wd in widths]
                   + [pl.BlockSpec((wd, tm), lambda i: (0, i)) for wd in vt_widths]),
        out_shape=([jax.ShapeDtypeStruct((t, wd), BF16) for wd in widths]
                   + [jax.ShapeDtypeStruct((wd, t), BF16) for wd in vt_widths]),
        compiler_params=_cparams(("parallel",)),
        name="normproj_rope" if rope else "normproj_plain",
    )(*args)


def _diff_attn_kernel(*refs, lam_init, tk, nq):
    i = pl.program_id(2)

    @pl.when(i < nq)
    def _():
        _diff_attn_block(*refs, lam_init=lam_init, has_lat=True, tk=tk)

    @pl.when(i == nq)
    def _():
        _diff_attn_block(*refs, lam_init=lam_init, has_lat=False, tk=tk)


def _diff_attn_block(q_ref, kc_ref, vtc_ref, kl_ref, vtl_ref, lam_ref, g_ref, o_ref, m_scr, acc_scr, mc_scr, s_scr,
                     *, lam_init, has_lat, tk):
    q = q_ref[...]
    lane = lax.broadcasted_iota(jnp.int32, (1, LANES), 1)
    nt = (((1,), (1,)), ((), ()))
    qms = [jnp.where((lane >= i * A_QK) & (lane < (i + 1) * A_QK), q, jnp.zeros_like(q)) for i in range(4)]

    sub = 512

    def scores(k, slot, nk):
        for i in range(4):
            s = lax.dot_general(k, qms[i], nt, preferred_element_type=F32)
            s_scr[slot, i, 0:nk, :] = s
            mc_scr[slot, i] = jnp.max(s, axis=0, keepdims=True)

    def consume(vt_of, slot, nk, first):
        for i in range(4):
            m_cur = mc_scr[slot, i]
            if first:
                m_new = m_cur
            else:
                m_run = m_scr[i]
                m_new = jnp.maximum(m_run, m_cur)
            pv = None
            for t in range(nk // min(sub, nk)):
                w = min(sub, nk)
                p = jnp.exp2(s_scr[slot, i, t * w:(t + 1) * w, :] - m_new).astype(BF16)
                d = jnp.dot(vt_of(i // 2, t, w), p, preferred_element_type=F32)
                pv = d if pv is None else pv + d
            if first:
                acc_scr[i] = pv
            else:
                acc_scr[i] = jnp.exp2(m_run - m_new) * acc_scr[i] + pv
            m_scr[i] = m_new

    nctx = kc_ref.shape[0]
    scores(kc_ref[...], 1, nctx)

    def with_ones(vt):
        return jnp.concatenate([vt, jnp.ones((8, vt.shape[1]), vt.dtype)], axis=0)

    def vt_ctx(hh, t, w):
        return with_ones(vtc_ref[hh * A_V:(hh + 1) * A_V, t * w:(t + 1) * w])

    if not has_lat:
        consume(vt_ctx, 1, nctx, True)
    else:
        n_chunks = kl_ref.shape[0] // tk

        def k_lat(c):
            return kl_ref[pl.ds(pl.multiple_of(c * tk, tk), tk), :]

        def vt_lat(c):
            def get(hh, t, w):
                return with_ones(vtl_ref[hh * A_V:(hh + 1) * A_V, pl.ds(pl.multiple_of(c * tk + t * w, w), w)])
            return get

        scores(k_lat(0), 0, tk)
        consume(vt_ctx, 1, nctx, True)

        def body(j, carry):
            c = 2 * j
            scores(k_lat(c + 1), 1, tk)
            consume(vt_lat(c), 0, tk, False)
            scores(k_lat(c + 2), 0, tk)
            consume(vt_lat(c + 1), 1, tk, False)
            return carry
        pairs = (n_chunks - 1) // 2
        lax.fori_loop(0, pairs, body, 0)
        c_last = 2 * pairs
        if c_last + 1 < n_chunks:
            scores(k_lat(c_last + 1), 1, tk)
            consume(vt_lat(c_last), 0, tk, False)
            consume(vt_lat(c_last + 1), 1, tk, False)
        else:
            consume(vt_lat(c_last), 0, tk, False)

    lp = lam_ref[...]
    lam = (jnp.exp(jnp.sum(lp[0:1] * lp[1:2], axis=1, keepdims=True))
           - jnp.exp(jnp.sum(lp[2:3] * lp[3:4], axis=1, keepdims=True)) + lam_init)
    heads = []
    for hh in range(2):
        maps = []
        for m in range(2):
            acc = acc_scr[hh * 2 + m]
            maps.append(acc[0:A_V] / acc[A_V:A_V + 1])
        oh = maps[0] - lam * maps[1]
        ms = jnp.mean(oh * oh, axis=0, keepdims=True)
        heads.append(oh * lax.rsqrt(ms + EPS) * g_ref[...] * (1.0 - lam_init))
    o_ref[...] = jnp.concatenate(heads, axis=0).T.astype(o_ref.dtype)


def _diff_attn(aq, ak, avt, lam_p, norm_g, lam_init, nb, seq, ctx):
    tq = 256
    assert ctx == tq
    tk = min(1024, seq)
    nq = seq // tq
    ctx_blk0 = nb * seq // ctx

    def qblk(b, i):
        return jnp.where(i < nq, b * nq + i, ctx_blk0 + b)

    return pl.pallas_call(
        functools.partial(_diff_attn_kernel, lam_init=lam_init, tk=tk, nq=nq),
        grid=(nb, 2, nq + 1),
        in_specs=[pl.BlockSpec((tq, LANES), lambda b, p, i: (qblk(b, i), p)),
                  pl.BlockSpec((ctx, LANES), lambda b, p, i: (ctx_blk0 + b, p)),
                  pl.BlockSpec((2 * A_V, ctx), lambda b, p, i: (p, ctx_blk0 + b)),
                  pl.BlockSpec((seq, LANES), lambda b, p, i: (b, p)),
                  pl.BlockSpec((2 * A_V, seq), lambda b, p, i: (p, b)),
                  pl.BlockSpec((4, A_QK), lambda b, p, i: (0, 0)),
                  pl.BlockSpec((A_V, 1), lambda b, p, i: (0, 0))],
        out_specs=pl.BlockSpec((tq, LANES), lambda b, p, i: (qblk(b, i), p)),
        out_shape=jax.ShapeDtypeStruct((aq.shape[0], 2 * LANES), BF16),
        scratch_shapes=[pltpu.VMEM((4, 1, tq), F32), pltpu.VMEM((4, A_V + 8, tq), F32),
                        pltpu.VMEM((2, 4, 1, tq), F32),
                        pltpu.VMEM((2, 4, tk, tq), F32)],
        compiler_params=_cparams(("parallel", "parallel", "arbitrary")),
        name="diff_attn",
    )(aq, ak, avt, ak, avt, lam_p, norm_g.reshape(A_V, 1))


def _win_attn_kernel(*refs, seq, ctx, tq):
    n = pl.program_id(1)

    @pl.when(n < seq // tq)
    def _():
        _win_attn_block(*refs, with_window=True, seq=seq, ctx=ctx, tq=tq)

    @pl.when(n >= seq // tq)
    def _():
        _win_attn_block(*refs, with_window=False, seq=seq, ctx=ctx, tq=tq)


def _win_attn_block(q_ref, kc_ref, vtc_ref, kp_ref, kn0_ref, kn_ref, vtp_ref, vt0_ref, vtn_ref, sink_ref, o_ref,
                    *, with_window, seq, ctx, tq):
    if with_window:
        n = pl.program_id(1)
        k_all = jnp.concatenate([kc_ref[...], kp_ref[...], kn0_ref[...], kn_ref[...]], axis=0)
        vt_all = jnp.concatenate([vtc_ref[...], vtp_ref[...], vt0_ref[...], vtn_ref[...]], axis=1)
        nk = ctx + tq + 2 * BLOCK
        r = lax.broadcasted_iota(jnp.int32, (nk, tq), 0)
        c = lax.broadcasted_iota(jnp.int32, (nk, tq), 1)
        krel = r - (ctx + BLOCK)
        start = n * tq
        in_win = ((jnp.abs(c - krel) <= WINDOW) & (krel >= -start) & (krel < seq - start))
        is_ctx = r < ctx
    else:
        k_all = kc_ref[...]
        vt_all = vtc_ref[...]
    lane = lax.broadcasted_iota(jnp.int32, (1, LANES), 1)
    sink_row = sink_ref[...]
    nt = (((1,), (1,)), ((), ()))
    rep = B_HEADS // B_KV
    outs = []
    for j in range(B_HEADS):
        g = j // rep
        s = lax.dot_general(k_all, q_ref[:, j * LANES:(j + 1) * LANES], nt, preferred_element_type=F32)
        if with_window:
            s = jnp.where(is_ctx, s, jnp.where(in_win, s, NEG))
        sk = jnp.max(jnp.where(lane == j, sink_row, NEG), axis=1, keepdims=True)
        m = jnp.maximum(jnp.max(s, axis=0, keepdims=True), sk)
        p = jnp.exp2(s - m)
        l = jnp.sum(p, axis=0, keepdims=True) + jnp.exp2(sk - m)
        ot = jnp.dot(vt_all, p.astype(BF16), preferred_element_type=F32)
        outs.append(ot[g * B_HD:(g + 1) * B_HD] / l)
    for t in range(B_HEADS // 2):
        pair = jnp.concatenate([outs[2 * t], outs[2 * t + 1]], axis=0)
        o_ref[:, t * LANES:(t + 1) * LANES] = pair.T.astype(o_ref.dtype)


def _win_attn(bq, bk, bvt, sink_row, nb, seq, ctx):
    tq = BLOCK
    per = tq // BLOCK
    nblk = seq // BLOCK
    ntile = seq // tq
    nctx_t = ctx // tq
    ctx_blk0 = nb * seq // ctx
    q_ctx0 = nb * seq // tq
    qw = B_HEADS * LANES

    def qblk(b, n):
        return jnp.where(n < ntile, b * ntile + n, q_ctx0 + b * nctx_t + (n - ntile))

    def lat(n):
        return jnp.minimum(n, ntile - 1)

    def prev(b, n):
        return b * nblk + jnp.maximum(per * lat(n) - 1, 0)

    def cur(b, n):
        return b * ntile + lat(n)

    def nxt(b, n):
        return b * nblk + jnp.minimum(per * lat(n) + per, nblk - 1)

    return pl.pallas_call(
        functools.partial(_win_attn_kernel, seq=seq, ctx=ctx, tq=tq),
        grid=(nb, ntile + nctx_t),
        in_specs=[pl.BlockSpec((tq, qw), lambda b, n: (qblk(b, n), 0)),
                  pl.BlockSpec((ctx, LANES), lambda b, n: (ctx_blk0 + b, 0)),
                  pl.BlockSpec((LANES, ctx), lambda b, n: (0, ctx_blk0 + b)),
                  pl.BlockSpec((BLOCK, LANES), lambda b, n: (prev(b, n), 0)),
                  pl.BlockSpec((tq, LANES), lambda b, n: (cur(b, n), 0)),
                  pl.BlockSpec((BLOCK, LANES), lambda b, n: (nxt(b, n), 0)),
                  pl.BlockSpec((LANES, BLOCK), lambda b, n: (0, prev(b, n))),
                  pl.BlockSpec((LANES, tq), lambda b, n: (0, cur(b, n))),
                  pl.BlockSpec((LANES, BLOCK), lambda b, n: (0, nxt(b, n))),
                  pl.BlockSpec((1, LANES), lambda b, n: (0, 0))],
        out_specs=pl.BlockSpec((tq, B_HEADS * B_HD), lambda b, n: (qblk(b, n), 0)),
        out_shape=jax.ShapeDtypeStruct((bq.shape[0], B_HEADS * B_HD), BF16),
        compiler_params=_cparams(("parallel", "arbitrary")),
        name="win_attn",
    )(bq, bk, bvt, bk, bk, bk, bvt, bvt, bvt, sink_row)


def _split3(x):
    hi = x.astype(BF16)
    r1 = x - hi.astype(F32)
    mid = r1.astype(BF16)
    lo = (r1 - mid.astype(F32)).astype(BF16)
    return hi, mid, lo


def _gla_kernel(*refs, reverse, final, n_sub):
    if final:
        (q_ref, k_ref, v_ref, vt_ref, gl_ref, w2_ref, bg_ref, sin_ref, of_ref, r_ref, ng_ref,
         o_ref, sout_ref, st_ref, qd_scr, ke_scr, dec_scr, o_scr) = refs
    else:
        (q_ref, k_ref, v_ref, vt_ref, gl_ref, w2_ref, bg_ref, sin_ref,
         o_ref, sout_ref, st_ref, qd_scr, ke_scr, dec_scr, o_scr) = refs
    gsz = GLA_GROUP
    nch = gsz // C_CHUNK
    step = pl.program_id(1)

    @pl.when(step == 0)
    def _():
        st_ref[...] = sin_ref[0]

    r = lax.broadcasted_iota(jnp.int32, (gsz, gsz), 0)
    c = lax.broadcasted_iota(jnp.int32, (gsz, gsz), 1)
    same = (r // C_CHUNK) == (c // C_CHUNK)
    tri = same & ((c >= r) if reverse else (c <= r))
    tri_b = jnp.where(tri, 1.0, 0.0).astype(BF16)
    edge = 0 if reverse else C_CHUNK - 1
    nt = (((1,), (1,)), ((), ()))

    for sg in range(n_sub):
        rs = slice(sg * gsz, (sg + 1) * gsz)
        q = q_ref[rs, :].astype(F32)
        k = k_ref[rs, :].astype(F32)
        pre = jnp.dot(gl_ref[rs, :], w2_ref[...], preferred_element_type=F32) + bg_ref[...]
        la = (jnp.minimum(pre, 0.0) - jnp.log(1.0 + jnp.exp(-jnp.abs(pre)))) * (1.0 / C_GATE_NORM)
        parts = _split3(la)
        bcum = sum(jnp.dot(tri_b, p, preferred_element_type=F32) for p in parts)
        btot = jnp.concatenate(
            [jnp.broadcast_to(bcum[ci * C_CHUNK + edge:ci * C_CHUNK + edge + 1], (C_CHUNK, bcum.shape[1]))
             for ci in range(nch)], axis=0)
        qd = (q * (jnp.exp(bcum) * (C_DK ** -0.5))).astype(BF16)
        ki = (k * jnp.exp(-bcum)).astype(BF16)
        qd_scr[rs, :] = qd
        ke_scr[rs, :] = (k * jnp.exp(btot - bcum)).astype(BF16)
        for ci in range(nch):
            dec_scr[sg * nch + ci:sg * nch + ci + 1, :] = jnp.exp(bcum[ci * C_CHUNK + edge:ci * C_CHUNK + edge + 1])
        for h in range(C_HEADS):
            sl = slice(h * LANES, (h + 1) * LANES)
            att = lax.dot_general(qd[:, sl], ki[:, sl], nt, preferred_element_type=F32)
            att = jnp.where(tri, att, 0.0).astype(BF16)
            o_scr[rs, sl] = jnp.dot(att, v_ref[rs, sl], preferred_element_type=F32)

    rowid = lax.broadcasted_iota(jnp.int32, (gsz, LANES), 0) // C_CHUNK
    sub_order = list(range(n_sub))[::-1] if reverse else list(range(n_sub))
    chunk_order = list(range(nch))[::-1] if reverse else list(range(nch))
    for h in range(C_HEADS):
        sl = slice(h * LANES, (h + 1) * LANES)
        st = st_ref[h]
        for sg in sub_order:
            r0 = sg * gsz
            ke = ke_scr[r0:r0 + gsz, sl]
            vth = vt_ref[sl, r0:r0 + gsz]
            for ci in chunk_order:
                rows = slice(r0 + ci * C_CHUNK, r0 + (ci + 1) * C_CHUNK)
                o_scr[rows, sl] += lax.dot_general(qd_scr[rows, sl], st.astype(BF16), nt,
                                                   preferred_element_type=F32)
                ke_c = jnp.where(rowid == ci, ke, jnp.zeros_like(ke))
                upd = jnp.dot(vth, ke_c, preferred_element_type=F32)
                st = dec_scr[sg * nch + ci:sg * nch + ci + 1, sl] * st + upd
        st_ref[h] = st

    if not final:
        o_ref[...] = o_scr[...]
    else:
        for h in range(C_HEADS):
            sl = slice(h * LANES, (h + 1) * LANES)
            o = o_scr[:, sl] + of_ref[:, sl]
            ms = jnp.sum(o * o, axis=1, keepdims=True) * (1.0 / C_DV)
            y = o * lax.rsqrt(ms + EPS) * ng_ref[:, sl]
            rr = r_ref[:, sl].astype(F32)
            o_ref[:, sl] = (y * (rr * (1.0 / (1.0 + jnp.exp(-rr))))).astype(o_ref.dtype)

    @pl.when(step == pl.num_programs(1) - 1)
    def _():
        sout_ref[0] = st_ref[...]


def _gla_dir(cq, ck, cv, cvt, cg, w2p, bgp, st_in, nb, seq, ctx, reverse, latent, fwd_out=None, cr=None,
             ng=None):
    gsz = GLA_GROUP
    assert ctx == gsz
    final = fwd_out is not None
    w = C_HEADS * LANES
    if latent:
        n_sub = 4
        rows = n_sub * gsz
        nt_ = seq // rows

        def blk(b, i):
            return b * nt_ + ((nt_ - 1 - i) if reverse else i)
        sep_blk = blk
        grid = (nb, nt_)
        n_out = nb * seq
    else:
        n_sub = 1
        rows = gsz
        blk0 = nb * seq // gsz

        def blk(b, i):
            return blk0 + b

        def sep_blk(b, i):
            return b
        grid = (nb, 1)
        n_out = nb * ctx
    row_spec = pl.BlockSpec((rows, w), lambda b, i: (blk(b, i), 0))
    out_spec = pl.BlockSpec((rows, w), lambda b, i: (sep_blk(b, i), 0))
    st_spec = pl.BlockSpec((1, C_HEADS, LANES, LANES), lambda b, i: (b, 0, 0, 0))
    in_specs = [row_spec, row_spec, row_spec,
                pl.BlockSpec((w, rows), lambda b, i: (0, blk(b, i))),
                pl.BlockSpec((rows, LANES), lambda b, i: (blk(b, i), 0)),
                pl.BlockSpec((LANES, w), lambda b, i: (0, 0)),
                pl.BlockSpec((1, w), lambda b, i: (0, 0)),
                st_spec]
    args = [cq, ck, cv, cvt, cg, w2p, bgp, st_in]
    if final:
        in_specs += [out_spec, row_spec, pl.BlockSpec((1, w), lambda b, i: (0, 0))]
        args += [fwd_out, cr, ng]
    name = ("gla_bwd" if reverse else "gla_fwd") + ("_lat" if latent else "_ctx")
    return pl.pallas_call(
        functools.partial(_gla_kernel, reverse=reverse, final=final, n_sub=n_sub),
        grid=grid,
        in_specs=in_specs,
        out_specs=[out_spec, st_spec],
        out_shape=[jax.ShapeDtypeStruct((n_out, w), BF16 if final else F32),
                   jax.ShapeDtypeStruct(st_in.shape, F32)],
        scratch_shapes=[pltpu.VMEM((C_HEADS, LANES, LANES), F32),
                        pltpu.VMEM((rows, w), BF16), pltpu.VMEM((rows, w), BF16),
                        pltpu.VMEM((max(8, n_sub * (gsz // C_CHUNK)), w), F32),
                        pltpu.VMEM((rows, w), F32)],
        compiler_params=_cparams(("parallel", "arbitrary")),
        name=name,
    )(*args)


def _outproj_kernel(x_ref, a_ref, b_ref, c_ref, wa_ref, wb_ref, wc_ref, g1_ref, o_ref):
    y = jnp.dot(a_ref[...], wa_ref[...], preferred_element_type=F32)
    y += jnp.dot(b_ref[...], wb_ref[...], preferred_element_type=F32)
    y += jnp.dot(c_ref[...], wc_ref[...], preferred_element_type=F32)
    o_ref[...] = x_ref[...] + g1_ref[0] * y


def _outproj(xs, a, b, c, wa, wb, wc, mod, tm, modrow):
    t, d = xs.shape
    return pl.pallas_call(
        _outproj_kernel,
        grid=(t // tm,),
        in_specs=[pl.BlockSpec((tm, d), lambda i: (i, 0)),
                  pl.BlockSpec((tm, a.shape[1]), lambda i: (i, 0)),
                  pl.BlockSpec((tm, b.shape[1]), lambda i: (i, 0)),
                  pl.BlockSpec((tm, c.shape[1]), lambda i: (i, 0)),
                  pl.BlockSpec(wa.shape, lambda i: (0, 0)),
                  pl.BlockSpec(wb.shape, lambda i: (0, 0)),
                  pl.BlockSpec(wc.shape, lambda i: (0, 0)),
                  pl.BlockSpec((1, 1, d), lambda i: (modrow(i), 0, 2))],
        out_specs=pl.BlockSpec((tm, d), lambda i: (i, 0)),
        out_shape=jax.ShapeDtypeStruct((t, d), F32),
        compiler_params=_cparams(("parallel",)),
        name="outproj",
    )(xs, a, b, c, wa, wb, wc, mod)


def _ffn_kernel(x_ref, g_ref, sc_ref, sh_ref, gate_ref, wg_ref, wu_ref, wd_ref, o_ref, h_ref, acc_ref):
    f = pl.program_id(1)

    @pl.when(f == 0)
    def _():
        h_ref[...] = _norm_mod(x_ref[...], g_ref[...], sc_ref[0], sh_ref[0]).astype(BF16)
        acc_ref[...] = jnp.zeros(acc_ref.shape, F32)

    h = h_ref[...]
    a = jnp.dot(h, wg_ref[...].astype(BF16), preferred_element_type=F32)
    u = jnp.dot(h, wu_ref[...].astype(BF16), preferred_element_type=F32)
    act = a * (1.0 / (1.0 + jnp.exp(-a))) * u
    acc_ref[...] += jnp.dot(act.astype(BF16), wd_ref[...].astype(BF16), preferred_element_type=F32)

    @pl.when(f == pl.num_programs(1) - 1)
    def _():
        o_ref[...] = x_ref[...] + gate_ref[0] * acc_ref[...]


def _ffn(xs, g, mod, wg, wu, wd, tm, tf, modrow):
    t, d = xs.shape
    ff = wg.shape[1]
    return pl.pallas_call(
        _ffn_kernel,
        grid=(t // tm, ff // tf),
        in_specs=[pl.BlockSpec((tm, d), lambda i, f: (i, 0)),
                  pl.BlockSpec((1, d), lambda i, f: (0, 0)),
                  pl.BlockSpec((1, 1, d), lambda i, f: (modrow(i), 0, 4)),
                  pl.BlockSpec((1, 1, d), lambda i, f: (modrow(i), 0, 3)),
                  pl.BlockSpec((1, 1, d), lambda i, f: (modrow(i), 0, 5)),
                  pl.BlockSpec((d, tf), lambda i, f: (0, f)),
                  pl.BlockSpec((d, tf), lambda i, f: (0, f)),
                  pl.BlockSpec((tf, d), lambda i, f: (f, 0))],
        out_specs=pl.BlockSpec((tm, d), lambda i, f: (i, 0)),
        out_shape=jax.ShapeDtypeStruct((t, d), F32),
        scratch_shapes=[pltpu.VMEM((tm, d), BF16), pltpu.VMEM((tm, d), F32)],
        compiler_params=_cparams(("parallel", "arbitrary")),
        name="ffn_swiglu",
    )(xs, g.reshape(1, d), mod, mod, mod, wg, wu, wd)


def _router_kernel(x_ref, g_ref, sc_ref, sh_ref, rt_ref, h_ref, r_ref):
    lane = lax.broadcasted_iota(jnp.int32, (1, LANES), 1)
    hf = _norm_mod(x_ref[...], g_ref[...], sc_ref[0], sh_ref[0])
    h_ref[...] = hf
    logits = jnp.dot(hf, rt_ref[...], precision=lax.Precision.HIGHEST, preferred_element_type=F32)
    lanef = lane.astype(F32)
    lg = jnp.where(lane < N_EXPERTS, logits, NEG)
    m1 = jnp.max(lg, axis=1, keepdims=True)
    i1 = jnp.min(jnp.where(lg == m1, lanef, float(LANES)), axis=1, keepdims=True)
    lg2 = jnp.where(lanef == i1, NEG, lg)
    m2 = jnp.max(lg2, axis=1, keepdims=True)
    i2 = jnp.min(jnp.where(lg2 == m2, lanef, float(LANES)), axis=1, keepdims=True)
    e2 = jnp.exp(m2 - m1)
    w1 = 1.0 / (1.0 + e2)
    r_ref[...] = jnp.where(lane == 0, i1, jnp.where(lane == 1, i2, jnp.where(lane == 2, w1,
                           jnp.where(lane == 3, e2 * w1, 0.0))))


def _router(xs, g, mod, router_p, tm, modrow):
    t, d = xs.shape
    return pl.pallas_call(
        _router_kernel,
        grid=(t // tm,),
        in_specs=[pl.BlockSpec((tm, d), lambda i: (i, 0)),
                  pl.BlockSpec((1, d), lambda i: (0, 0)),
                  pl.BlockSpec((1, 1, d), lambda i: (modrow(i), 0, 4)),
                  pl.BlockSpec((1, 1, d), lambda i: (modrow(i), 0, 3)),
                  pl.BlockSpec((d, LANES), lambda i: (0, 0))],
        out_specs=[pl.BlockSpec((tm, d), lambda i: (i, 0)), pl.BlockSpec((tm, LANES), lambda i: (i, 0))],
        out_shape=[jax.ShapeDtypeStruct((t, d), F32), jax.ShapeDtypeStruct((t, LANES), F32)],
        compiler_params=_cparams(("parallel",)),
        name="moe_router",
    )(xs, g.reshape(1, d), mod, mod, router_p)


def _route_plan(rinfo, tg):
    t = rinfo.shape[0]
    n_tiles = -(-2 * t // tg) + N_EXPERTS
    e_flat = jnp.concatenate([rinfo[:, 0], rinfo[:, 1]]).astype(jnp.int32)
    onehot = (e_flat[:, None] == jnp.arange(N_EXPERTS, dtype=jnp.int32)[None, :]).astype(jnp.int32)
    csum = jnp.cumsum(onehot, axis=0)
    rank = jnp.sum(onehot * (csum - 1), axis=1)
    counts = csum[-1]
    padded = (counts + tg - 1) // tg * tg
    ends = jnp.cumsum(padded)
    pos = jnp.sum(onehot * (ends - padded)[None, :], axis=1) + rank
    tile_start = jnp.arange(n_tiles, dtype=jnp.int32) * tg
    tile_expert = jnp.minimum(jnp.sum((tile_start[:, None] >= ends[None, :]).astype(jnp.int32), axis=1),
                              N_EXPERTS - 1)
    n_used = (ends[-1] // tg).reshape(1)
    flat = jnp.arange(2 * t, dtype=jnp.int32)
    order = jnp.sort(e_flat * (2 * t) + flat) % (2 * t)
    unpadded_start = jnp.cumsum(counts) - counts
    row = jnp.arange(n_tiles * tg, dtype=jnp.int32)
    row_e = jnp.repeat(tile_expert, tg)
    r_in = row - (ends - padded)[row_e]
    src = order[jnp.minimum(unpadded_start[row_e] + r_in, 2 * t - 1)] % t
    src_tok = jnp.where(r_in < counts[row_e], src, 0)
    return src_tok, tile_expert, n_used, pos


def _row_copy(src_ref, row, dst_ref, r, sem):
    return pltpu.make_async_copy(src_ref.at[pl.ds(row, 1), :], dst_ref.at[pl.ds(r, 1), :], sem)


def _moe_experts_kernel(te_ref, nu_ref, idx_ref, idxn_ref, x_ref, wg_ref, wu_ref, wd_ref, o_ref,
                        xbuf, h_ref, acc_ref, sems, *, rows_per_step, n_steps):
    i = pl.program_id(0)
    f = pl.program_id(1)
    nf = pl.num_programs(1)
    last = nf - 1
    n_tiles = pl.num_programs(0)
    n_used = nu_ref[0]
    used = i < n_used
    slot = i % 2
    rows = h_ref.shape[0]
    per_step = rows_per_step
    n_dma = per_step * n_steps

    def drain(s):
        def body(r, carry):
            _row_copy(x_ref, 0, xbuf.at[s], r, sems.at[s]).wait()
            return carry
        lax.fori_loop(0, n_dma, body, 0, unroll=n_steps)

    @pl.when((i == 0) & (f == 0))
    def _():
        def body(r, carry):
            _row_copy(x_ref, idx_ref[0, 0, r], xbuf.at[0], r, sems.at[0]).start()
            return carry
        lax.fori_loop(0, n_dma, body, 0, unroll=n_steps)

    @pl.when((i <= n_used) & (f == 0))
    def _():
        drain(slot)

    @pl.when(used & (f == 0))
    def _():
        h_ref[...] = xbuf[slot, 0:rows].astype(BF16)
        acc_ref[...] = jnp.zeros(acc_ref.shape, F32)

    @pl.when(used)
    def _():
        for k in range(per_step):
            r = f * per_step + k
            _row_copy(x_ref, idxn_ref[0, 0, r], xbuf.at[1 - slot], r, sems.at[1 - slot]).start()
        h = h_ref[...]
        a = jnp.dot(h, wg_ref[0, 0].astype(BF16), preferred_element_type=F32)
        u = jnp.dot(h, wu_ref[0, 0].astype(BF16), preferred_element_type=F32)
        act = a * (1.0 / (1.0 + jnp.exp(-a))) * u
        acc_ref[...] += jnp.dot(act.astype(BF16), wd_ref[0, 0].astype(BF16), preferred_element_type=F32)

    @pl.when(used & (f == last))
    def _():
        o_ref[...] = acc_ref[...]

    @pl.when(used & (i == n_tiles - 1) & (f == last))
    def _():
        drain(1 - slot)

    @pl.when(jnp.logical_not(used) & (f == last))
    def _():
        o_ref[...] = jnp.zeros(o_ref.shape, F32)


def _moe_experts(h2, src_tok, tile_expert, n_used, wg, wu, wd, layer, tg, tf):
    d = h2.shape[1]
    n_tiles = src_tok.shape[0] // tg
    ff = wg.shape[3]
    nf = ff // tf
    per_step = -(-tg // nf)
    n_dma = per_step * nf
    idw = -(-n_dma // LANES) * LANES
    ids = jnp.pad(src_tok.reshape(n_tiles, 1, tg), ((0, 0), (0, 0), (0, idw - tg)))
    grid_spec = pltpu.PrefetchScalarGridSpec(
        num_scalar_prefetch=2,
        grid=(n_tiles, nf),
        in_specs=[pl.BlockSpec((1, 1, idw), lambda i, f, te, nu: (i, 0, 0), memory_space=pltpu.SMEM),
                  pl.BlockSpec((1, 1, idw), lambda i, f, te, nu: (jnp.minimum(i + 1, n_tiles - 1), 0, 0),
                               memory_space=pltpu.SMEM),
                  pl.BlockSpec(memory_space=pl.ANY),
                  pl.BlockSpec((1, 1, d, tf), lambda i, f, te, nu: (layer, te[i], 0, f)),
                  pl.BlockSpec((1, 1, d, tf), lambda i, f, te, nu: (layer, te[i], 0, f)),
                  pl.BlockSpec((1, 1, tf, d), lambda i, f, te, nu: (layer, te[i], f, 0))],
        out_specs=pl.BlockSpec((tg, d), lambda i, f, te, nu: (i, 0)),
        scratch_shapes=[pltpu.VMEM((2, -(-n_dma // 8) * 8, d), F32), pltpu.VMEM((tg, d), BF16),
                        pltpu.VMEM((tg, d), F32), pltpu.SemaphoreType.DMA((2,))])
    return pl.pallas_call(
        functools.partial(_moe_experts_kernel, rows_per_step=per_step, n_steps=nf),
        grid_spec=grid_spec,
        out_shape=jax.ShapeDtypeStruct((n_tiles * tg, d), F32),
        compiler_params=_cparams(("arbitrary", "arbitrary")),
        name="moe_experts",
    )(tile_expert, n_used, ids, ids, h2, wg, wu, wd)


def _moe_combine_kernel(i0_ref, i1_ref, n0_ref, n1_ref, y_ref, x_ref, r_ref, gate_ref, o_ref, buf, sems):
    i = pl.program_id(0)
    slot = i % 2
    rows = o_ref.shape[0]

    def fetch(a_ref, b_ref, s):
        def body(r, carry):
            _row_copy(y_ref, a_ref[0, 0, r], buf.at[s, 0], r, sems.at[s]).start()
            _row_copy(y_ref, b_ref[0, 0, r], buf.at[s, 1], r, sems.at[s]).start()
            return carry
        lax.fori_loop(0, rows, body, 0, unroll=8)

    @pl.when(i == 0)
    def _():
        fetch(i0_ref, i1_ref, 0)

    @pl.when(i + 1 < pl.num_programs(0))
    def _():
        fetch(n0_ref, n1_ref, 1 - slot)

    def drain(r, carry):
        _row_copy(y_ref, 0, buf.at[slot, 0], r, sems.at[slot]).wait()
        _row_copy(y_ref, 0, buf.at[slot, 1], r, sems.at[slot]).wait()
        return carry
    lax.fori_loop(0, rows, drain, 0, unroll=8)
    y = r_ref[:, 2:3] * buf[slot, 0] + r_ref[:, 3:4] * buf[slot, 1]
    o_ref[...] = x_ref[...] + gate_ref[0] * y


def _moe_combine(xs, yg, pos, rinfo, mod, tm, modrow):
    t, d = xs.shape
    nt = t // tm
    p3 = pos.reshape(2, nt, 1, tm)

    def cur(i):
        return (i, 0, 0)

    def nxt(i):
        return (jnp.minimum(i + 1, nt - 1), 0, 0)

    return pl.pallas_call(
        _moe_combine_kernel,
        grid=(nt,),
        in_specs=[pl.BlockSpec((1, 1, tm), cur, memory_space=pltpu.SMEM),
                  pl.BlockSpec((1, 1, tm), cur, memory_space=pltpu.SMEM),
                  pl.BlockSpec((1, 1, tm), nxt, memory_space=pltpu.SMEM),
                  pl.BlockSpec((1, 1, tm), nxt, memory_space=pltpu.SMEM),
                  pl.BlockSpec(memory_space=pl.ANY),
                  pl.BlockSpec((tm, d), lambda i: (i, 0)),
                  pl.BlockSpec((tm, LANES), lambda i: (i, 0)),
                  pl.BlockSpec((1, 1, d), lambda i: (modrow(i), 0, 5))],
        out_specs=pl.BlockSpec((tm, d), lambda i: (i, 0)),
        out_shape=jax.ShapeDtypeStruct((t, d), F32),
        scratch_shapes=[pltpu.VMEM((2, 2, tm, d), F32), pltpu.SemaphoreType.DMA((2,))],
        compiler_params=_cparams(("arbitrary",)),
        name="moe_combine",
    )(p3[0], p3[1], p3[0], p3[1], yg, xs, rinfo, mod)


def _moe(xs, g, mod, router_p, wg, wu, wd, layer, tm, tf, modrow):
    tg = tm
    h2, rinfo = _router(xs, g, mod, router_p, tm, modrow)
    src_tok, tile_expert, n_used, pos = _route_plan(rinfo, tg)
    yg = _moe_experts(h2, src_tok, tile_expert, n_used, wg, wu, wd, layer, tg, tf)
    return _moe_combine(xs, yg, pos, rinfo, mod, tm, modrow)


def _final_norm_kernel(x_ref, g_ref, o_ref):
    x = x_ref[...]
    ms = jnp.mean(x * x, axis=-1, keepdims=True)
    o_ref[...] = x * lax.rsqrt(ms + EPS) * g_ref[...]


def _final_norm(xs, g, rows, tm):
    d = xs.shape[1]
    return pl.pallas_call(
        _final_norm_kernel,
        grid=(rows // tm,),
        in_specs=[pl.BlockSpec((tm, d), lambda i: (i, 0)), pl.BlockSpec((1, d), lambda i: (0, 0))],
        out_specs=pl.BlockSpec((tm, d), lambda i: (i, 0)),
        out_shape=jax.ShapeDtypeStruct((rows, d), F32),
        compiler_params=_cparams(("parallel",)),
        name="final_norm",
    )(xs, g.reshape(1, d))


def _pad_heads(v, width, used):
    lead = v.shape[:-1]
    v = v.reshape(lead + (C_HEADS, used))
    v = jnp.pad(v, [(0, 0)] * len(lead) + [(0, 0), (0, width - used)])
    return v.reshape(lead + (C_HEADS * width,))


def kernel(x, c, ctx, c_ctx, norm1_g, norm2_g, ada_w, ada_b, w_in, w_out, a_lambda, a_norm_g, b_sink,
           c_gate_w2, c_gate_b, c_norm_g, ffn_w_gate, ffn_w_up, ffn_w_down, moe_router, moe_w_gate,
           moe_w_up, moe_w_down, final_g):
    nb, seq, d = x.shape
    nctx = ctx.shape[1]
    depth = w_in.shape[0]
    n_lat = nb * seq
    tm = nb * nctx
    assert seq % tm == 0 and nb < 8
    tm_r = tm // 2
    n_lat_tiles = n_lat // tm

    def modrow(i):
        return jnp.where(i < n_lat_tiles, i // (seq // tm), nb)

    xs = jnp.concatenate([x.reshape(n_lat, d), ctx.reshape(nb * nctx, d)], axis=0)
    cc = jnp.zeros((8, d), F32).at[:nb].set(c).at[nb].set(c_ctx)
    mod_all = _modulation(cc, ada_w, ada_b).reshape(depth, 8, 1, 6 * d)
    tables = _rope_tables(seq, tm_r)

    for layer in range(depth):
        lam_init = 0.8 - 0.6 * math.exp(-0.3 * layer)
        mod = mod_all[layer]
        w_pad = jnp.concatenate([w_in[layer], jnp.zeros((d, 1), F32)], axis=1)
        w_rope = jnp.take(w_pad, _ROPE_COLS, axis=1).astype(BF16)
        w_plain = jnp.take(w_pad, _PLAIN_COLS, axis=1).astype(BF16)
        w_vt = jnp.take(w_pad, _VT_COLS, axis=1).T.astype(BF16)

        aq, ak, bq, bk = _normproj(xs, norm1_g[layer], mod, 1, 0, w_rope, ROPE_WIDTHS, tm_r,
                                   n_lat // tm_r, seq // tm_r, nb, tables)
        cq, ck, cv, cr, cg, avt, bvt, cvt = _normproj(xs, norm1_g[layer], mod, 1, 0, w_plain, PLAIN_WIDTHS, tm,
                                                      n_lat_tiles, seq // tm, nb, wt=w_vt, vt_widths=VT_WIDTHS)

        a_all = _diff_attn(aq, ak, avt, a_lambda[layer], a_norm_g[layer], lam_init, nb, seq, nctx)
        sink_row = jnp.zeros((1, LANES), F32).at[0, :B_HEADS].set(b_sink[layer] * LOG2E)
        b_all = _win_attn(bq, bk, bvt, sink_row, nb, seq, nctx)

        w2 = c_gate_w2[layer]
        w2p = [jnp.zeros((LANES, C_HEADS * LANES), F32).at[dd * C_RANK:(dd + 1) * C_RANK].set(
            _pad_heads(w2[dd], LANES, C_DK)).astype(BF16) for dd in range(2)]
        bgp = [_pad_heads(c_gate_b[layer, dd], LANES, C_DK).reshape(1, -1) for dd in range(2)]
        ng = _pad_heads(jnp.tile(c_norm_g[layer], C_HEADS), LANES, C_DV).reshape(1, -1)
        st0 = jnp.zeros((nb, C_HEADS, LANES, LANES), F32)
        gla_in = (cq, ck, cv, cvt, cg)
        of_ctx, st_f = _gla_dir(*gla_in, w2p[0], bgp[0], st0, nb, seq, nctx, False, False)
        of_lat, _ = _gla_dir(*gla_in, w2p[0], bgp[0], st_f, nb, seq, nctx, False, True)
        g_ctx, st_b = _gla_dir(*gla_in, w2p[1], bgp[1], st0, nb, seq, nctx, True, False, of_ctx, cr, ng)
        g_lat, _ = _gla_dir(*gla_in, w2p[1], bgp[1], st_b, nb, seq, nctx, True, True, of_lat, cr, ng)
        g_out = jnp.concatenate([g_lat, g_ctx], axis=0)

        wo = w_out[layer]
        wa = wo[:256].astype(BF16)
        wb = wo[256:640].astype(BF16)
        wc = jnp.pad(wo[640:].reshape(C_HEADS, C_DV, d), ((0, 0), (0, LANES - C_DV), (0, 0))).reshape(
            C_HEADS * LANES, d).astype(BF16)
        xs = _outproj(xs, a_all, b_all, g_out, wa, wb, wc, mod, tm, modrow)

        j = layer // 2
        if layer % 2 == 0:
            xs = _ffn(xs, norm2_g[layer], mod, ffn_w_gate[j], ffn_w_up[j], ffn_w_down[j], tm, 256, modrow)
        else:
            router_p = jnp.pad(moe_router[j], ((0, 0), (0, LANES - N_EXPERTS)))
            xs = _moe(xs, norm2_g[layer], mod, router_p, moe_w_gate, moe_w_up, moe_w_down, j,
                      tm, 512, modrow)

    return _final_norm(xs, final_g, n_lat, tm).reshape(nb, seq, d)
```

```python
import functools
import math

import numpy as np
import jax
import jax.numpy as jnp
from jax import lax
from jax.experimental import pallas as pl
from jax.experimental.pallas import tpu as pltpu

F32 = jnp.float32
BF16 = jnp.bfloat16

EPS = 1e-6
ROPE_BASE = 10000.0
GRID_W = 64
LANES = 128
LOG2E = math.log2(math.e)
NEG = -1e30

A_HEADS, A_QK, A_V = 4, 32, 64
B_HEADS, B_KV, B_HD, WINDOW, BLOCK = 6, 2, 64, 128, 128
C_HEADS, C_DK, C_DV, C_RANK, C_GATE_NORM, C_CHUNK = 4, 48, 96, 16, 16.0, 64
N_EXPERTS = 8
IN_SIZES = (256, 256, 256, 384, 128, 128, 192, 192, 384, 384, 32)
IN_W = sum(IN_SIZES)

ROPE_WIDTHS = (256, 256, 768, 128)
PLAIN_WIDTHS = (512, 512, 512, 512, 128)
VT_WIDTHS = (256, 128, 512)
ROPE_W = sum(ROPE_WIDTHS)
ROPE_QUARTERS = (A_QK // 4, A_QK // 4, B_HD // 4, B_HD // 4)
GLA_GROUP = 256
VMEM_LIMIT = 48 * 1024 * 1024


def _cparams(sem):
    return pltpu.CompilerParams(dimension_semantics=sem, vmem_limit_bytes=VMEM_LIMIT)


def _column_maps():
    off = np.concatenate([[0], np.cumsum(IN_SIZES)])
    aq0, ak0, av0, bq0, bk0, bv0, cq0, ck0, cv0, cr0, cg0 = [int(v) for v in off[:11]]
    zero = IN_W

    main, dim, dd, scale = [], [], [], []
    a_scale = A_QK ** -0.5 * LOG2E
    b_scale = B_HD ** -0.5 * LOG2E
    for base, sc in ((aq0, a_scale), (ak0, 1.0)):
        for j in range(256):
            main.append(base + j)
            dim.append(32); dd.append(j % 32); scale.append(sc)
    for t in range(B_HEADS):
        g = t // (B_HEADS // B_KV)
        for lane in range(LANES):
            d = lane % 64
            main.append(bq0 + t * 64 + d if lane // 64 == g else zero)
            dim.append(64); dd.append(d); scale.append(b_scale)
    for j in range(128):
        main.append(bk0 + j)
        dim.append(64); dd.append(j % 64); scale.append(1.0)

    plain = []
    for base in (cq0, ck0):
        for h in range(C_HEADS):
            plain += [base + h * C_DK + d if d < C_DK else zero for d in range(LANES)]
    for base in (cv0, cr0):
        for h in range(C_HEADS):
            plain += [base + h * C_DV + d if d < C_DV else zero for d in range(LANES)]
    plain += [cg0 + d if d < 2 * C_RANK else zero for d in range(LANES)]
    vt = list(range(av0, av0 + 256)) + list(range(bv0, bv0 + 128))
    for h in range(C_HEADS):
        vt += [cv0 + h * C_DV + d if d < C_DV else zero for d in range(LANES)]
    return (np.array(main, np.int32), np.array(plain, np.int32), np.array(vt, np.int32),
            np.array(dim), np.array(dd), np.array(scale, np.float32))


_ROPE_COLS, _PLAIN_COLS, _VT_COLS, _R_DIM, _R_D, _R_SCALE = _column_maps()


def _rope_tables(seq, pad_rows):
    sec_start = np.cumsum((0,) + ROPE_WIDTHS[:-1])
    cols = np.concatenate([np.arange(s, s + LANES) for s in sec_start for _ in range(2)])
    is_sin = jnp.asarray(np.tile(np.repeat([False, True], LANES), len(ROPE_WIDTHS)))[None, :]
    r_dim, r_d, r_scale = _R_DIM[cols], _R_D[cols], _R_SCALE[cols]
    quarter = r_dim // 4
    half = r_dim // 2
    is_col = jnp.asarray((r_d % r_dim) >= half)
    ddh = r_d % half
    first = jnp.asarray(ddh < quarter)[None, :]
    f = (ddh % quarter).astype(np.float32)
    inv = jnp.asarray(ROPE_BASE, F32) ** (-jnp.asarray(f) / jnp.asarray(quarter.astype(np.float32)))
    scale = jnp.asarray(r_scale)[None, :]

    def trig(n):
        ang = jnp.arange(n, dtype=F32)[:, None] * inv[None, :]
        return jnp.where(is_sin, jnp.where(first, -jnp.sin(ang), jnp.sin(ang)), jnp.cos(ang)) * scale

    t_row = trig(seq // GRID_W)
    t_col = trig(GRID_W)
    tab = jnp.where(is_col[None, None, :], t_col[None, :, :], t_row[:, None, :]).reshape(seq, -1)
    ident = jnp.broadcast_to(jnp.where(is_sin, 0.0, scale), (pad_rows, tab.shape[1]))
    return jnp.concatenate([tab, ident], axis=0)


def _mod_kernel(c_ref, w_ref, b_ref, o_ref):
    c = c_ref[...]
    s = c * (1.0 / (1.0 + jnp.exp(-c)))
    o_ref[0] = jnp.dot(s, w_ref[0], precision=lax.Precision.HIGHEST,
                       preferred_element_type=F32) + b_ref[0]


def _modulation(cc, ada_w, ada_b):
    depth, d, n = ada_w.shape
    tn = n // 4
    return pl.pallas_call(
        _mod_kernel,
        grid=(depth, n // tn),
        in_specs=[pl.BlockSpec((8, d), lambda l, j: (0, 0)),
                  pl.BlockSpec((1, d, tn), lambda l, j: (l, 0, j)),
                  pl.BlockSpec((1, 1, tn), lambda l, j: (l, 0, j))],
        out_specs=pl.BlockSpec((1, 8, tn), lambda l, j: (l, 0, j)),
        out_shape=jax.ShapeDtypeStruct((depth, 8, n), F32),
        compiler_params=_cparams(("arbitrary", "arbitrary")),
        name="adaln_mod",
    )(cc, ada_w, ada_b.reshape(depth, 1, n))


def _norm_mod(x, g, sc, sh):
    ms = jnp.mean(x * x, axis=-1, keepdims=True)
    return (x * lax.rsqrt(ms + EPS) * g) * (1.0 + sc) + sh


def _normproj_kernel(x_ref, g_ref, sc_ref, sh_ref, w_ref, *rest, rope, widths, vt_widths):
    if rope:
        t_ref = rest[0]
        rest = rest[1:]
    if vt_widths:
        wt_ref = rest[0]
        rest = rest[1:]
    outs = rest[:len(widths)]
    vt_outs = rest[len(widths):]
    h = _norm_mod(x_ref[...], g_ref[...], sc_ref[0], sh_ref[0]).astype(BF16)
    acc = jnp.dot(h, w_ref[...], preferred_element_type=F32)
    lane = lax.broadcasted_iota(jnp.int32, (1, LANES), 1)
    off = 0
    for sec, (o_ref, w) in enumerate(zip(outs, widths)):
        if rope:
            cos = t_ref[:, (2 * sec) * LANES:(2 * sec + 1) * LANES]
            sin = t_ref[:, (2 * sec + 1) * LANES:(2 * sec + 2) * LANES]
            qd = rope[sec]
            first = (lane & (2 * qd - 1)) < qd
            for j in range(w // LANES):
                x = acc[:, off + j * LANES:off + (j + 1) * LANES]
                partner = jnp.where(first, pltpu.roll(x, LANES - qd, axis=1), pltpu.roll(x, qd, axis=1))
                o_ref[:, j * LANES:(j + 1) * LANES] = (x * cos + partner * sin).astype(o_ref.dtype)
        else:
            o_ref[...] = acc[:, off:off + w].astype(o_ref.dtype)
        off += w
    off = 0
    for o_ref, w in zip(vt_outs, vt_widths):
        o_ref[...] = lax.dot_general(wt_ref[off:off + w, :], h, (((1,), (1,)), ((), ())),
                                     preferred_element_type=F32).astype(o_ref.dtype)
        off += w


def _normproj(xs, g, mod, sc_chunk, sh_chunk, w, widths, tm, n_lat_tiles, tiles_per_batch, nb, tables=None,
              wt=None, vt_widths=()):
    t, d = xs.shape
    rope = ROPE_QUARTERS if tables is not None else ()

    def modrow(i):
        return jnp.where(i < n_lat_tiles, i // tiles_per_batch, nb)

    in_specs = [pl.BlockSpec((tm, d), lambda i: (i, 0)),
                pl.BlockSpec((1, d), lambda i: (0, 0)),
                pl.BlockSpec((1, 1, d), lambda i: (modrow(i), 0, sc_chunk)),
                pl.BlockSpec((1, 1, d), lambda i: (modrow(i), 0, sh_chunk)),
                pl.BlockSpec(w.shape, lambda i: (0, 0))]
    args = [xs, g.reshape(1, d), mod, mod, w]
    if rope:
        def tabrow(i):
            return jnp.where(i < n_lat_tiles, i % tiles_per_batch, tiles_per_batch)
        in_specs += [pl.BlockSpec((tm, tables.shape[1]), lambda i: (tabrow(i), 0))]
        args += [tables]
    if vt_widths:
        in_specs += [pl.BlockSpec(wt.shape, lambda i: (0, 0))]
        args += [wt]
    return pl.pallas_call(
        functools.partial(_normproj_kernel, rope=rope, widths=widths, vt_widths=vt_widths),
        grid=(t // tm,),
        in_specs=in_specs,
        out_specs=([pl.BlockSpec((tm, wd), lambda i: (i, 0)) for wd in widths]
                   + [pl.BlockSpec((wd, tm), lambda i: (0, i)) for wd in vt_widths]),
        out_shape=([jax.ShapeDtypeStruct((t, wd), BF16) for wd in widths]
                   + [jax.ShapeDtypeStruct((wd, t), BF16) for wd in vt_widths]),
        compiler_params=_cparams(("parallel",)),
        name="normproj_rope" if rope else "normproj_plain",
    )(*args)


def _diff_attn_kernel(*refs, lam_init, tk, nq):
    i = pl.program_id(2)

    @pl.when(i < nq)
    def _():
        _diff_attn_block(*refs, lam_init=lam_init, has_lat=True, tk=tk)

    @pl.when(i == nq)
    def _():
        _diff_attn_block(*refs, lam_init=lam_init, has_lat=False, tk=tk)


def _diff_attn_block(q_ref, kc_ref, vtc_ref, kl_ref, vtl_ref, lam_ref, g_ref, o_ref, m_scr, acc_scr, mc_scr, s_scr,
                     *, lam_init, has_lat, tk):
    q = q_ref[...]
    lane = lax.broadcasted_iota(jnp.int32, (1, LANES), 1)
    nt = (((1,), (1,)), ((), ()))
    qms = [jnp.where((lane >= i * A_QK) & (lane < (i + 1) * A_QK), q, jnp.zeros_like(q)) for i in range(4)]

    sub = 512

    def scores(k, slot, nk):
        for i in range(4):
            s = lax.dot_general(k, qms[i], nt, preferred_element_type=F32)
            s_scr[slot, i, 0:nk, :] = s
            mc_scr[slot, i] = jnp.max(s, axis=0, keepdims=True)

    def consume(vt_of, slot, nk, first):
        for i in range(4):
            m_cur = mc_scr[slot, i]
            if first:
                m_new = m_cur
            else:
                m_run = m_scr[i]
                m_new = jnp.maximum(m_run, m_cur)
            pv = None
            for t in range(nk // min(sub, nk)):
                w = min(sub, nk)
                p = jnp.exp2(s_scr[slot, i, t * w:(t + 1) * w, :] - m_new).astype(BF16)
                d = jnp.dot(vt_of(i // 2, t, w), p, preferred_element_type=F32)
                pv = d if pv is None else pv + d
            if first:
                acc_scr[i] = pv
            else:
                acc_scr[i] = jnp.exp2(m_run - m_new) * acc_scr[i] + pv
            m_scr[i] = m_new

    nctx = kc_ref.shape[0]
    scores(kc_ref[...], 1, nctx)

    def with_ones(vt):
        return jnp.concatenate([vt, jnp.ones((8, vt.shape[1]), vt.dtype)], axis=0)

    def vt_ctx(hh, t, w):
        return with_ones(vtc_ref[hh * A_V:(hh + 1) * A_V, t * w:(t + 1) * w])

    if not has_lat:
        consume(vt_ctx, 1, nctx, True)
    else:
        n_chunks = kl_ref.shape[0] // tk

        def k_lat(c):
            return kl_ref[pl.ds(pl.multiple_of(c * tk, tk), tk), :]

        def vt_lat(c):
            def get(hh, t, w):
                return with_ones(vtl_ref[hh * A_V:(hh + 1) * A_V, pl.ds(pl.multiple_of(c * tk + t * w, w), w)])
            return get

        scores(k_lat(0), 0, tk)
        consume(vt_ctx, 1, nctx, True)

        def body(j, carry):
            c = 2 * j
            scores(k_lat(c + 1), 1, tk)
            consume(vt_lat(c), 0, tk, False)
            scores(k_lat(c + 2), 0, tk)
            consume(vt_lat(c + 1), 1, tk, False)
            return carry
        pairs = (n_chunks - 1) // 2
        lax.fori_loop(0, pairs, body, 0)
        c_last = 2 * pairs
        if c_last + 1 < n_chunks:
            scores(k_lat(c_last + 1), 1, tk)
            consume(vt_lat(c_last), 0, tk, False)
            consume(vt_lat(c_last + 1), 1, tk, False)
        else:
            consume(vt_lat(c_last), 0, tk, False)

    lp = lam_ref[...]
    lam = (jnp.exp(jnp.sum(lp[0:1] * lp[1:2], axis=1, keepdims=True))
           - jnp.exp(jnp.sum(lp[2:3] * lp[3:4], axis=1, keepdims=True)) + lam_init)
    heads = []
    for hh in range(2):
        maps = []
        for m in range(2):
            acc = acc_scr[hh * 2 + m]
            maps.append(acc[0:A_V] / acc[A_V:A_V + 1])
        oh = maps[0] - lam * maps[1]
        ms = jnp.mean(oh * oh, axis=0, keepdims=True)
        heads.append(oh * lax.rsqrt(ms + EPS) * g_ref[...] * (1.0 - lam_init))
    o_ref[...] = jnp.concatenate(heads, axis=0).T.astype(o_ref.dtype)


def _diff_attn(aq, ak, avt, lam_p, norm_g, lam_init, nb, seq, ctx):
    tq = 256
    assert ctx == tq
    tk = min(1024, seq)
    nq = seq // tq
    ctx_blk0 = nb * seq // ctx

    def qblk(b, i):
        return jnp.where(i < nq, b * nq + i, ctx_blk0 + b)

    return pl.pallas_call(
        functools.partial(_diff_attn_kernel, lam_init=lam_init, tk=tk, nq=nq),
        grid=(nb, 2, nq + 1),
        in_specs=[pl.BlockSpec((tq, LANES), lambda b, p, i: (qblk(b, i), p)),
                  pl.BlockSpec((ctx, LANES), lambda b, p, i: (ctx_blk0 + b, p)),
                  pl.BlockSpec((2 * A_V, ctx), lambda b, p, i: (p, ctx_blk0 + b)),
                  pl.BlockSpec((seq, LANES), lambda b, p, i: (b, p)),
                  pl.BlockSpec((2 * A_V, seq), lambda b, p, i: (p, b)),
                  pl.BlockSpec((4, A_QK), lambda b, p, i: (0, 0)),
                  pl.BlockSpec((A_V, 1), lambda b, p, i: (0, 0))],
        out_specs=pl.BlockSpec((tq, LANES), lambda b, p, i: (qblk(b, i), p)),
        out_shape=jax.ShapeDtypeStruct((aq.shape[0], 2 * LANES), BF16),
        scratch_shapes=[pltpu.VMEM((4, 1, tq), F32), pltpu.VMEM((4, A_V + 8, tq), F32),
                        pltpu.VMEM((2, 4, 1, tq), F32),
                        pltpu.VMEM((2, 4, tk, tq), F32)],
        compiler_params=_cparams(("parallel", "parallel", "arbitrary")),
        name="diff_attn",
    )(aq, ak, avt, ak, avt, lam_p, norm_g.reshape(A_V, 1))


def _win_attn_kernel(*refs, seq, ctx, tq):
    n = pl.program_id(1)

    @pl.when(n < seq // tq)
    def _():
        _win_attn_block(*refs, with_window=True, seq=seq, ctx=ctx, tq=tq)

    @pl.when(n >= seq // tq)
    def _():
        _win_attn_block(*refs, with_window=False, seq=seq, ctx=ctx, tq=tq)


def _win_attn_block(q_ref, kc_ref, vtc_ref, kp_ref, kn0_ref, kn_ref, vtp_ref, vt0_ref, vtn_ref, sink_ref, o_ref,
                    *, with_window, seq, ctx, tq):
    if with_window:
        n = pl.program_id(1)
        k_all = jnp.concatenate([kc_ref[...], kp_ref[...], kn0_ref[...], kn_ref[...]], axis=0)
        vt_all = jnp.concatenate([vtc_ref[...], vtp_ref[...], vt0_ref[...], vtn_ref[...]], axis=1)
        nk = ctx + tq + 2 * BLOCK
        r = lax.broadcasted_iota(jnp.int32, (nk, tq), 0)
        c = lax.broadcasted_iota(jnp.int32, (nk, tq), 1)
        krel = r - (ctx + BLOCK)
        start = n * tq
        in_win = ((jnp.abs(c - krel) <= WINDOW) & (krel >= -start) & (krel < seq - start))
        is_ctx = r < ctx
    else:
        k_all = kc_ref[...]
        vt_all = vtc_ref[...]
    lane = lax.broadcasted_iota(jnp.int32, (1, LANES), 1)
    sink_row = sink_ref[...]
    nt = (((1,), (1,)), ((), ()))
    rep = B_HEADS // B_KV
    outs = []
    for j in range(B_HEADS):
        g = j // rep
        s = lax.dot_general(k_all, q_ref[:, j * LANES:(j + 1) * LANES], nt, preferred_element_type=F32)
        if with_window:
            s = jnp.where(is_ctx, s, jnp.where(in_win, s, NEG))
        sk = jnp.max(jnp.where(lane == j, sink_row, NEG), axis=1, keepdims=True)
        m = jnp.maximum(jnp.max(s, axis=0, keepdims=True), sk)
        p = jnp.exp2(s - m)
        l = jnp.sum(p, axis=0, keepdims=True) + jnp.exp2(sk - m)
        ot = jnp.dot(vt_all, p.astype(BF16), preferred_element_type=F32)
        outs.append(ot[g * B_HD:(g + 1) * B_HD] / l)
    for t in range(B_HEADS // 2):
        pair = jnp.concatenate([outs[2 * t], outs[2 * t + 1]], axis=0)
        o_ref[:, t * LANES:(t + 1) * LANES] = pair.T.astype(o_ref.dtype)


def _win_attn(bq, bk, bvt, sink_row, nb, seq, ctx):
    tq = BLOCK
    per = tq // BLOCK
    nblk = seq // BLOCK
    ntile = seq // tq
    nctx_t = ctx // tq
    ctx_blk0 = nb * seq // ctx
    q_ctx0 = nb * seq // tq
    qw = B_HEADS * LANES

    def qblk(b, n):
        return jnp.where(n < ntile, b * ntile + n, q_ctx0 + b * nctx_t + (n - ntile))

    def lat(n):
        return jnp.minimum(n, ntile - 1)

    def prev(b, n):
        return b * nblk + jnp.maximum(per * lat(n) - 1, 0)

    def cur(b, n):
        return b * ntile + lat(n)

    def nxt(b, n):
        return b * nblk + jnp.minimum(per * lat(n) + per, nblk - 1)

    return pl.pallas_call(
        functools.partial(_win_attn_kernel, seq=seq, ctx=ctx, tq=tq),
        grid=(nb, ntile + nctx_t),
        in_specs=[pl.BlockSpec((tq, qw), lambda b, n: (qblk(b, n), 0)),
                  pl.BlockSpec((ctx, LANES), lambda b, n: (ctx_blk0 + b, 0)),
                  pl.BlockSpec((LANES, ctx), lambda b, n: (0, ctx_blk0 + b)),
                  pl.BlockSpec((BLOCK, LANES), lambda b, n: (prev(b, n), 0)),
                  pl.BlockSpec((tq, LANES), lambda b, n: (cur(b, n), 0)),
                  pl.BlockSpec((BLOCK, LANES), lambda b, n: (nxt(b, n), 0)),
                  pl.BlockSpec((LANES, BLOCK), lambda b, n: (0, prev(b, n))),
                  pl.BlockSpec((LANES, tq), lambda b, n: (0, cur(b, n))),
                  pl.BlockSpec((LANES, BLOCK), lambda b, n: (0, nxt(b, n))),
                  pl.BlockSpec((1, LANES), lambda b, n: (0, 0))],
        out_specs=pl.BlockSpec((tq, B_HEADS * B_HD), lambda b, n: (qblk(b, n), 0)),
        out_shape=jax.ShapeDtypeStruct((bq.shape[0], B_HEADS * B_HD), BF16),
        compiler_params=_cparams(("parallel", "arbitrary")),
        name="win_attn",
    )(bq, bk, bvt, bk, bk, bk, bvt, bvt, bvt, sink_row)


def _split3(x):
    hi = x.astype(BF16)
    r1 = x - hi.astype(F32)
    mid = r1.astype(BF16)
    lo = (r1 - mid.astype(F32)).astype(BF16)
    return hi, mid, lo


def _gla_kernel(*refs, reverse, final, n_sub):
    if final:
        (q_ref, k_ref, v_ref, vt_ref, gl_ref, w2_ref, bg_ref, sin_ref, of_ref, r_ref, ng_ref,
         o_ref, sout_ref, st_ref, qd_scr, ke_scr, dec_scr, o_scr) = refs
    else:
        (q_ref, k_ref, v_ref, vt_ref, gl_ref, w2_ref, bg_ref, sin_ref,
         o_ref, sout_ref, st_ref, qd_scr, ke_scr, dec_scr, o_scr) = refs
    gsz = GLA_GROUP
    nch = gsz // C_CHUNK
    step = pl.program_id(1)

    @pl.when(step == 0)
    def _():
        st_ref[...] = sin_ref[0]

    r = lax.broadcasted_iota(jnp.int32, (gsz, gsz), 0)
    c = lax.broadcasted_iota(jnp.int32, (gsz, gsz), 1)
    same = (r // C_CHUNK) == (c // C_CHUNK)
    tri = same & ((c >= r) if reverse else (c <= r))
    tri_b = jnp.where(tri, 1.0, 0.0).astype(BF16)
    edge = 0 if reverse else C_CHUNK - 1
    nt = (((1,), (1,)), ((), ()))

    for sg in range(n_sub):
        rs = slice(sg * gsz, (sg + 1) * gsz)
        q = q_ref[rs, :].astype(F32)
        k = k_ref[rs, :].astype(F32)
        pre = jnp.dot(gl_ref[rs, :], w2_ref[...], preferred_element_type=F32) + bg_ref[...]
        la = (jnp.minimum(pre, 0.0) - jnp.log(1.0 + jnp.exp(-jnp.abs(pre)))) * (1.0 / C_GATE_NORM)
        parts = _split3(la)
        bcum = sum(jnp.dot(tri_b, p, preferred_element_type=F32) for p in parts)
        btot = jnp.concatenate(
            [jnp.broadcast_to(bcum[ci * C_CHUNK + edge:ci * C_CHUNK + edge + 1], (C_CHUNK, bcum.shape[1]))
             for ci in range(nch)], axis=0)
        qd = (q * (jnp.exp(bcum) * (C_DK ** -0.5))).astype(BF16)
        ki = (k * jnp.exp(-bcum)).astype(BF16)
        qd_scr[rs, :] = qd
        ke_scr[rs, :] = (k * jnp.exp(btot - bcum)).astype(BF16)
        for ci in range(nch):
            dec_scr[sg * nch + ci:sg * nch + ci + 1, :] = jnp.exp(bcum[ci * C_CHUNK + edge:ci * C_CHUNK + edge + 1])
        for h in range(C_HEADS):
            sl = slice(h * LANES, (h + 1) * LANES)
            att = lax.dot_general(qd[:, sl], ki[:, sl], nt, preferred_element_type=F32)
            att = jnp.where(tri, att, 0.0).astype(BF16)
            o_scr[rs, sl] = jnp.dot(att, v_ref[rs, sl], preferred_element_type=F32)

    rowid = lax.broadcasted_iota(jnp.int32, (gsz, LANES), 0) // C_CHUNK
    sub_order = list(range(n_sub))[::-1] if reverse else list(range(n_sub))
    chunk_order = list(range(nch))[::-1] if reverse else list(range(nch))
    for h in range(C_HEADS):
        sl = slice(h * LANES, (h + 1) * LANES)
        st = st_ref[h]
        for sg in sub_order:
            r0 = sg * gsz
            ke = ke_scr[r0:r0 + gsz, sl]
            vth = vt_ref[sl, r0:r0 + gsz]
            for ci in chunk_order:
                rows = slice(r0 + ci * C_CHUNK, r0 + (ci + 1) * C_CHUNK)
                o_scr[rows, sl] += lax.dot_general(qd_scr[rows, sl], st.astype(BF16), nt,
                                                   preferred_element_type=F32)
                ke_c = jnp.where(rowid == ci, ke, jnp.zeros_like(ke))
                upd = jnp.dot(vth, ke_c, preferred_element_type=F32)
                st = dec_scr[sg * nch + ci:sg * nch + ci + 1, sl] * st + upd
        st_ref[h] = st

    if not final:
        o_ref[...] = o_scr[...]
    else:
        for h in range(C_HEADS):
            sl = slice(h * LANES, (h + 1) * LANES)
            o = o_scr[:, sl] + of_ref[:, sl]
            ms = jnp.sum(o * o, axis=1, keepdims=True) * (1.0 / C_DV)
            y = o * lax.rsqrt(ms + EPS) * ng_ref[:, sl]
            rr = r_ref[:, sl].astype(F32)
            o_ref[:, sl] = (y * (rr * (1.0 / (1.0 + jnp.exp(-rr))))).astype(o_ref.dtype)

    @pl.when(step == pl.num_programs(1) - 1)
    def _():
        sout_ref[0] = st_ref[...]


def _gla_dir(cq, ck, cv, cvt, cg, w2p, bgp, st_in, nb, seq, ctx, reverse, latent, fwd_out=None, cr=None,
             ng=None):
    gsz = GLA_GROUP
    assert ctx == gsz
    final = fwd_out is not None
    w = C_HEADS * LANES
    if latent:
        n_sub = 4
        rows = n_sub * gsz
        nt_ = seq // rows

        def blk(b, i):
            return b * nt_ + ((nt_ - 1 - i) if reverse else i)
        sep_blk = blk
        grid = (nb, nt_)
        n_out = nb * seq
    else:
        n_sub = 1
        rows = gsz
        blk0 = nb * seq // gsz

        def blk(b, i):
            return blk0 + b

        def sep_blk(b, i):
            return b
        grid = (nb, 1)
        n_out = nb * ctx
    row_spec = pl.BlockSpec((rows, w), lambda b, i: (blk(b, i), 0))
    out_spec = pl.BlockSpec((rows, w), lambda b, i: (sep_blk(b, i), 0))
    st_spec = pl.BlockSpec((1, C_HEADS, LANES, LANES), lambda b, i: (b, 0, 0, 0))
    in_specs = [row_spec, row_spec, row_spec,
                pl.BlockSpec((w, rows), lambda b, i: (0, blk(b, i))),
                pl.BlockSpec((rows, LANES), lambda b, i: (blk(b, i), 0)),
                pl.BlockSpec((LANES, w), lambda b, i: (0, 0)),
                pl.BlockSpec((1, w), lambda b, i: (0, 0)),
                st_spec]
    args = [cq, ck, cv, cvt, cg, w2p, bgp, st_in]
    if final:
        in_specs += [out_spec, row_spec, pl.BlockSpec((1, w), lambda b, i: (0, 0))]
        args += [fwd_out, cr, ng]
    name = ("gla_bwd" if reverse else "gla_fwd") + ("_lat" if latent else "_ctx")
    return pl.pallas_call(
        functools.partial(_gla_kernel, reverse=reverse, final=final, n_sub=n_sub),
        grid=grid,
        in_specs=in_specs,
        out_specs=[out_spec, st_spec],
        out_shape=[jax.ShapeDtypeStruct((n_out, w), BF16 if final else F32),
                   jax.ShapeDtypeStruct(st_in.shape, F32)],
        scratch_shapes=[pltpu.VMEM((C_HEADS, LANES, LANES), F32),
                        pltpu.VMEM((rows, w), BF16), pltpu.VMEM((rows, w), BF16),
                        pltpu.VMEM((max(8, n_sub * (gsz // C_CHUNK)), w), F32),
                        pltpu.VMEM((rows, w), F32)],
        compiler_params=_cparams(("parallel", "arbitrary")),
        name=name,
    )(*args)


def _outproj_kernel(x_ref, a_ref, b_ref, cl_ref, cc_ref, wa_ref, wb_ref, wc_ref, g1_ref, o_ref, *, n_lat_tiles):
    i = pl.program_id(0)
    y = jnp.dot(a_ref[...], wa_ref[...], preferred_element_type=F32)
    y += jnp.dot(b_ref[...], wb_ref[...], preferred_element_type=F32)

    def finish(c):
        o_ref[...] = x_ref[...] + g1_ref[0] * (y + jnp.dot(c, wc_ref[...], preferred_element_type=F32))

    @pl.when(i < n_lat_tiles)
    def _():
        finish(cl_ref[...])

    @pl.when(i >= n_lat_tiles)
    def _():
        finish(cc_ref[...])


def _outproj(xs, a, b, c_lat, c_ctx, wa, wb, wc, mod, tm, modrow):
    t, d = xs.shape
    n_lat_tiles = c_lat.shape[0] // tm
    assert c_ctx.shape[0] == tm
    return pl.pallas_call(
        functools.partial(_outproj_kernel, n_lat_tiles=n_lat_tiles),
        grid=(t // tm,),
        in_specs=[pl.BlockSpec((tm, d), lambda i: (i, 0)),
                  pl.BlockSpec((tm, a.shape[1]), lambda i: (i, 0)),
                  pl.BlockSpec((tm, b.shape[1]), lambda i: (i, 0)),
                  pl.BlockSpec((tm, c_lat.shape[1]), lambda i: (jnp.minimum(i, n_lat_tiles - 1), 0)),
                  pl.BlockSpec((tm, c_ctx.shape[1]), lambda i: (0, 0)),
                  pl.BlockSpec(wa.shape, lambda i: (0, 0)),
                  pl.BlockSpec(wb.shape, lambda i: (0, 0)),
                  pl.BlockSpec(wc.shape, lambda i: (0, 0)),
                  pl.BlockSpec((1, 1, d), lambda i: (modrow(i), 0, 2))],
        out_specs=pl.BlockSpec((tm, d), lambda i: (i, 0)),
        out_shape=jax.ShapeDtypeStruct((t, d), F32),
        compiler_params=_cparams(("parallel",)),
        name="outproj",
    )(xs, a, b, c_lat, c_ctx, wa, wb, wc, mod)


def _ffn_kernel(x_ref, g_ref, sc_ref, sh_ref, gate_ref, wg_ref, wu_ref, wd_ref, o_ref, h_ref, acc_ref):
    f = pl.program_id(1)

    @pl.when(f == 0)
    def _():
        h_ref[...] = _norm_mod(x_ref[...], g_ref[...], sc_ref[0], sh_ref[0]).astype(BF16)
        acc_ref[...] = jnp.zeros(acc_ref.shape, F32)

    h = h_ref[...]
    a = jnp.dot(h, wg_ref[...].astype(BF16), preferred_element_type=F32)
    u = jnp.dot(h, wu_ref[...].astype(BF16), preferred_element_type=F32)
    act = a * (1.0 / (1.0 + jnp.exp(-a))) * u
    acc_ref[...] += jnp.dot(act.astype(BF16), wd_ref[...].astype(BF16), preferred_element_type=F32)

    @pl.when(f == pl.num_programs(1) - 1)
    def _():
        o_ref[...] = x_ref[...] + gate_ref[0] * acc_ref[...]


def _ffn(xs, g, mod, wg, wu, wd, tm, tf, modrow):
    t, d = xs.shape
    ff = wg.shape[1]
    return pl.pallas_call(
        _ffn_kernel,
        grid=(t // tm, ff // tf),
        in_specs=[pl.BlockSpec((tm, d), lambda i, f: (i, 0)),
                  pl.BlockSpec((1, d), lambda i, f: (0, 0)),
                  pl.BlockSpec((1, 1, d), lambda i, f: (modrow(i), 0, 4)),
                  pl.BlockSpec((1, 1, d), lambda i, f: (modrow(i), 0, 3)),
                  pl.BlockSpec((1, 1, d), lambda i, f: (modrow(i), 0, 5)),
                  pl.BlockSpec((d, tf), lambda i, f: (0, f)),
                  pl.BlockSpec((d, tf), lambda i, f: (0, f)),
                  pl.BlockSpec((tf, d), lambda i, f: (f, 0))],
        out_specs=pl.BlockSpec((tm, d), lambda i, f: (i, 0)),
        out_shape=jax.ShapeDtypeStruct((t, d), F32),
        scratch_shapes=[pltpu.VMEM((tm, d), BF16), pltpu.VMEM((tm, d), F32)],
        compiler_params=_cparams(("parallel", "arbitrary")),
        name="ffn_swiglu",
    )(xs, g.reshape(1, d), mod, mod, mod, wg, wu, wd)


def _router_kernel(x_ref, g_ref, sc_ref, sh_ref, rt_ref, h_ref, r_ref):
    lane = lax.broadcasted_iota(jnp.int32, (1, LANES), 1)
    hf = _norm_mod(x_ref[...], g_ref[...], sc_ref[0], sh_ref[0])
    h_ref[...] = hf
    logits = jnp.dot(hf, rt_ref[...], precision=lax.Precision.HIGHEST, preferred_element_type=F32)
    lanef = lane.astype(F32)
    lg = jnp.where(lane < N_EXPERTS, logits, NEG)
    m1 = jnp.max(lg, axis=1, keepdims=True)
    i1 = jnp.min(jnp.where(lg == m1, lanef, float(LANES)), axis=1, keepdims=True)
    lg2 = jnp.where(lanef == i1, NEG, lg)
    m2 = jnp.max(lg2, axis=1, keepdims=True)
    i2 = jnp.min(jnp.where(lg2 == m2, lanef, float(LANES)), axis=1, keepdims=True)
    e2 = jnp.exp(m2 - m1)
    w1 = 1.0 / (1.0 + e2)
    r_ref[...] = jnp.where(lane == 0, i1, jnp.where(lane == 1, i2, jnp.where(lane == 2, w1,
                           jnp.where(lane == 3, e2 * w1, 0.0))))


def _router(xs, g, mod, router_p, tm, modrow):
    t, d = xs.shape
    return pl.pallas_call(
        _router_kernel,
        grid=(t // tm,),
        in_specs=[pl.BlockSpec((tm, d), lambda i: (i, 0)),
                  pl.BlockSpec((1, d), lambda i: (0, 0)),
                  pl.BlockSpec((1, 1, d), lambda i: (modrow(i), 0, 4)),
                  pl.BlockSpec((1, 1, d), lambda i: (modrow(i), 0, 3)),
                  pl.BlockSpec((d, LANES), lambda i: (0, 0))],
        out_specs=[pl.BlockSpec((tm, d), lambda i: (i, 0)), pl.BlockSpec((tm, LANES), lambda i: (i, 0))],
        out_shape=[jax.ShapeDtypeStruct((t, d), F32), jax.ShapeDtypeStruct((t, LANES), F32)],
        compiler_params=_cparams(("parallel",)),
        name="moe_router",
    )(xs, g.reshape(1, d), mod, mod, router_p)


def _route_plan(rinfo, tg):
    t = rinfo.shape[0]
    n_tiles = -(-2 * t // tg) + N_EXPERTS
    e_flat = jnp.concatenate([rinfo[:, 0], rinfo[:, 1]]).astype(jnp.int32)
    onehot = (e_flat[:, None] == jnp.arange(N_EXPERTS, dtype=jnp.int32)[None, :]).astype(jnp.int32)
    csum = jnp.cumsum(onehot, axis=0)
    rank = jnp.sum(onehot * (csum - 1), axis=1)
    counts = csum[-1]
    padded = (counts + tg - 1) // tg * tg
    ends = jnp.cumsum(padded)
    pos = jnp.sum(onehot * (ends - padded)[None, :], axis=1) + rank
    tile_start = jnp.arange(n_tiles, dtype=jnp.int32) * tg
    tile_expert = jnp.minimum(jnp.sum((tile_start[:, None] >= ends[None, :]).astype(jnp.int32), axis=1),
                              N_EXPERTS - 1)
    n_used = (ends[-1] // tg).reshape(1)
    flat = jnp.arange(2 * t, dtype=jnp.int32)
    order = jnp.sort(e_flat * (2 * t) + flat) % (2 * t)
    unpadded_start = jnp.cumsum(counts) - counts
    row = jnp.arange(n_tiles * tg, dtype=jnp.int32)
    row_e = jnp.repeat(tile_expert, tg)
    r_in = row - (ends - padded)[row_e]
    src = order[jnp.minimum(unpadded_start[row_e] + r_in, 2 * t - 1)] % t
    src_tok = jnp.where(r_in < counts[row_e], src, 0)
    return src_tok, tile_expert, n_used, pos


def _row_copy(src_ref, row, dst_ref, r, sem):
    return pltpu.make_async_copy(src_ref.at[pl.ds(row, 1), :], dst_ref.at[pl.ds(r, 1), :], sem)


def _moe_experts_kernel(te_ref, nu_ref, idx_ref, idxn_ref, x_ref, wg_ref, wu_ref, wd_ref, o_ref,
                        xbuf, h_ref, acc_ref, sems, *, rows_per_step, n_steps):
    i = pl.program_id(0)
    f = pl.program_id(1)
    nf = pl.num_programs(1)
    last = nf - 1
    n_tiles = pl.num_programs(0)
    n_used = nu_ref[0]
    used = i < n_used
    slot = i % 2
    rows = h_ref.shape[0]
    per_step = rows_per_step
    n_dma = per_step * n_steps

    def drain(s):
        def body(r, carry):
            _row_copy(x_ref, 0, xbuf.at[s], r, sems.at[s]).wait()
            return carry
        lax.fori_loop(0, n_dma, body, 0, unroll=n_steps)

    @pl.when((i == 0) & (f == 0))
    def _():
        def body(r, carry):
            _row_copy(x_ref, idx_ref[0, 0, r], xbuf.at[0], r, sems.at[0]).start()
            return carry
        lax.fori_loop(0, n_dma, body, 0, unroll=n_steps)

    @pl.when((i <= n_used) & (f == 0))
    def _():
        drain(slot)

    @pl.when(used & (f == 0))
    def _():
        h_ref[...] = xbuf[slot, 0:rows].astype(BF16)
        acc_ref[...] = jnp.zeros(acc_ref.shape, F32)

    @pl.when(used)
    def _():
        for k in range(per_step):
            r = f * per_step + k
            _row_copy(x_ref, idxn_ref[0, 0, r], xbuf.at[1 - slot], r, sems.at[1 - slot]).start()
        h = h_ref[...]
        a = jnp.dot(h, wg_ref[0, 0].astype(BF16), preferred_element_type=F32)
        u = jnp.dot(h, wu_ref[0, 0].astype(BF16), preferred_element_type=F32)
        act = a * (1.0 / (1.0 + jnp.exp(-a))) * u
        acc_ref[...] += jnp.dot(act.astype(BF16), wd_ref[0, 0].astype(BF16), preferred_element_type=F32)

    @pl.when(used & (f == last))
    def _():
        o_ref[...] = acc_ref[...]

    @pl.when(used & (i == n_tiles - 1) & (f == last))
    def _():
        drain(1 - slot)

    @pl.when(jnp.logical_not(used) & (f == last))
    def _():
        o_ref[...] = jnp.zeros(o_ref.shape, F32)


def _moe_experts(h2, src_tok, tile_expert, n_used, wg, wu, wd, layer, tg, tf):
    d = h2.shape[1]
    n_tiles = src_tok.shape[0] // tg
    ff = wg.shape[3]
    nf = ff // tf
    per_step = -(-tg // nf)
    n_dma = per_step * nf
    idw = -(-n_dma // LANES) * LANES
    ids = jnp.pad(src_tok.reshape(n_tiles, 1, tg), ((0, 0), (0, 0), (0, idw - tg)))
    grid_spec = pltpu.PrefetchScalarGridSpec(
        num_scalar_prefetch=2,
        grid=(n_tiles, nf),
        in_specs=[pl.BlockSpec((1, 1, idw), lambda i, f, te, nu: (i, 0, 0), memory_space=pltpu.SMEM),
                  pl.BlockSpec((1, 1, idw), lambda i, f, te, nu: (jnp.minimum(i + 1, n_tiles - 1), 0, 0),
                               memory_space=pltpu.SMEM),
                  pl.BlockSpec(memory_space=pl.ANY),
                  pl.BlockSpec((1, 1, d, tf), lambda i, f, te, nu: (layer, te[i], 0, f)),
                  pl.BlockSpec((1, 1, d, tf), lambda i, f, te, nu: (layer, te[i], 0, f)),
                  pl.BlockSpec((1, 1, tf, d), lambda i, f, te, nu: (layer, te[i], f, 0))],
        out_specs=pl.BlockSpec((tg, d), lambda i, f, te, nu: (i, 0)),
        scratch_shapes=[pltpu.VMEM((2, -(-n_dma // 8) * 8, d), F32), pltpu.VMEM((tg, d), BF16),
                        pltpu.VMEM((tg, d), F32), pltpu.SemaphoreType.DMA((2,))])
    return pl.pallas_call(
        functools.partial(_moe_experts_kernel, rows_per_step=per_step, n_steps=nf),
        grid_spec=grid_spec,
        out_shape=jax.ShapeDtypeStruct((n_tiles * tg, d), F32),
        compiler_params=_cparams(("arbitrary", "arbitrary")),
        name="moe_experts",
    )(tile_expert, n_used, ids, ids, h2, wg, wu, wd)


def _moe_combine_kernel(i0_ref, i1_ref, n0_ref, n1_ref, y_ref, x_ref, r_ref, gate_ref, o_ref, buf, sems):
    i = pl.program_id(0)
    slot = i % 2
    rows = o_ref.shape[0]

    def fetch(a_ref, b_ref, s):
        def body(r, carry):
            _row_copy(y_ref, a_ref[0, 0, r], buf.at[s, 0], r, sems.at[s]).start()
            _row_copy(y_ref, b_ref[0, 0, r], buf.at[s, 1], r, sems.at[s]).start()
            return carry
        lax.fori_loop(0, rows, body, 0, unroll=8)

    @pl.when(i == 0)
    def _():
        fetch(i0_ref, i1_ref, 0)

    @pl.when(i + 1 < pl.num_programs(0))
    def _():
        fetch(n0_ref, n1_ref, 1 - slot)

    def drain(r, carry):
        _row_copy(y_ref, 0, buf.at[slot, 0], r, sems.at[slot]).wait()
        _row_copy(y_ref, 0, buf.at[slot, 1], r, sems.at[slot]).wait()
        return carry
    lax.fori_loop(0, rows, drain, 0, unroll=8)
    y = r_ref[:, 2:3] * buf[slot, 0] + r_ref[:, 3:4] * buf[slot, 1]
    o_ref[...] = x_ref[...] + gate_ref[0] * y


def _moe_combine(xs, yg, pos, rinfo, mod, tm, modrow):
    t, d = xs.shape
    nt = t // tm
    p3 = pos.reshape(2, nt, 1, tm)

    def cur(i):
        return (i, 0, 0)

    def nxt(i):
        return (jnp.minimum(i + 1, nt - 1), 0, 0)

    return pl.pallas_call(
        _moe_combine_kernel,
        grid=(nt,),
        in_specs=[pl.BlockSpec((1, 1, tm), cur, memory_space=pltpu.SMEM),
                  pl.BlockSpec((1, 1, tm), cur, memory_space=pltpu.SMEM),
                  pl.BlockSpec((1, 1, tm), nxt, memory_space=pltpu.SMEM),
                  pl.BlockSpec((1, 1, tm), nxt, memory_space=pltpu.SMEM),
                  pl.BlockSpec(memory_space=pl.ANY),
                  pl.BlockSpec((tm, d), lambda i: (i, 0)),
                  pl.BlockSpec((tm, LANES), lambda i: (i, 0)),
                  pl.BlockSpec((1, 1, d), lambda i: (modrow(i), 0, 5))],
        out_specs=pl.BlockSpec((tm, d), lambda i: (i, 0)),
        out_shape=jax.ShapeDtypeStruct((t, d), F32),
        scratch_shapes=[pltpu.VMEM((2, 2, tm, d), F32), pltpu.SemaphoreType.DMA((2,))],
        compiler_params=_cparams(("arbitrary",)),
        name="moe_combine",
    )(p3[0], p3[1], p3[0], p3[1], yg, xs, rinfo, mod)


def _moe(xs, g, mod, router_p, wg, wu, wd, layer, tm, tf, modrow):
    tg = tm
    h2, rinfo = _router(xs, g, mod, router_p, tm, modrow)
    src_tok, tile_expert, n_used, pos = _route_plan(rinfo, tg)
    yg = _moe_experts(h2, src_tok, tile_expert, n_used, wg, wu, wd, layer, tg, tf)
    return _moe_combine(xs, yg, pos, rinfo, mod, tm, modrow)


def _final_norm_kernel(x_ref, g_ref, o_ref):
    x = x_ref[...]
    ms = jnp.mean(x * x, axis=-1, keepdims=True)
    o_ref[...] = x * lax.rsqrt(ms + EPS) * g_ref[...]


def _final_norm(xs, g, rows, tm):
    d = xs.shape[1]
    return pl.pallas_call(
        _final_norm_kernel,
        grid=(rows // tm,),
        in_specs=[pl.BlockSpec((tm, d), lambda i: (i, 0)), pl.BlockSpec((1, d), lambda i: (0, 0))],
        out_specs=pl.BlockSpec((tm, d), lambda i: (i, 0)),
        out_shape=jax.ShapeDtypeStruct((rows, d), F32),
        compiler_params=_cparams(("parallel",)),
        name="final_norm",
    )(xs, g.reshape(1, d))


def _pad_heads(v, width, used):
    lead = v.shape[:-1]
    v = v.reshape(lead + (C_HEADS, used))
    v = jnp.pad(v, [(0, 0)] * len(lead) + [(0, 0), (0, width - used)])
    return v.reshape(lead + (C_HEADS * width,))


def kernel(x, c, ctx, c_ctx, norm1_g, norm2_g, ada_w, ada_b, w_in, w_out, a_lambda, a_norm_g, b_sink,
           c_gate_w2, c_gate_b, c_norm_g, ffn_w_gate, ffn_w_up, ffn_w_down, moe_router, moe_w_gate,
           moe_w_up, moe_w_down, final_g):
    nb, seq, d = x.shape
    nctx = ctx.shape[1]
    depth = w_in.shape[0]
    n_lat = nb * seq
    tm = nb * nctx
    assert seq % tm == 0 and nb < 8
    tm_r = tm // 2
    n_lat_tiles = n_lat // tm

    def modrow(i):
        return jnp.where(i < n_lat_tiles, i // (seq // tm), nb)

    xs = jnp.concatenate([x.reshape(n_lat, d), ctx.reshape(nb * nctx, d)], axis=0)
    cc = jnp.zeros((8, d), F32).at[:nb].set(c).at[nb].set(c_ctx)
    mod_all = _modulation(cc, ada_w, ada_b).reshape(depth, 8, 1, 6 * d)
    tables = _rope_tables(seq, tm_r)

    for layer in range(depth):
        lam_init = 0.8 - 0.6 * math.exp(-0.3 * layer)
        mod = mod_all[layer]
        w_pad = jnp.concatenate([w_in[layer], jnp.zeros((d, 1), F32)], axis=1)
        w_rope = jnp.take(w_pad, _ROPE_COLS, axis=1).astype(BF16)
        w_plain = jnp.take(w_pad, _PLAIN_COLS, axis=1).astype(BF16)
        w_vt = jnp.take(w_pad, _VT_COLS, axis=1).T.astype(BF16)

        aq, ak, bq, bk = _normproj(xs, norm1_g[layer], mod, 1, 0, w_rope, ROPE_WIDTHS, tm_r,
                                   n_lat // tm_r, seq // tm_r, nb, tables)
        cq, ck, cv, cr, cg, avt, bvt, cvt = _normproj(xs, norm1_g[layer], mod, 1, 0, w_plain, PLAIN_WIDTHS, tm,
                                                      n_lat_tiles, seq // tm, nb, wt=w_vt, vt_widths=VT_WIDTHS)

        a_all = _diff_attn(aq, ak, avt, a_lambda[layer], a_norm_g[layer], lam_init, nb, seq, nctx)
        sink_row = jnp.zeros((1, LANES), F32).at[0, :B_HEADS].set(b_sink[layer] * LOG2E)
        b_all = _win_attn(bq, bk, bvt, sink_row, nb, seq, nctx)

        w2 = c_gate_w2[layer]
        w2p = [jnp.zeros((LANES, C_HEADS * LANES), F32).at[dd * C_RANK:(dd + 1) * C_RANK].set(
            _pad_heads(w2[dd], LANES, C_DK)).astype(BF16) for dd in range(2)]
        bgp = [_pad_heads(c_gate_b[layer, dd], LANES, C_DK).reshape(1, -1) for dd in range(2)]
        ng = _pad_heads(jnp.tile(c_norm_g[layer], C_HEADS), LANES, C_DV).reshape(1, -1)
        st0 = jnp.zeros((nb, C_HEADS, LANES, LANES), F32)
        gla_in = (cq, ck, cv, cvt, cg)
        of_ctx, st_f = _gla_dir(*gla_in, w2p[0], bgp[0], st0, nb, seq, nctx, False, False)
        of_lat, _ = _gla_dir(*gla_in, w2p[0], bgp[0], st_f, nb, seq, nctx, False, True)
        g_ctx, st_b = _gla_dir(*gla_in, w2p[1], bgp[1], st0, nb, seq, nctx, True, False, of_ctx, cr, ng)
        g_lat, _ = _gla_dir(*gla_in, w2p[1], bgp[1], st_b, nb, seq, nctx, True, True, of_lat, cr, ng)

        wo = w_out[layer]
        wa = wo[:256].astype(BF16)
        wb = wo[256:640].astype(BF16)
        wc = jnp.pad(wo[640:].reshape(C_HEADS, C_DV, d), ((0, 0), (0, LANES - C_DV), (0, 0))).reshape(
            C_HEADS * LANES, d).astype(BF16)
        xs = _outproj(xs, a_all, b_all, g_lat, g_ctx, wa, wb, wc, mod, tm, modrow)

        j = layer // 2
        if layer % 2 == 0:
            xs = _ffn(xs, norm2_g[layer], mod, ffn_w_gate[j], ffn_w_up[j], ffn_w_down[j], tm, 256, modrow)
        else:
            router_p = jnp.pad(moe_router[j], ((0, 0), (0, LANES - N_EXPERTS)))
            xs = _moe(xs, norm2_g[layer], mod, router_p, moe_w_gate, moe_w_up, moe_w_down, j,
                      tm, 512, modrow)

    return _final_norm(xs, final_g, n_lat, tm).reshape(nb, seq, d)
```

```python
import functools
import math

import numpy as np
import jax
import jax.numpy as jnp
from jax import lax
from jax.experimental import pallas as pl
from jax.experimental.pallas import tpu as pltpu

F32 = jnp.float32
BF16 = jnp.bfloat16

EPS = 1e-6
ROPE_BASE = 10000.0
GRID_W = 64
LANES = 128
LOG2E = math.log2(math.e)
NEG = -1e30

A_HEADS, A_QK, A_V = 4, 32, 64
B_HEADS, B_KV, B_HD, WINDOW, BLOCK = 6, 2, 64, 128, 128
C_HEADS, C_DK, C_DV, C_RANK, C_GATE_NORM, C_CHUNK = 4, 48, 96, 16, 16.0, 64
N_EXPERTS = 8
IN_SIZES = (256, 256, 256, 384, 128, 128, 192, 192, 384, 384, 32)
IN_W = sum(IN_SIZES)

ROPE_WIDTHS = (256, 256, 768, 128)
PLAIN_WIDTHS = (512, 512, 512, 512, 128)
VT_WIDTHS = (256, 128, 512)
ROPE_W = sum(ROPE_WIDTHS)
ROPE_QUARTERS = (A_QK // 4, A_QK // 4, B_HD // 4, B_HD // 4)
GLA_GROUP = 256
VMEM_LIMIT = 48 * 1024 * 1024


def _cparams(sem):
    return pltpu.CompilerParams(dimension_semantics=sem, vmem_limit_bytes=VMEM_LIMIT)


def _column_maps():
    off = np.concatenate([[0], np.cumsum(IN_SIZES)])
    aq0, ak0, av0, bq0, bk0, bv0, cq0, ck0, cv0, cr0, cg0 = [int(v) for v in off[:11]]
    zero = IN_W

    main, dim, dd, scale = [], [], [], []
    a_scale = A_QK ** -0.5 * LOG2E
    b_scale = B_HD ** -0.5 * LOG2E
    for base, sc in ((aq0, a_scale), (ak0, 1.0)):
        for j in range(256):
            main.append(base + j)
            dim.append(32); dd.append(j % 32); scale.append(sc)
    for t in range(B_HEADS):
        g = t // (B_HEADS // B_KV)
        for lane in range(LANES):
            d = lane % 64
            main.append(bq0 + t * 64 + d if lane // 64 == g else zero)
            dim.append(64); dd.append(d); scale.append(b_scale)
    for j in range(128):
        main.append(bk0 + j)
        dim.append(64); dd.append(j % 64); scale.append(1.0)

    plain = []
    for base in (cq0, ck0):
        for h in range(C_HEADS):
            plain += [base + h * C_DK + d if d < C_DK else zero for d in range(LANES)]
    for base in (cv0, cr0):
        for h in range(C_HEADS):
            plain += [base + h * C_DV + d if d < C_DV else zero for d in range(LANES)]
    plain += [cg0 + d if d < 2 * C_RANK else zero for d in range(LANES)]
    vt = list(range(av0, av0 + 256)) + list(range(bv0, bv0 + 128))
    for h in range(C_HEADS):
        vt += [cv0 + h * C_DV + d if d < C_DV else zero for d in range(LANES)]
    return (np.array(main, np.int32), np.array(plain, np.int32), np.array(vt, np.int32),
            np.array(dim), np.array(dd), np.array(scale, np.float32))


_ROPE_COLS, _PLAIN_COLS, _VT_COLS, _R_DIM, _R_D, _R_SCALE = _column_maps()


def _rope_tables(seq, pad_rows):
    sec_start = np.cumsum((0,) + ROPE_WIDTHS[:-1])
    cols = np.concatenate([np.arange(s, s + LANES) for s in sec_start for _ in range(2)])
    is_sin = jnp.asarray(np.tile(np.repeat([False, True], LANES), len(ROPE_WIDTHS)))[None, :]
    r_dim, r_d, r_scale = _R_DIM[cols], _R_D[cols], _R_SCALE[cols]
    quarter = r_dim // 4
    half = r_dim // 2
    is_col = jnp.asarray((r_d % r_dim) >= half)
    ddh = r_d % half
    first = jnp.asarray(ddh < quarter)[None, :]
    f = (ddh % quarter).astype(np.float32)
    inv = jnp.asarray(ROPE_BASE, F32) ** (-jnp.asarray(f) / jnp.asarray(quarter.astype(np.float32)))
    scale = jnp.asarray(r_scale)[None, :]

    def trig(n):
        ang = jnp.arange(n, dtype=F32)[:, None] * inv[None, :]
        return jnp.where(is_sin, jnp.where(first, -jnp.sin(ang), jnp.sin(ang)), jnp.cos(ang)) * scale

    t_row = trig(seq // GRID_W)
    t_col = trig(GRID_W)
    tab = jnp.where(is_col[None, None, :], t_col[None, :, :], t_row[:, None, :]).reshape(seq, -1)
    ident = jnp.broadcast_to(jnp.where(is_sin, 0.0, scale), (pad_rows, tab.shape[1]))
    return jnp.concatenate([tab, ident], axis=0)


def _mod_kernel(c_ref, w_ref, b_ref, o_ref):
    c = c_ref[...]
    s = c * (1.0 / (1.0 + jnp.exp(-c)))
    o_ref[0] = jnp.dot(s, w_ref[0], precision=lax.Precision.HIGHEST,
                       preferred_element_type=F32) + b_ref[0]


def _modulation(cc, ada_w, ada_b):
    depth, d, n = ada_w.shape
    tn = n // 4
    return pl.pallas_call(
        _mod_kernel,
        grid=(depth, n // tn),
        in_specs=[pl.BlockSpec((8, d), lambda l, j: (0, 0)),
                  pl.BlockSpec((1, d, tn), lambda l, j: (l, 0, j)),
                  pl.BlockSpec((1, 1, tn), lambda l, j: (l, 0, j))],
        out_specs=pl.BlockSpec((1, 8, tn), lambda l, j: (l, 0, j)),
        out_shape=jax.ShapeDtypeStruct((depth, 8, n), F32),
        compiler_params=_cparams(("arbitrary", "arbitrary")),
        name="adaln_mod",
    )(cc, ada_w, ada_b.reshape(depth, 1, n))


def _norm_mod(x, g, sc, sh):
    ms = jnp.mean(x * x, axis=-1, keepdims=True)
    return (x * lax.rsqrt(ms + EPS) * g) * (1.0 + sc) + sh


def _normproj_kernel(x_ref, g_ref, sc_ref, sh_ref, w_ref, *rest, rope, widths, vt_widths):
    if rope:
        t_ref = rest[0]
        rest = rest[1:]
    if vt_widths:
        wt_ref = rest[0]
        rest = rest[1:]
    outs = rest[:len(widths)]
    vt_outs = rest[len(widths):]
    h = _norm_mod(x_ref[...], g_ref[...], sc_ref[0], sh_ref[0]).astype(BF16)
    acc = jnp.dot(h, w_ref[...], preferred_element_type=F32)
    lane = lax.broadcasted_iota(jnp.int32, (1, LANES), 1)
    off = 0
    for sec, (o_ref, w) in enumerate(zip(outs, widths)):
        if rope:
            cos = t_ref[:, (2 * sec) * LANES:(2 * sec + 1) * LANES]
            sin = t_ref[:, (2 * sec + 1) * LANES:(2 * sec + 2) * LANES]
            qd = rope[sec]
            first = (lane & (2 * qd - 1)) < qd
            for j in range(w // LANES):
                x = acc[:, off + j * LANES:off + (j + 1) * LANES]
                partner = jnp.where(first, pltpu.roll(x, LANES - qd, axis=1), pltpu.roll(x, qd, axis=1))
                o_ref[:, j * LANES:(j + 1) * LANES] = (x * cos + partner * sin).astype(o_ref.dtype)
        else:
            o_ref[...] = acc[:, off:off + w].astype(o_ref.dtype)
        off += w
    off = 0
    for o_ref, w in zip(vt_outs, vt_widths):
        o_ref[...] = lax.dot_general(wt_ref[off:off + w, :], h, (((1,), (1,)), ((), ())),
                                     preferred_element_type=F32).astype(o_ref.dtype)
        off += w


def _normproj(xs, g, mod, sc_chunk, sh_chunk, w, widths, tm, n_lat_tiles, tiles_per_batch, nb, tables=None,
              wt=None, vt_widths=()):
    t, d = xs.shape
    rope = ROPE_QUARTERS if tables is not None else ()

    def modrow(i):
        return jnp.where(i < n_lat_tiles, i // tiles_per_batch, nb)

    in_specs = [pl.BlockSpec((tm, d), lambda i: (i, 0)),
                pl.BlockSpec((1, d), lambda i: (0, 0)),
                pl.BlockSpec((1, 1, d), lambda i: (modrow(i), 0, sc_chunk)),
                pl.BlockSpec((1, 1, d), lambda i: (modrow(i), 0, sh_chunk)),
                pl.BlockSpec(w.shape, lambda i: (0, 0))]
    args = [xs, g.reshape(1, d), mod, mod, w]
    if rope:
        def tabrow(i):
            return jnp.where(i < n_lat_tiles, i % tiles_per_batch, tiles_per_batch)
        in_specs += [pl.BlockSpec((tm, tables.shape[1]), lambda i: (tabrow(i), 0))]
        args += [tables]
    if vt_widths:
        in_specs += [pl.BlockSpec(wt.shape, lambda i: (0, 0))]
        args += [wt]
    return pl.pallas_call(
        functools.partial(_normproj_kernel, rope=rope, widths=widths, vt_widths=vt_widths),
        grid=(t // tm,),
        in_specs=in_specs,
        out_specs=([pl.BlockSpec((tm, wd), lambda i: (i, 0)) for wd in widths]
                   + [pl.BlockSpec((wd, tm), lambda i: (0, i)) for wd in vt_widths]),
        out_shape=([jax.ShapeDtypeStruct((t, wd), BF16) for wd in widths]
                   + [jax.ShapeDtypeStruct((wd, t), BF16) for wd in vt_widths]),
        compiler_params=_cparams(("parallel",)),
        name="normproj_rope" if rope else "normproj_plain",
    )(*args)


def _diff_attn_kernel(*refs, lam_init, tk, nq):
    i = pl.program_id(2)

    @pl.when(i < nq)
    def _():
        _diff_attn_block(*refs, lam_init=lam_init, has_lat=True, tk=tk)

    @pl.when(i == nq)
    def _():
        _diff_attn_block(*refs, lam_init=lam_init, has_lat=False, tk=tk)


def _diff_attn_block(q_ref, kc_ref, vtc_ref, kl_ref, vtl_ref, lam_ref, g_ref, o_ref, m_scr, acc_scr, mc_scr, s_scr,
                     *, lam_init, has_lat, tk):
    q = q_ref[...]
    lane = lax.broadcasted_iota(jnp.int32, (1, LANES), 1)
    nt = (((1,), (1,)), ((), ()))
    qms = [jnp.where((lane >= i * A_QK) & (lane < (i + 1) * A_QK), q, jnp.zeros_like(q)) for i in range(4)]

    sub = 512

    def scores(k, slot, nk):
        for i in range(4):
            s = lax.dot_general(k, qms[i], nt, preferred_element_type=F32)
            s_scr[slot, i, 0:nk, :] = s
            mc_scr[slot, i] = jnp.max(s, axis=0, keepdims=True)

    def consume(vt_of, slot, nk, first):
        for i in range(4):
            m_cur = mc_scr[slot, i]
            if first:
                m_new = m_cur
            else:
                m_run = m_scr[i]
                m_new = jnp.maximum(m_run, m_cur)
            pv = None
            for t in range(nk // min(sub, nk)):
                w = min(sub, nk)
                p = jnp.exp2(s_scr[slot, i, t * w:(t + 1) * w, :] - m_new).astype(BF16)
                d = jnp.dot(vt_of(i // 2, t, w), p, preferred_element_type=F32)
                pv = d if pv is None else pv + d
            if first:
                acc_scr[i] = pv
            else:
                acc_scr[i] = jnp.exp2(m_run - m_new) * acc_scr[i] + pv
            m_scr[i] = m_new

    nctx = kc_ref.shape[0]
    scores(kc_ref[...], 1, nctx)

    def with_ones(vt):
        return jnp.concatenate([vt, jnp.ones((8, vt.shape[1]), vt.dtype)], axis=0)

    def vt_ctx(hh, t, w):
        return with_ones(vtc_ref[hh * A_V:(hh + 1) * A_V, t * w:(t + 1) * w])

    if not has_lat:
        consume(vt_ctx, 1, nctx, True)
    else:
        n_chunks = kl_ref.shape[0] // tk

        def k_lat(c):
            return kl_ref[pl.ds(pl.multiple_of(c * tk, tk), tk), :]

        def vt_lat(c):
            def get(hh, t, w):
                return with_ones(vtl_ref[hh * A_V:(hh + 1) * A_V, pl.ds(pl.multiple_of(c * tk + t * w, w), w)])
            return get

        scores(k_lat(0), 0, tk)
        consume(vt_ctx, 1, nctx, True)

        def body(j, carry):
            c = 2 * j
            scores(k_lat(c + 1), 1, tk)
            consume(vt_lat(c), 0, tk, False)
            scores(k_lat(c + 2), 0, tk)
            consume(vt_lat(c + 1), 1, tk, False)
            return carry
        pairs = (n_chunks - 1) // 2
        lax.fori_loop(0, pairs, body, 0)
        c_last = 2 * pairs
        if c_last + 1 < n_chunks:
            scores(k_lat(c_last + 1), 1, tk)
            consume(vt_lat(c_last), 0, tk, False)
            consume(vt_lat(c_last + 1), 1, tk, False)
        else:
            consume(vt_lat(c_last), 0, tk, False)

    lp = lam_ref[...]
    lam = (jnp.exp(jnp.sum(lp[0:1] * lp[1:2], axis=1, keepdims=True))
           - jnp.exp(jnp.sum(lp[2:3] * lp[3:4], axis=1, keepdims=True)) + lam_init)
    heads = []
    for hh in range(2):
        maps = []
        for m in range(2):
            acc = acc_scr[hh * 2 + m]
            maps.append(acc[0:A_V] / acc[A_V:A_V + 1])
        oh = maps[0] - lam * maps[1]
        ms = jnp.mean(oh * oh, axis=0, keepdims=True)
        heads.append(oh * lax.rsqrt(ms + EPS) * g_ref[...] * (1.0 - lam_init))
    o_ref[...] = jnp.concatenate(heads, axis=0).T.astype(o_ref.dtype)


def _diff_attn(aq, ak, avt, lam_p, norm_g, lam_init, nb, seq, ctx):
    tq = 256
    assert ctx == tq
    tk = min(1024, seq)
    nq = seq // tq
    ctx_blk0 = nb * seq // ctx

    def qblk(b, i):
        return jnp.where(i < nq, b * nq + i, ctx_blk0 + b)

    return pl.pallas_call(
        functools.partial(_diff_attn_kernel, lam_init=lam_init, tk=tk, nq=nq),
        grid=(nb, 2, nq + 1),
        in_specs=[pl.BlockSpec((tq, LANES), lambda b, p, i: (qblk(b, i), p)),
                  pl.BlockSpec((ctx, LANES), lambda b, p, i: (ctx_blk0 + b, p)),
                  pl.BlockSpec((2 * A_V, ctx), lambda b, p, i: (p, ctx_blk0 + b)),
                  pl.BlockSpec((seq, LANES), lambda b, p, i: (b, p)),
                  pl.BlockSpec((2 * A_V, seq), lambda b, p, i: (p, b)),
                  pl.BlockSpec((4, A_QK), lambda b, p, i: (0, 0)),
                  pl.BlockSpec((A_V, 1), lambda b, p, i: (0, 0))],
        out_specs=pl.BlockSpec((tq, LANES), lambda b, p, i: (qblk(b, i), p)),
        out_shape=jax.ShapeDtypeStruct((aq.shape[0], 2 * LANES), BF16),
        scratch_shapes=[pltpu.VMEM((4, 1, tq), F32), pltpu.VMEM((4, A_V + 8, tq), F32),
                        pltpu.VMEM((2, 4, 1, tq), F32),
                        pltpu.VMEM((2, 4, tk, tq), F32)],
        compiler_params=_cparams(("parallel", "parallel", "arbitrary")),
        name="diff_attn",
    )(aq, ak, avt, ak, avt, lam_p, norm_g.reshape(A_V, 1))


def _win_attn_kernel(*refs, seq, ctx, tq):
    n = pl.program_id(1)

    @pl.when(n < seq // tq)
    def _():
        _win_attn_block(*refs, with_window=True, seq=seq, ctx=ctx, tq=tq)

    @pl.when(n >= seq // tq)
    def _():
        _win_attn_block(*refs, with_window=False, seq=seq, ctx=ctx, tq=tq)


def _win_attn_block(q_ref, kc_ref, vtc_ref, kp_ref, kn0_ref, kn_ref, vtp_ref, vt0_ref, vtn_ref, sink_ref, o_ref,
                    *, with_window, seq, ctx, tq):
    if with_window:
        n = pl.program_id(1)
        k_all = jnp.concatenate([kc_ref[...], kp_ref[...], kn0_ref[...], kn_ref[...]], axis=0)
        vt_all = jnp.concatenate([vtc_ref[...], vtp_ref[...], vt0_ref[...], vtn_ref[...]], axis=1)
        nk = ctx + tq + 2 * BLOCK
        r = lax.broadcasted_iota(jnp.int32, (nk, tq), 0)
        c = lax.broadcasted_iota(jnp.int32, (nk, tq), 1)
        krel = r - (ctx + BLOCK)
        start = n * tq
        in_win = ((jnp.abs(c - krel) <= WINDOW) & (krel >= -start) & (krel < seq - start))
        is_ctx = r < ctx
    else:
        k_all = kc_ref[...]
        vt_all = vtc_ref[...]
    lane = lax.broadcasted_iota(jnp.int32, (1, LANES), 1)
    sink_row = sink_ref[...]
    nt = (((1,), (1,)), ((), ()))
    rep = B_HEADS // B_KV
    outs = []
    for j in range(B_HEADS):
        g = j // rep
        s = lax.dot_general(k_all, q_ref[:, j * LANES:(j + 1) * LANES], nt, preferred_element_type=F32)
        if with_window:
            s = jnp.where(is_ctx, s, jnp.where(in_win, s, NEG))
        sk = jnp.max(jnp.where(lane == j, sink_row, NEG), axis=1, keepdims=True)
        m = jnp.maximum(jnp.max(s, axis=0, keepdims=True), sk)
        p = jnp.exp2(s - m)
        l = jnp.sum(p, axis=0, keepdims=True) + jnp.exp2(sk - m)
        ot = jnp.dot(vt_all, p.astype(BF16), preferred_element_type=F32)
        outs.append(ot[g * B_HD:(g + 1) * B_HD] / l)
    for t in range(B_HEADS // 2):
        pair = jnp.concatenate([outs[2 * t], outs[2 * t + 1]], axis=0)
        o_ref[:, t * LANES:(t + 1) * LANES] = pair.T.astype(o_ref.dtype)


def _win_attn(bq, bk, bvt, sink_row, nb, seq, ctx):
    tq = BLOCK
    per = tq // BLOCK
    nblk = seq // BLOCK
    ntile = seq // tq
    nctx_t = ctx // tq
    ctx_blk0 = nb * seq // ctx
    q_ctx0 = nb * seq // tq
    qw = B_HEADS * LANES

    def qblk(b, n):
        return jnp.where(n < ntile, b * ntile + n, q_ctx0 + b * nctx_t + (n - ntile))

    def lat(n):
        return jnp.minimum(n, ntile - 1)

    def prev(b, n):
        return b * nblk + jnp.maximum(per * lat(n) - 1, 0)

    def cur(b, n):
        return b * ntile + lat(n)

    def nxt(b, n):
        return b * nblk + jnp.minimum(per * lat(n) + per, nblk - 1)

    return pl.pallas_call(
        functools.partial(_win_attn_kernel, seq=seq, ctx=ctx, tq=tq),
        grid=(nb, ntile + nctx_t),
        in_specs=[pl.BlockSpec((tq, qw), lambda b, n: (qblk(b, n), 0)),
                  pl.BlockSpec((ctx, LANES), lambda b, n: (ctx_blk0 + b, 0)),
                  pl.BlockSpec((LANES, ctx), lambda b, n: (0, ctx_blk0 + b)),
                  pl.BlockSpec((BLOCK, LANES), lambda b, n: (prev(b, n), 0)),
                  pl.BlockSpec((tq, LANES), lambda b, n: (cur(b, n), 0)),
                  pl.BlockSpec((BLOCK, LANES), lambda b, n: (nxt(b, n), 0)),
                  pl.BlockSpec((LANES, BLOCK), lambda b, n: (0, prev(b, n))),
                  pl.BlockSpec((LANES, tq), lambda b, n: (0, cur(b, n))),
                  pl.BlockSpec((LANES, BLOCK), lambda b, n: (0, nxt(b, n))),
                  pl.BlockSpec((1, LANES), lambda b, n: (0, 0))],
        out_specs=pl.BlockSpec((tq, B_HEADS * B_HD), lambda b, n: (qblk(b, n), 0)),
        out_shape=jax.ShapeDtypeStruct((bq.shape[0], B_HEADS * B_HD), BF16),
        compiler_params=_cparams(("parallel", "arbitrary")),
        name="win_attn",
    )(bq, bk, bvt, bk, bk, bk, bvt, bvt, bvt, sink_row)


def _split3(x):
    hi = x.astype(BF16)
    r1 = x - hi.astype(F32)
    mid = r1.astype(BF16)
    lo = (r1 - mid.astype(F32)).astype(BF16)
    return hi, mid, lo


def _gla_kernel(*refs, reverse, final, n_sub):
    if final:
        (q_ref, k_ref, v_ref, vt_ref, gl_ref, w2_ref, bg_ref, sin_ref, of_ref, r_ref, ng_ref,
         o_ref, sout_ref, st_ref, qd_scr, ke_scr, dec_scr, o_scr) = refs
    else:
        (q_ref, k_ref, v_ref, vt_ref, gl_ref, w2_ref, bg_ref, sin_ref,
         o_ref, sout_ref, st_ref, qd_scr, ke_scr, dec_scr, o_scr) = refs
    gsz = GLA_GROUP
    nch = gsz // C_CHUNK
    step = pl.program_id(1)

    @pl.when(step == 0)
    def _():
        st_ref[...] = sin_ref[0]

    r = lax.broadcasted_iota(jnp.int32, (gsz, gsz), 0)
    c = lax.broadcasted_iota(jnp.int32, (gsz, gsz), 1)
    same = (r // C_CHUNK) == (c // C_CHUNK)
    tri = same & ((c >= r) if reverse else (c <= r))
    tri_b = jnp.where(tri, 1.0, 0.0).astype(BF16)
    edge = 0 if reverse else C_CHUNK - 1
    nt = (((1,), (1,)), ((), ()))

    for sg in range(n_sub):
        rs = slice(sg * gsz, (sg + 1) * gsz)
        q = q_ref[rs, :].astype(F32)
        k = k_ref[rs, :].astype(F32)
        pre = jnp.dot(gl_ref[rs, :], w2_ref[...], preferred_element_type=F32) + bg_ref[...]
        la = (jnp.minimum(pre, 0.0) - jnp.log(1.0 + jnp.exp(-jnp.abs(pre)))) * (1.0 / C_GATE_NORM)
        parts = _split3(la)
        bcum = sum(jnp.dot(tri_b, p, preferred_element_type=F32) for p in parts)
        btot = jnp.concatenate(
            [jnp.broadcast_to(bcum[ci * C_CHUNK + edge:ci * C_CHUNK + edge + 1], (C_CHUNK, bcum.shape[1]))
             for ci in range(nch)], axis=0)
        qd = (q * (jnp.exp(bcum) * (C_DK ** -0.5))).astype(BF16)
        ki = (k * jnp.exp(-bcum)).astype(BF16)
        qd_scr[rs, :] = qd
        ke_scr[rs, :] = (k * jnp.exp(btot - bcum)).astype(BF16)
        for ci in range(nch):
            dec_scr[sg * nch + ci:sg * nch + ci + 1, :] = jnp.exp(bcum[ci * C_CHUNK + edge:ci * C_CHUNK + edge + 1])
        for h in range(C_HEADS):
            sl = slice(h * LANES, (h + 1) * LANES)
            att = lax.dot_general(qd[:, sl], ki[:, sl], nt, preferred_element_type=F32)
            att = jnp.where(tri, att, 0.0).astype(BF16)
            o_scr[rs, sl] = jnp.dot(att, v_ref[rs, sl], preferred_element_type=F32)

    rowid = lax.broadcasted_iota(jnp.int32, (gsz, LANES), 0) // C_CHUNK
    sub_order = list(range(n_sub))[::-1] if reverse else list(range(n_sub))
    chunk_order = list(range(nch))[::-1] if reverse else list(range(nch))
    for h in range(C_HEADS):
        sl = slice(h * LANES, (h + 1) * LANES)
        st = st_ref[h]
        for sg in sub_order:
            r0 = sg * gsz
            ke = ke_scr[r0:r0 + gsz, sl]
            vth = vt_ref[sl, r0:r0 + gsz]
            for ci in chunk_order:
                rows = slice(r0 + ci * C_CHUNK, r0 + (ci + 1) * C_CHUNK)
                o_scr[rows, sl] += lax.dot_general(qd_scr[rows, sl], st.astype(BF16), nt,
                                                   preferred_element_type=F32)
                ke_c = jnp.where(rowid == ci, ke, jnp.zeros_like(ke))
                upd = jnp.dot(vth, ke_c, preferred_element_type=F32)
                st = dec_scr[sg * nch + ci:sg * nch + ci + 1, sl] * st + upd
        st_ref[h] = st

    if not final:
        o_ref[...] = o_scr[...]
    else:
        for h in range(C_HEADS):
            sl = slice(h * LANES, (h + 1) * LANES)
            o = o_scr[:, sl] + of_ref[:, sl]
            ms = jnp.sum(o * o, axis=1, keepdims=True) * (1.0 / C_DV)
            y = o * lax.rsqrt(ms + EPS) * ng_ref[:, sl]
            rr = r_ref[:, sl].astype(F32)
            o_ref[:, sl] = (y * (rr * (1.0 / (1.0 + jnp.exp(-rr))))).astype(o_ref.dtype)

    @pl.when(step == pl.num_programs(1) - 1)
    def _():
        sout_ref[0] = st_ref[...]


def _gla_dir(cq, ck, cv, cvt, cg, w2p, bgp, st_in, nb, seq, ctx, reverse, latent, fwd_out=None, cr=None,
             ng=None):
    gsz = GLA_GROUP
    assert ctx == gsz
    final = fwd_out is not None
    w = C_HEADS * LANES
    if latent:
        n_sub = 4
        rows = n_sub * gsz
        nt_ = seq // rows

        def blk(b, i):
            return b * nt_ + ((nt_ - 1 - i) if reverse else i)
        sep_blk = blk
        grid = (nb, nt_)
        n_out = nb * seq
    else:
        n_sub = 1
        rows = gsz
        blk0 = nb * seq // gsz

        def blk(b, i):
            return blk0 + b

        def sep_blk(b, i):
            return b
        grid = (nb, 1)
        n_out = nb * ctx
    row_spec = pl.BlockSpec((rows, w), lambda b, i: (blk(b, i), 0))
    out_spec = pl.BlockSpec((rows, w), lambda b, i: (sep_blk(b, i), 0))
    st_spec = pl.BlockSpec((1, C_HEADS, LANES, LANES), lambda b, i: (b, 0, 0, 0))
    in_specs = [row_spec, row_spec, row_spec,
                pl.BlockSpec((w, rows), lambda b, i: (0, blk(b, i))),
                pl.BlockSpec((rows, LANES), lambda b, i: (blk(b, i), 0)),
                pl.BlockSpec((LANES, w), lambda b, i: (0, 0)),
                pl.BlockSpec((1, w), lambda b, i: (0, 0)),
                st_spec]
    args = [cq, ck, cv, cvt, cg, w2p, bgp, st_in]
    if final:
        in_specs += [out_spec, row_spec, pl.BlockSpec((1, w), lambda b, i: (0, 0))]
        args += [fwd_out, cr, ng]
    name = ("gla_bwd" if reverse else "gla_fwd") + ("_lat" if latent else "_ctx")
    return pl.pallas_call(
        functools.partial(_gla_kernel, reverse=reverse, final=final, n_sub=n_sub),
        grid=grid,
        in_specs=in_specs,
        out_specs=[out_spec, st_spec],
        out_shape=[jax.ShapeDtypeStruct((n_out, w), BF16 if final else F32),
                   jax.ShapeDtypeStruct(st_in.shape, F32)],
        scratch_shapes=[pltpu.VMEM((C_HEADS, LANES, LANES), F32),
                        pltpu.VMEM((rows, w), BF16), pltpu.VMEM((rows, w), BF16),
                        pltpu.VMEM((max(8, n_sub * (gsz // C_CHUNK)), w), F32),
                        pltpu.VMEM((rows, w), F32)],
        compiler_params=_cparams(("parallel", "arbitrary")),
        name=name,
    )(*args)


def _outproj_kernel(x_ref, a_ref, b_ref, cl_ref, cc_ref, wa_ref, wb_ref, wc_ref, g1_ref, o_ref, *, n_lat_tiles):
    i = pl.program_id(0)
    y = jnp.dot(a_ref[...], wa_ref[...], preferred_element_type=F32)
    y += jnp.dot(b_ref[...], wb_ref[...], preferred_element_type=F32)

    def finish(c):
        o_ref[...] = x_ref[...] + g1_ref[0] * (y + jnp.dot(c, wc_ref[...], preferred_element_type=F32))

    @pl.when(i < n_lat_tiles)
    def _():
        finish(cl_ref[...])

    @pl.when(i >= n_lat_tiles)
    def _():
        finish(cc_ref[...])


def _outproj(xs, a, b, c_lat, c_ctx, wa, wb, wc, mod, tm, modrow):
    t, d = xs.shape
    n_lat_tiles = c_lat.shape[0] // tm
    assert c_ctx.shape[0] == tm
    return pl.pallas_call(
        functools.partial(_outproj_kernel, n_lat_tiles=n_lat_tiles),
        grid=(t // tm,),
        in_specs=[pl.BlockSpec((tm, d), lambda i: (i, 0)),
                  pl.BlockSpec((tm, a.shape[1]), lambda i: (i, 0)),
                  pl.BlockSpec((tm, b.shape[1]), lambda i: (i, 0)),
                  pl.BlockSpec((tm, c_lat.shape[1]), lambda i: (jnp.minimum(i, n_lat_tiles - 1), 0)),
                  pl.BlockSpec((tm, c_ctx.shape[1]), lambda i: (0, 0)),
                  pl.BlockSpec(wa.shape, lambda i: (0, 0)),
                  pl.BlockSpec(wb.shape, lambda i: (0, 0)),
                  pl.BlockSpec(wc.shape, lambda i: (0, 0)),
                  pl.BlockSpec((1, 1, d), lambda i: (modrow(i), 0, 2))],
        out_specs=pl.BlockSpec((tm, d), lambda i: (i, 0)),
        out_shape=jax.ShapeDtypeStruct((t, d), F32),
        compiler_params=_cparams(("parallel",)),
        name="outproj",
    )(xs, a, b, c_lat, c_ctx, wa, wb, wc, mod)


def _ffn_kernel(x_ref, g_ref, sc_ref, sh_ref, gate_ref, wg_ref, wu_ref, wd_ref, o_ref, h_ref, acc_ref):
    f = pl.program_id(1)

    @pl.when(f == 0)
    def _():
        h_ref[...] = _norm_mod(x_ref[...], g_ref[...], sc_ref[0], sh_ref[0]).astype(BF16)
        acc_ref[...] = jnp.zeros(acc_ref.shape, F32)

    h = h_ref[...]
    a = jnp.dot(h, wg_ref[...].astype(BF16), preferred_element_type=F32)
    u = jnp.dot(h, wu_ref[...].astype(BF16), preferred_element_type=F32)
    act = a * (1.0 / (1.0 + jnp.exp(-a))) * u
    acc_ref[...] += jnp.dot(act.astype(BF16), wd_ref[...].astype(BF16), preferred_element_type=F32)

    @pl.when(f == pl.num_programs(1) - 1)
    def _():
        o_ref[...] = x_ref[...] + gate_ref[0] * acc_ref[...]


def _ffn(xs, g, mod, wg, wu, wd, tm, tf, modrow):
    t, d = xs.shape
    ff = wg.shape[1]
    return pl.pallas_call(
        _ffn_kernel,
        grid=(t // tm, ff // tf),
        in_specs=[pl.BlockSpec((tm, d), lambda i, f: (i, 0)),
                  pl.BlockSpec((1, d), lambda i, f: (0, 0)),
                  pl.BlockSpec((1, 1, d), lambda i, f: (modrow(i), 0, 4)),
                  pl.BlockSpec((1, 1, d), lambda i, f: (modrow(i), 0, 3)),
                  pl.BlockSpec((1, 1, d), lambda i, f: (modrow(i), 0, 5)),
                  pl.BlockSpec((d, tf), lambda i, f: (0, f)),
                  pl.BlockSpec((d, tf), lambda i, f: (0, f)),
                  pl.BlockSpec((tf, d), lambda i, f: (f, 0))],
        out_specs=pl.BlockSpec((tm, d), lambda i, f: (i, 0)),
        out_shape=jax.ShapeDtypeStruct((t, d), F32),
        scratch_shapes=[pltpu.VMEM((tm, d), BF16), pltpu.VMEM((tm, d), F32)],
        compiler_params=_cparams(("parallel", "arbitrary")),
        name="ffn_swiglu",
    )(xs, g.reshape(1, d), mod, mod, mod, wg, wu, wd)


def _router_kernel(x_ref, g_ref, sc_ref, sh_ref, rt_ref, h_ref, r_ref):
    lane = lax.broadcasted_iota(jnp.int32, (1, LANES), 1)
    hf = _norm_mod(x_ref[...], g_ref[...], sc_ref[0], sh_ref[0])
    h_ref[...] = hf
    logits = jnp.dot(hf, rt_ref[...], precision=lax.Precision.HIGHEST, preferred_element_type=F32)
    lanef = lane.astype(F32)
    lg = jnp.where(lane < N_EXPERTS, logits, NEG)
    m1 = jnp.max(lg, axis=1, keepdims=True)
    i1 = jnp.min(jnp.where(lg == m1, lanef, float(LANES)), axis=1, keepdims=True)
    lg2 = jnp.where(lanef == i1, NEG, lg)
    m2 = jnp.max(lg2, axis=1, keepdims=True)
    i2 = jnp.min(jnp.where(lg2 == m2, lanef, float(LANES)), axis=1, keepdims=True)
    e2 = jnp.exp(m2 - m1)
    w1 = 1.0 / (1.0 + e2)
    r_ref[...] = jnp.where(lane == 0, i1, jnp.where(lane == 1, i2, jnp.where(lane == 2, w1,
                           jnp.where(lane == 3, e2 * w1, 0.0))))


def _router(xs, g, mod, router_p, tm, modrow):
    t, d = xs.shape
    return pl.pallas_call(
        _router_kernel,
        grid=(t // tm,),
        in_specs=[pl.BlockSpec((tm, d), lambda i: (i, 0)),
                  pl.BlockSpec((1, d), lambda i: (0, 0)),
                  pl.BlockSpec((1, 1, d), lambda i: (modrow(i), 0, 4)),
                  pl.BlockSpec((1, 1, d), lambda i: (modrow(i), 0, 3)),
                  pl.BlockSpec((d, LANES), lambda i: (0, 0))],
        out_specs=[pl.BlockSpec((tm, d), lambda i: (i, 0)), pl.BlockSpec((tm, LANES), lambda i: (i, 0))],
        out_shape=[jax.ShapeDtypeStruct((t, d), F32), jax.ShapeDtypeStruct((t, LANES), F32)],
        compiler_params=_cparams(("parallel",)),
        name="moe_router",
    )(xs, g.reshape(1, d), mod, mod, router_p)


def _route_plan(rinfo, tg):
    t = rinfo.shape[0]
    n_tiles = -(-2 * t // tg) + N_EXPERTS
    e_flat = jnp.concatenate([rinfo[:, 0], rinfo[:, 1]]).astype(jnp.int32)
    onehot = (e_flat[:, None] == jnp.arange(N_EXPERTS, dtype=jnp.int32)[None, :]).astype(jnp.int32)
    csum = jnp.cumsum(onehot, axis=0)
    rank = jnp.sum(onehot * (csum - 1), axis=1)
    counts = csum[-1]
    padded = (counts + tg - 1) // tg * tg
    ends = jnp.cumsum(padded)
    pos = jnp.sum(onehot * (ends - padded)[None, :], axis=1) + rank
    tile_start = jnp.arange(n_tiles, dtype=jnp.int32) * tg
    tile_expert = jnp.minimum(jnp.sum((tile_start[:, None] >= ends[None, :]).astype(jnp.int32), axis=1),
                              N_EXPERTS - 1)
    n_used = (ends[-1] // tg).reshape(1)
    flat = jnp.arange(2 * t, dtype=jnp.int32)
    order = jnp.sort(e_flat * (2 * t) + flat) % (2 * t)
    unpadded_start = jnp.cumsum(counts) - counts
    row = jnp.arange(n_tiles * tg, dtype=jnp.int32)
    row_e = jnp.repeat(tile_expert, tg)
    r_in = row - (ends - padded)[row_e]
    src = order[jnp.minimum(unpadded_start[row_e] + r_in, 2 * t - 1)] % t
    src_tok = jnp.where(r_in < counts[row_e], src, 0)
    return src_tok, tile_expert, n_used, pos


def _row_copy(src_ref, row, dst_ref, r, sem):
    return pltpu.make_async_copy(src_ref.at[pl.ds(row, 1), :], dst_ref.at[pl.ds(r, 1), :], sem)


def _moe_experts_kernel(te_ref, nu_ref, idx_ref, idxn_ref, x_ref, wg_ref, wu_ref, wd_ref, o_ref,
                        xbuf, h_ref, acc_ref, sems, *, rows_per_step, n_steps):
    i = pl.program_id(0)
    f = pl.program_id(1)
    nf = pl.num_programs(1)
    last = nf - 1
    n_tiles = pl.num_programs(0)
    n_used = nu_ref[0]
    used = i < n_used
    slot = i % 2
    rows = h_ref.shape[0]
    per_step = rows_per_step
    n_dma = per_step * n_steps

    def drain(s):
        def body(r, carry):
            _row_copy(x_ref, 0, xbuf.at[s], r, sems.at[s]).wait()
            return carry
        lax.fori_loop(0, n_dma, body, 0, unroll=n_steps)

    @pl.when((i == 0) & (f == 0))
    def _():
        def body(r, carry):
            _row_copy(x_ref, idx_ref[0, 0, r], xbuf.at[0], r, sems.at[0]).start()
            return carry
        lax.fori_loop(0, n_dma, body, 0, unroll=n_steps)

    @pl.when((i <= n_used) & (f == 0))
    def _():
        drain(slot)

    @pl.when(used & (f == 0))
    def _():
        h_ref[...] = xbuf[slot, 0:rows].astype(BF16)
        acc_ref[...] = jnp.zeros(acc_ref.shape, F32)

    @pl.when(used)
    def _():
        for k in range(per_step):
            r = f * per_step + k
            _row_copy(x_ref, idxn_ref[0, 0, r], xbuf.at[1 - slot], r, sems.at[1 - slot]).start()
        h = h_ref[...]
        a = jnp.dot(h, wg_ref[0, 0].astype(BF16), preferred_element_type=F32)
        u = jnp.dot(h, wu_ref[0, 0].astype(BF16), preferred_element_type=F32)
        act = a * (1.0 / (1.0 + jnp.exp(-a))) * u
        acc_ref[...] += jnp.dot(act.astype(BF16), wd_ref[0, 0].astype(BF16), preferred_element_type=F32)

    @pl.when(used & (f == last))
    def _():
        o_ref[...] = acc_ref[...]

    @pl.when(used & (i == n_tiles - 1) & (f == last))
    def _():
        drain(1 - slot)

    @pl.when(jnp.logical_not(used) & (f == last))
    def _():
        o_ref[...] = jnp.zeros(o_ref.shape, F32)


def _moe_experts(h2, src_tok, tile_expert, n_used, wg, wu, wd, layer, tg, tf):
    d = h2.shape[1]
    n_tiles = src_tok.shape[0] // tg
    ff = wg.shape[3]
    nf = ff // tf
    per_step = -(-tg // nf)
    n_dma = per_step * nf
    idw = -(-n_dma // LANES) * LANES
    ids = jnp.pad(src_tok.reshape(n_tiles, 1, tg), ((0, 0), (0, 0), (0, idw - tg)))
    grid_spec = pltpu.PrefetchScalarGridSpec(
        num_scalar_prefetch=2,
        grid=(n_tiles, nf),
        in_specs=[pl.BlockSpec((1, 1, idw), lambda i, f, te, nu: (i, 0, 0), memory_space=pltpu.SMEM),
                  pl.BlockSpec((1, 1, idw), lambda i, f, te, nu: (jnp.minimum(i + 1, n_tiles - 1), 0, 0),
                               memory_space=pltpu.SMEM),
                  pl.BlockSpec(memory_space=pl.ANY),
                  pl.BlockSpec((1, 1, d, tf), lambda i, f, te, nu: (layer, te[i], 0, f)),
                  pl.BlockSpec((1, 1, d, tf), lambda i, f, te, nu: (layer, te[i], 0, f)),
                  pl.BlockSpec((1, 1, tf, d), lambda i, f, te, nu: (layer, te[i], f, 0))],
        out_specs=pl.BlockSpec((tg, d), lambda i, f, te, nu: (i, 0)),
        scratch_shapes=[pltpu.VMEM((2, -(-n_dma // 8) * 8, d), F32), pltpu.VMEM((tg, d), BF16),
                        pltpu.VMEM((tg, d), F32), pltpu.SemaphoreType.DMA((2,))])
    return pl.pallas_call(
        functools.partial(_moe_experts_kernel, rows_per_step=per_step, n_steps=nf),
        grid_spec=grid_spec,
        out_shape=jax.ShapeDtypeStruct((n_tiles * tg, d), F32),
        compiler_params=_cparams(("arbitrary", "arbitrary")),
        name="moe_experts",
    )(tile_expert, n_used, ids, ids, h2, wg, wu, wd)


def _moe_combine_kernel(i0_ref, i1_ref, n0_ref, n1_ref, y_ref, x_ref, r_ref, gate_ref, o_ref, buf, sems):
    i = pl.program_id(0)
    slot = i % 2
    rows = o_ref.shape[0]

    def fetch(a_ref, b_ref, s):
        def body(r, carry):
            _row_copy(y_ref, a_ref[0, 0, r], buf.at[s, 0], r, sems.at[s]).start(priority=0)
            _row_copy(y_ref, b_ref[0, 0, r], buf.at[s, 1], r, sems.at[s]).start(priority=1)
            return carry
        lax.fori_loop(0, rows, body, 0, unroll=8)

    @pl.when(i == 0)
    def _():
        fetch(i0_ref, i1_ref, 0)

    @pl.when(i + 1 < pl.num_programs(0))
    def _():
        fetch(n0_ref, n1_ref, 1 - slot)

    def drain(r, carry):
        _row_copy(y_ref, 0, buf.at[slot, 0], r, sems.at[slot]).wait()
        _row_copy(y_ref, 0, buf.at[slot, 1], r, sems.at[slot]).wait()
        return carry
    lax.fori_loop(0, rows, drain, 0, unroll=8)
    y = r_ref[:, 2:3] * buf[slot, 0] + r_ref[:, 3:4] * buf[slot, 1]
    o_ref[...] = x_ref[...] + gate_ref[0] * y


def _moe_combine(xs, yg, pos, rinfo, mod, tm, modrow):
    t, d = xs.shape
    nt = t // tm
    p3 = pos.reshape(2, nt, 1, tm)

    def cur(i):
        return (i, 0, 0)

    def nxt(i):
        return (jnp.minimum(i + 1, nt - 1), 0, 0)

    return pl.pallas_call(
        _moe_combine_kernel,
        grid=(nt,),
        in_specs=[pl.BlockSpec((1, 1, tm), cur, memory_space=pltpu.SMEM),
                  pl.BlockSpec((1, 1, tm), cur, memory_space=pltpu.SMEM),
                  pl.BlockSpec((1, 1, tm), nxt, memory_space=pltpu.SMEM),
                  pl.BlockSpec((1, 1, tm), nxt, memory_space=pltpu.SMEM),
                  pl.BlockSpec(memory_space=pl.ANY),
                  pl.BlockSpec((tm, d), lambda i: (i, 0)),
                  pl.BlockSpec((tm, LANES), lambda i: (i, 0)),
                  pl.BlockSpec((1, 1, d), lambda i: (modrow(i), 0, 5))],
        out_specs=pl.BlockSpec((tm, d), lambda i: (i, 0)),
        out_shape=jax.ShapeDtypeStruct((t, d), F32),
        scratch_shapes=[pltpu.VMEM((2, 2, tm, d), F32), pltpu.SemaphoreType.DMA((2,))],
        compiler_params=_cparams(("arbitrary",)),
        name="moe_combine",
    )(p3[0], p3[1], p3[0], p3[1], yg, xs, rinfo, mod)


def _moe(xs, g, mod, router_p, wg, wu, wd, layer, tm, tf, modrow):
    tg = tm
    h2, rinfo = _router(xs, g, mod, router_p, tm, modrow)
    src_tok, tile_expert, n_used, pos = _route_plan(rinfo, tg)
    yg = _moe_experts(h2, src_tok, tile_expert, n_used, wg, wu, wd, layer, tg, tf)
    return _moe_combine(xs, yg, pos, rinfo, mod, tm, modrow)


def _final_norm_kernel(x_ref, g_ref, o_ref):
    x = x_ref[...]
    ms = jnp.mean(x * x, axis=-1, keepdims=True)
    o_ref[...] = x * lax.rsqrt(ms + EPS) * g_ref[...]


def _final_norm(xs, g, rows, tm):
    d = xs.shape[1]
    return pl.pallas_call(
        _final_norm_kernel,
        grid=(rows // tm,),
        in_specs=[pl.BlockSpec((tm, d), lambda i: (i, 0)), pl.BlockSpec((1, d), lambda i: (0, 0))],
        out_specs=pl.BlockSpec((tm, d), lambda i: (i, 0)),
        out_shape=jax.ShapeDtypeStruct((rows, d), F32),
        compiler_params=_cparams(("parallel",)),
        name="final_norm",
    )(xs, g.reshape(1, d))


def _pad_heads(v, width, used):
    lead = v.shape[:-1]
    v = v.reshape(lead + (C_HEADS, used))
    v = jnp.pad(v, [(0, 0)] * len(lead) + [(0, 0), (0, width - used)])
    return v.reshape(lead + (C_HEADS * width,))


def kernel(x, c, ctx, c_ctx, norm1_g, norm2_g, ada_w, ada_b, w_in, w_out, a_lambda, a_norm_g, b_sink,
           c_gate_w2, c_gate_b, c_norm_g, ffn_w_gate, ffn_w_up, ffn_w_down, moe_router, moe_w_gate,
           moe_w_up, moe_w_down, final_g):
    nb, seq, d = x.shape
    nctx = ctx.shape[1]
    depth = w_in.shape[0]
    n_lat = nb * seq
    tm = nb * nctx
    assert seq % tm == 0 and nb < 8
    tm_r = tm // 2
    n_lat_tiles = n_lat // tm

    def modrow(i):
        return jnp.where(i < n_lat_tiles, i // (seq // tm), nb)

    xs = jnp.concatenate([x.reshape(n_lat, d), ctx.reshape(nb * nctx, d)], axis=0)
    cc = jnp.zeros((8, d), F32).at[:nb].set(c).at[nb].set(c_ctx)
    mod_all = _modulation(cc, ada_w, ada_b).reshape(depth, 8, 1, 6 * d)
    tables = _rope_tables(seq, tm_r)

    for layer in range(depth):
        lam_init = 0.8 - 0.6 * math.exp(-0.3 * layer)
        mod = mod_all[layer]
        w_pad = jnp.concatenate([w_in[layer], jnp.zeros((d, 1), F32)], axis=1)
        w_rope = jnp.take(w_pad, _ROPE_COLS, axis=1).astype(BF16)
        w_plain = jnp.take(w_pad, _PLAIN_COLS, axis=1).astype(BF16)
        w_vt = jnp.take(w_pad, _VT_COLS, axis=1).T.astype(BF16)

        aq, ak, bq, bk = _normproj(xs, norm1_g[layer], mod, 1, 0, w_rope, ROPE_WIDTHS, tm_r,
                                   n_lat // tm_r, seq // tm_r, nb, tables)
        cq, ck, cv, cr, cg, avt, bvt, cvt = _normproj(xs, norm1_g[layer], mod, 1, 0, w_plain, PLAIN_WIDTHS, tm,
                                                      n_lat_tiles, seq // tm, nb, wt=w_vt, vt_widths=VT_WIDTHS)

        a_all = _diff_attn(aq, ak, avt, a_lambda[layer], a_norm_g[layer], lam_init, nb, seq, nctx)
        sink_row = jnp.zeros((1, LANES), F32).at[0, :B_HEADS].set(b_sink[layer] * LOG2E)
        b_all = _win_attn(bq, bk, bvt, sink_row, nb, seq, nctx)

        w2 = c_gate_w2[layer]
        w2p = [jnp.zeros((LANES, C_HEADS * LANES), F32).at[dd * C_RANK:(dd + 1) * C_RANK].set(
            _pad_heads(w2[dd], LANES, C_DK)).astype(BF16) for dd in range(2)]
        bgp = [_pad_heads(c_gate_b[layer, dd], LANES, C_DK).reshape(1, -1) for dd in range(2)]
        ng = _pad_heads(jnp.tile(c_norm_g[layer], C_HEADS), LANES, C_DV).reshape(1, -1)
        st0 = jnp.zeros((nb, C_HEADS, LANES, LANES), F32)
        gla_in = (cq, ck, cv, cvt, cg)
        of_ctx, st_f = _gla_dir(*gla_in, w2p[0], bgp[0], st0, nb, seq, nctx, False, False)
        of_lat, _ = _gla_dir(*gla_in, w2p[0], bgp[0], st_f, nb, seq, nctx, False, True)
        g_ctx, st_b = _gla_dir(*gla_in, w2p[1], bgp[1], st0, nb, seq, nctx, True, False, of_ctx, cr, ng)
        g_lat, _ = _gla_dir(*gla_in, w2p[1], bgp[1], st_b, nb, seq, nctx, True, True, of_lat, cr, ng)

        wo = w_out[layer]
        wa = wo[:256].astype(BF16)
        wb = wo[256:640].astype(BF16)
        wc = jnp.pad(wo[640:].reshape(C_HEADS, C_DV, d), ((0, 0), (0, LANES - C_DV), (0, 0))).reshape(
            C_HEADS * LANES, d).astype(BF16)
        xs = _outproj(xs, a_all, b_all, g_lat, g_ctx, wa, wb, wc, mod, tm, modrow)

        j = layer // 2
        if layer % 2 == 0:
            xs = _ffn(xs, norm2_g[layer], mod, ffn_w_gate[j], ffn_w_up[j], ffn_w_down[j], tm, 256, modrow)
        else:
            router_p = jnp.pad(moe_router[j], ((0, 0), (0, LANES - N_EXPERTS)))
            xs = _moe(xs, norm2_g[layer], mod, router_p, moe_w_gate, moe_w_up, moe_w_down, j,
                      tm, 512, modrow)

    return _final_norm(xs, final_g, n_lat, tm).reshape(nb, seq, d)
```
